```python
import jax, jax.numpy as jnp
from jax import lax
import numpy as np

D_MODEL = 1024
BATCH = 32
SEQ = 2048
DEPTH = 2

HEAD_DIM = 64
N_HEADS_A = 8
N_HEADS_B = 8
WIDTH_AB = (N_HEADS_A + N_HEADS_B) * HEAD_DIM
IN_AB = 3 * WIDTH_AB + N_HEADS_B
DILATED_CONFIGS = ((128, 1), (512, 4), (2048, 16))
ROPE_THETA = 500000.0
ROPE_DIM = HEAD_DIM // 4
BLOCK = 128
HGRN_EXPAND = 128
N_HEADS_C = D_MODEL // HGRN_EXPAND
HGRN_DK = HGRN_EXPAND
HGRN_DV = D_MODEL // N_HEADS_C
HGRN_CHUNK = 64
D_FF = 2816
N_EXPERTS = 8
TOP_K = 2
D_FF_EXPERT = 3584
N_EVEN = (DEPTH + 1) // 2
N_ODD = DEPTH // 2
EPS = 1e-6

kernel_name = "hybrid_dilated_fox_hgrn2_moe"


def _rmsnorm(x, gain):
    x32 = x.astype(jnp.float32)
    y = x32 * lax.rsqrt(jnp.mean(x32 * x32, axis=-1, keepdims=True) + EPS)
    return (y * gain.astype(jnp.float32)).astype(x.dtype)


def _partial_rope(x, positions):
    half = ROPE_DIM // 2
    inv_freq = jnp.power(jnp.float32(ROPE_THETA), -jnp.arange(half, dtype=jnp.float32) / half)
    ang = positions.astype(jnp.float32)[..., None] * inv_freq
    cos = jnp.cos(ang)[:, :, None, :]
    sin = jnp.sin(ang)[:, :, None, :]
    xr = x[..., :ROPE_DIM].astype(jnp.float32)
    x1, x2 = xr[..., :half], xr[..., half:]
    rot = jnp.concatenate([x1 * cos - x2 * sin, x2 * cos + x1 * sin], axis=-1)
    return jnp.concatenate([rot.astype(x.dtype), x[..., ROPE_DIM:]], axis=-1)


def _dilated_branch(q, k, v, window, dilation):
    B_, H_, S_, Dh = q.shape
    r = dilation
    n = window // dilation
    assert n <= BLOCK
    L = S_ // r
    nb = -(-L // BLOCK)
    Lp = nb * BLOCK

    def to_sub(t):
        t = t.reshape(B_, H_, L, r, Dh).transpose(0, 1, 3, 2, 4)
        t = jnp.pad(t, ((0, 0), (0, 0), (0, 0), (0, Lp - L), (0, 0)))
        return t.reshape(B_, H_, r, nb, BLOCK, Dh)

    def band(t):
        prev = jnp.pad(t[:, :, :, :-1], ((0, 0), (0, 0), (0, 0), (1, 0), (0, 0), (0, 0)))
        return jnp.concatenate([prev, t], axis=-2)

    qs = to_sub(q)
    kb = band(to_sub(k))
    vb = band(to_sub(v))
    logits = jnp.einsum('bhrnqd,bhrnkd->bhrnqk', qs, kb).astype(jnp.float32)
    qi = jnp.arange(BLOCK)[:, None]
    kj = jnp.arange(2 * BLOCK)[None, :]
    dist = BLOCK + qi - kj
    key_pos = (jnp.arange(nb)[:, None, None] - 1) * BLOCK + kj[None]
    valid = (dist >= 0) & (dist <= n) & (key_pos >= 0)
    logits = jnp.where(valid, logits, -jnp.inf)
    m = jnp.max(logits, axis=-1, keepdims=True)
    e = jnp.exp(logits - m)
    l = jnp.sum(e, axis=-1, keepdims=True)
    o = jnp.einsum('bhrnqk,bhrnkd->bhrnqd', (e / l).astype(v.dtype), vb)
    lse = (m + jnp.log(l))[..., 0]

    def from_sub(t):
        extra = t.shape[5:]
        t = t.reshape((B_, H_, r, Lp) + extra)[:, :, :, :L]
        t = jnp.moveaxis(t, 2, 3)
        return t.reshape((B_, H_, S_) + extra)

    return from_sub(o), from_sub(lse)


def _forgetting_attention(q, k, v, logf):
    B_, H_, S_, Dh = q.shape
    nq = S_ // BLOCK
    c = jnp.cumsum(logf, axis=-1)
    qb = q.reshape(B_, H_, nq, BLOCK, Dh).transpose(2, 0, 1, 3, 4)
    cb = c.reshape(B_, H_, nq, BLOCK).transpose(2, 0, 1, 3)
    kpos = jnp.arange(S_)

    def one_block(args):
        qblk, cblk, j = args
        logits = jnp.einsum('bhqd,bhkd->bhqk', qblk, k).astype(jnp.float32)
        logits = logits + cblk[..., None] - c[:, :, None, :]
        qpos = j * BLOCK + jnp.arange(BLOCK)
        logits = jnp.where(kpos[None, :] <= qpos[:, None], logits, -jnp.inf)
        p = jax.nn.softmax(logits, axis=-1)
        return jnp.einsum('bhqk,bhkd->bhqd', p.astype(v.dtype), v)

    o = lax.map(one_block, (qb, cb, jnp.arange(nq)))
    return o.transpose(1, 2, 0, 3, 4).reshape(B_, H_, S_, Dh)


def _hgrn2_chunkwise(q, k, v, logf):
    B_, H_, S_, dk = q.shape
    dv = v.shape[-1]
    nc = S_ // HGRN_CHUNK

    def chunks(t):
        return t.reshape(B_, H_, nc, HGRN_CHUNK, t.shape[-1]).transpose(2, 0, 1, 3, 4)

    causal = jnp.tril(jnp.ones((HGRN_CHUNK, HGRN_CHUNK), dtype=bool))[:, :, None]

    def step(state, xs):
        qc, kc, vc, gc = xs
        b = jnp.cumsum(gc, axis=-2)
        o_inter = jnp.einsum('bhtd,bhde->bhte', qc * jnp.exp(b), state)
        rel = jnp.where(causal, b[:, :, :, None, :] - b[:, :, None, :, :], -jnp.inf)
        scores = jnp.einsum('bhtd,bhtsd,bhsd->bhts', qc, jnp.exp(rel), kc)
        o_intra = jnp.einsum('bhts,bhse->bhte', scores, vc)
        b_last = b[:, :, -1, :]
        new_state = jnp.exp(b_last)[..., None] * state + jnp.einsum(
            'bhsd,bhse->bhde', kc * jnp.exp(b_last[:, :, None, :] - b), vc)
        return new_state, o_inter + o_intra

    state0 = jnp.zeros((B_, H_, dk, dv), jnp.float32)
    _, o = lax.scan(step, state0, (chunks(q), chunks(k), chunks(v), chunks(logf)))
    return o.transpose(1, 2, 0, 3, 4).reshape(B_, H_, S_, dv)


def _mixer_ab(h, positions, w_in, f_bias, w_out):
    B_, S_, _ = h.shape
    proj = h @ w_in
    wa = N_HEADS_A * HEAD_DIM
    wb = N_HEADS_B * HEAD_DIM
    splits = np.cumsum([wa, wa, wa, wb, wb, wb]).tolist()
    qa, ka, va, qb, kb, vb, f_logit = jnp.split(proj, splits, axis=-1)
    scale = HEAD_DIM ** -0.5

    def heads(t, n):
        return t.reshape(B_, S_, n, HEAD_DIM)

    def bhsd(t):
        return t.transpose(0, 2, 1, 3)

    qa = bhsd(_partial_rope(heads(qa, N_HEADS_A), positions) * scale)
    ka = bhsd(_partial_rope(heads(ka, N_HEADS_A), positions))
    va = bhsd(heads(va, N_HEADS_A))
    outs, lses = [], []
    for window, dilation in DILATED_CONFIGS:
        o_i, lse_i = _dilated_branch(qa, ka, va, window, dilation)
        outs.append(o_i.astype(jnp.float32))
        lses.append(lse_i)
    mix = jax.nn.softmax(jnp.stack(lses, axis=0), axis=0)
    out_a = jnp.sum(mix[..., None] * jnp.stack(outs, axis=0), axis=0).astype(h.dtype)

    logf = jax.nn.log_sigmoid((f_logit + f_bias).astype(jnp.float32)).transpose(0, 2, 1)
    out_b = _forgetting_attention(bhsd(heads(qb, N_HEADS_B)) * scale,
                                  bhsd(heads(kb, N_HEADS_B)),
                                  bhsd(heads(vb, N_HEADS_B)), logf).astype(h.dtype)

    o = jnp.concatenate([out_a, out_b], axis=1)
    o = o.transpose(0, 2, 1, 3).reshape(B_, S_, WIDTH_AB)
    return o @ w_out


def _mixer_c(h, w_in, lower_bound, gnorm, w_out):
    B_, S_, _ = h.shape
    proj = h @ w_in
    q, f_logit, i_in, g = jnp.split(proj, 4, axis=-1)

    def bhsd(t, d):
        return t.reshape(B_, S_, N_HEADS_C, d).transpose(0, 2, 1, 3).astype(jnp.float32)

    lb = lower_bound.astype(jnp.float32).reshape(N_HEADS_C, HGRN_DK)[None, :, None, :]
    f = lb + (1.0 - lb) * jax.nn.sigmoid(bhsd(f_logit, HGRN_DK))
    logf = jnp.log(f)
    k = 1.0 - f
    o = _hgrn2_chunkwise(bhsd(q, HGRN_DK) * (HGRN_DK ** -0.5), k, bhsd(i_in, HGRN_DV), logf)
    o = _rmsnorm(o, gnorm)
    o = o.transpose(0, 2, 1, 3).reshape(B_, S_, D_MODEL).astype(h.dtype)
    return (o * jax.nn.silu(g)) @ w_out


def _swiglu(h, w_gate, w_up, w_down):
    return (jax.nn.silu(h @ w_gate) * (h @ w_up)) @ w_down


def _moe_swiglu(h, router, w_gate, w_up, w_down):
    logits = (h @ router).astype(jnp.float32)
    top_vals, top_idx = lax.top_k(logits, TOP_K)
    gates = jax.nn.softmax(top_vals, axis=-1)
    combine = jnp.sum(jax.nn.one_hot(top_idx, N_EXPERTS, dtype=jnp.float32) * gates[..., None], axis=-2)
    y = jnp.zeros(h.shape, jnp.float32)
    for e in range(N_EXPERTS):
        y = y + combine[..., e:e + 1] * _swiglu(h, w_gate[e], w_up[e], w_down[e]).astype(jnp.float32)
    return y.astype(h.dtype)


def setup_inputs(seed: int = 0) -> dict:
    key = jax.random.key(seed)
    ks = jax.random.split(key, 20)

    def w(k, shape, fan_in):
        return jax.random.normal(k, shape, jnp.float32) * fan_in ** -0.5

    def gain(k, shape):
        return 1.0 + 0.02 * jax.random.normal(k, shape, jnp.float32)

    x = jax.random.normal(ks[0], (BATCH, SEQ, D_MODEL), jnp.float32)
    offsets = jax.random.randint(ks[1], (BATCH, 1), 0, 4096, dtype=jnp.int32)
    positions = offsets + jnp.arange(SEQ, dtype=jnp.int32)[None, :]
    return {
        "x": x,
        "positions": positions,
        "norm_mix": gain(ks[2], (DEPTH, D_MODEL)),
        "norm_ffn": gain(ks[3], (DEPTH, D_MODEL)),
        "w_in_ab": w(ks[4], (N_EVEN, D_MODEL, IN_AB), D_MODEL),
        "fgate_bias": 3.0 + 0.5 * jax.random.normal(ks[5], (N_EVEN, N_HEADS_B), jnp.float32),
        "w_out_ab": w(ks[6], (N_EVEN, WIDTH_AB, D_MODEL), WIDTH_AB),
        "w_in_c": w(ks[7], (N_ODD, D_MODEL, 4 * D_MODEL), D_MODEL),
        "lower_bounds": 0.1 * jax.random.normal(ks[8], (DEPTH, D_MODEL), jnp.float32),
        "gnorm_c": gain(ks[9], (N_ODD, HGRN_DV)),
        "w_out_c": w(ks[10], (N_ODD, D_MODEL, D_MODEL), D_MODEL),
        "w_gate_ffn": w(ks[11], (N_EVEN, D_MODEL, D_FF), D_MODEL),
        "w_up_ffn": w(ks[12], (N_EVEN, D_MODEL, D_FF), D_MODEL),
        "w_down_ffn": w(ks[13], (N_EVEN, D_FF, D_MODEL), D_FF),
        "router": w(ks[14], (N_ODD, D_MODEL, N_EXPERTS), D_MODEL),
        "w_gate_moe": w(ks[15], (N_ODD, N_EXPERTS, D_MODEL, D_FF_EXPERT), D_MODEL),
        "w_up_moe": w(ks[16], (N_ODD, N_EXPERTS, D_MODEL, D_FF_EXPERT), D_MODEL),
        "w_down_moe": w(ks[17], (N_ODD, N_EXPERTS, D_FF_EXPERT, D_MODEL), D_FF_EXPERT),
        "norm_final": gain(ks[18], (D_MODEL,)),
    }


def reference(x, positions, norm_mix, norm_ffn, w_in_ab, fgate_bias, w_out_ab, w_in_c,
              lower_bounds, gnorm_c, w_out_c, w_gate_ffn, w_up_ffn, w_down_ffn, router,
              w_gate_moe, w_up_moe, w_down_moe, norm_final):
    lb_all = jnp.cumsum(jax.nn.softmax(lower_bounds.astype(jnp.float32), axis=0), axis=0)
    lb_all = lb_all - lb_all[0:1]
    for i in range(DEPTH):
        j = i // 2
        h = _rmsnorm(x, norm_mix[i])
        if i % 2 == 0:
            x = x + _mixer_ab(h, positions, w_in_ab[j], fgate_bias[j], w_out_ab[j])
            h = _rmsnorm(x, norm_ffn[i])
            x = x + _swiglu(h, w_gate_ffn[j], w_up_ffn[j], w_down_ffn[j])
        else:
            x = x + _mixer_c(h, w_in_c[j], lb_all[i], gnorm_c[j], w_out_c[j])
            h = _rmsnorm(x, norm_ffn[i])
            x = x + _moe_swiglu(h, router[j], w_gate_moe[j], w_up_moe[j], w_down_moe[j])
    return _rmsnorm(x, norm_final)
```

```python
import functools

import jax
import jax.numpy as jnp
from jax import lax
from jax.experimental import pallas as pl
from jax.experimental.pallas import tpu as pltpu

F32 = jnp.float32
BF16 = jnp.bfloat16

D_MODEL = 1024
HEAD_DIM = 64
N_HEADS_A = 8
N_HEADS_B = 8
WIDTH_A = N_HEADS_A * HEAD_DIM
WIDTH_B = N_HEADS_B * HEAD_DIM
QKV_WIDTH = 3 * (WIDTH_A + WIDTH_B)
ROPE_THETA = 500000.0
ROPE_DIM = HEAD_DIM // 4
ROPE_HALF = ROPE_DIM // 2
ATT_BLOCK = 128
DILATIONS = (1, 4, 16)
N_HEADS_C = 8
HGRN_DK = 128
HGRN_CHUNK = 64
HGRN_LEAF = 16
N_EXPERTS = 8
TOP_K = 2
EPS = 1e-6

LANES = 128
VMEM_LIMIT = 56 * 1024 * 1024

NT_DIMS = (((1,), (1,)), ((), ()))
TN_DIMS = (((0,), (0,)), ((), ()))


def _params(semantics, **kw):
    return pltpu.CompilerParams(dimension_semantics=semantics, vmem_limit_bytes=VMEM_LIMIT, **kw)


def _rmsnorm(x, gain):
    return x * lax.rsqrt(jnp.mean(x * x, axis=-1, keepdims=True) + EPS) * gain


def _sigmoid(x):
    return 1.0 / (1.0 + jnp.exp(-x))


def _split3(x):
    hi = x.astype(BF16)
    r1 = x - hi.astype(F32)
    mid = r1.astype(BF16)
    lo = (r1 - mid.astype(F32)).astype(BF16)
    return hi, mid, lo


def _dot(a, b):
    return jnp.dot(a, b, preferred_element_type=F32)


def _dot_nt(a, b):
    return lax.dot_general(a, b, NT_DIMS, preferred_element_type=F32)


def _cumsum_rows(tri, x):
    hi, mid, lo = _split3(x)
    return _dot(tri, hi) + _dot(tri, mid) + _dot(tri, lo)


def _inproj_ab_body(x_ref, gain_ref, w_ref, cos_ref, sin_ref, qkv_ref, f_ref):
    h = _rmsnorm(x_ref[...], gain_ref[...]).astype(BF16)
    cos = cos_ref[...]
    sin = sin_ref[...]
    lane = lax.broadcasted_iota(jnp.int32, cos.shape, 1)
    low = (lane & (HEAD_DIM - 1)) < ROPE_HALF
    scale = HEAD_DIM ** -0.5
    for c in range(QKV_WIDTH // 256):
        y = _dot(h, w_ref[:, c * 256:(c + 1) * 256])
        seg = c // 2
        for s in range(2):
            yy = y[:, s * LANES:(s + 1) * LANES]
            if seg in (0, 1):
                partner = jnp.where(low, pltpu.roll(yy, LANES - ROPE_HALF, 1), pltpu.roll(yy, ROPE_HALF, 1))
                yy = yy * cos + partner * sin
            if seg in (0, 3):
                yy = yy * scale
            qkv_ref[:, c * 256 + s * LANES:c * 256 + (s + 1) * LANES] = yy.astype(BF16)
    f_ref[...] = _dot(h, w_ref[:, QKV_WIDTH:QKV_WIDTH + LANES])


def _inproj_ab(x2d, gain, w, cos, sin, tm):
    T = x2d.shape[0]
    wn = w.shape[1]
    return pl.pallas_call(
        _inproj_ab_body,
        grid=(T // tm,),
        in_specs=[
            pl.BlockSpec((tm, D_MODEL), lambda i: (i, 0)),
            pl.BlockSpec((1, D_MODEL), lambda i: (0, 0)),
            pl.BlockSpec((D_MODEL, wn), lambda i: (0, 0)),
            pl.BlockSpec((tm, LANES), lambda i: (i, 0)),
            pl.BlockSpec((tm, LANES), lambda i: (i, 0)),
        ],
        out_specs=[
            pl.BlockSpec((tm, QKV_WIDTH), lambda i: (i, 0)),
            pl.BlockSpec((tm, LANES), lambda i: (i, 0)),
        ],
        out_shape=[
            jax.ShapeDtypeStruct((T, QKV_WIDTH), BF16),
            jax.ShapeDtypeStruct((T, LANES), F32),
        ],
        compiler_params=_params(("parallel",)),
        name="inproj_ab",
    )(x2d, gain, w, cos, sin)


def _dilated_body(q_ref, k_ref, v_ref, o_ref, qf, kf, vf, ob, lb):
    S = q_ref.shape[0]
    nb = ATT_BLOCK
    qf[...] = q_ref[...].astype(F32)
    kf[...] = k_ref[...].astype(F32)
    vf[...] = v_ref[...].astype(F32)
    lane = lax.broadcasted_iota(jnp.int32, (nb, LANES), 1)
    head0 = lane < HEAD_DIM
    qi2 = lax.broadcasted_iota(jnp.int32, (nb, 2 * nb), 0)
    kj2 = lax.broadcasted_iota(jnp.int32, (nb, 2 * nb), 1)
    valid2 = (kj2 >= qi2) & (kj2 <= qi2 + nb)
    qi1 = lax.broadcasted_iota(jnp.int32, (nb, nb), 0)
    kj1 = lax.broadcasted_iota(jnp.int32, (nb, nb), 1)
    valid1 = kj1 <= qi1

    def rows(start, size, r):
        return pl.ds(start, size) if r == 1 else pl.ds(start, size, stride=r)

    def block(br, r, q0, k0, nk):
        qs = qf[rows(q0, nb, r), :]
        kb = kf[rows(k0, nk, r), :].astype(BF16)
        vb = vf[rows(k0, nk, r), :].astype(BF16)
        valid = valid2 if nk == 2 * nb else valid1
        outs, lses = [], []
        for hh in range(2):
            own = head0 if hh == 0 else jnp.logical_not(head0)
            qm = jnp.where(own, qs, 0.0).astype(BF16)
            s = jnp.where(valid, _dot_nt(qm, kb), -jnp.inf)
            m = jnp.max(s, axis=-1, keepdims=True)
            e = jnp.exp(s - m)
            l = jnp.sum(e, axis=-1, keepdims=True)
            outs.append(_dot(e.astype(BF16), vb) / l)
            lses.append(m + jnp.log(l))
        ob[br, rows(q0, nb, r), :] = jnp.where(head0, outs[0], outs[1])
        lb[br, rows(q0, nb, r), :] = jnp.where(head0, lses[0], lses[1])

    for br, r in enumerate(DILATIONS):
        n_blocks = S // (r * nb)
        if n_blocks == 1:
            def first_only(c, carry, br=br, r=r):
                block(br, r, c, c, nb)
                return carry
            lax.fori_loop(0, r, first_only, 0)
        else:
            for c in range(r):
                block(br, r, c, c, nb)

                def later(n, carry, br=br, r=r, c=c):
                    block(br, r, n * (nb * r) + c, (n - 1) * (nb * r) + c, 2 * nb)
                    return carry
                lax.fori_loop(1, n_blocks, later, 0)

    rows_per = 256
    for ch in range(S // rows_per):
        sl = pl.ds(ch * rows_per, rows_per)
        l0, l1, l2 = lb[0, sl, :], lb[1, sl, :], lb[2, sl, :]
        m = jnp.maximum(jnp.maximum(l0, l1), l2)
        w0, w1, w2 = jnp.exp(l0 - m), jnp.exp(l1 - m), jnp.exp(l2 - m)
        o = (w0 * ob[0, sl, :] + w1 * ob[1, sl, :] + w2 * ob[2, sl, :]) / (w0 + w1 + w2)
        o_ref[sl, :] = o.astype(BF16)


def _dilated_attention(qkv, B, S):
    n_pairs = WIDTH_A // LANES
    blk = lambda off: pl.BlockSpec((None, S, LANES), lambda b, p: (b, 0, off + p))
    return pl.pallas_call(
        _dilated_body,
        grid=(B, n_pairs),
        in_specs=[blk(0), blk(n_pairs), blk(2 * n_pairs)],
        out_specs=pl.BlockSpec((None, S, LANES), lambda b, p: (b, 0, p)),
        out_shape=jax.ShapeDtypeStruct((B, S, WIDTH_A), BF16),
        scratch_shapes=[
            pltpu.VMEM((S, LANES), F32), pltpu.VMEM((S, LANES), F32), pltpu.VMEM((S, LANES), F32),
            pltpu.VMEM((len(DILATIONS), S, LANES), F32), pltpu.VMEM((len(DILATIONS), S, LANES), F32),
        ],
        compiler_params=_params(("parallel", "parallel")),
        name="dilated_attention",
    )(qkv, qkv, qkv)


def _fox_body(q_ref, k_ref, v_ref, f_ref, bias_ref, o_ref, c_scr, qa_scr, ka_scr, *, tile):
    S = q_ref.shape[0]
    p = pl.program_id(1)
    ri = lax.broadcasted_iota(jnp.int32, (LANES, LANES), 0)
    ci = lax.broadcasted_iota(jnp.int32, (LANES, LANES), 1)
    tri = (ri >= ci).astype(BF16)

    def cum_body(i, carry):
        x = f_ref[pl.ds(i * LANES, LANES), :] + bias_ref[...]
        logf = -(jnp.maximum(-x, 0.0) + jnp.log1p(jnp.exp(-jnp.abs(x))))
        c = _cumsum_rows(tri, logf) + carry
        c_scr[pl.ds(i * LANES, LANES), :] = c
        return c[LANES - 1:LANES, :]

    lax.fori_loop(0, S // LANES, cum_body, jnp.zeros((1, LANES), F32))

    rows_per = 256
    lane = lax.broadcasted_iota(jnp.int32, (rows_per, LANES), 1)
    for hh in range(2):
        head = 2 * p + hh
        own = (lane < HEAD_DIM) if hh == 0 else (lane >= HEAD_DIM)
        a0 = HEAD_DIM if hh == 0 else 0

        def build(i, carry, hh=hh, head=head, own=own, a0=a0):
            sl = pl.ds(i * rows_per, rows_per)
            ccol = jnp.sum(jnp.where(lane == head, c_scr[sl, :], 0.0), axis=1, keepdims=True)
            hi, mid, lo = (t.astype(F32) for t in _split3(ccol))
            q = q_ref[sl, :].astype(F32)
            k = k_ref[sl, :].astype(F32)
            ones_q = (lane >= a0 + 3) & (lane < a0 + 6)
            ones_k = (lane >= a0) & (lane < a0 + 3)
            qaug = jnp.where(own, q, jnp.where(lane == a0, hi, jnp.where(lane == a0 + 1, mid, jnp.where(
                lane == a0 + 2, lo, jnp.where(ones_q, 1.0, 0.0)))))
            kaug = jnp.where(own, k, jnp.where(lane == a0 + 3, -hi, jnp.where(lane == a0 + 4, -mid, jnp.where(
                lane == a0 + 5, -lo, jnp.where(ones_k, 1.0, 0.0)))))
            qa_scr[hh, sl, :] = qaug.astype(BF16)
            ka_scr[hh, sl, :] = kaug.astype(BF16)
            return carry

        lax.fori_loop(0, S // rows_per, build, 0)

    row_t = lax.broadcasted_iota(jnp.int32, (tile, tile), 0)
    col_t = lax.broadcasted_iota(jnp.int32, (tile, tile), 1)
    causal = col_t <= row_t
    head0 = lax.broadcasted_iota(jnp.int32, (tile, LANES), 1) < HEAD_DIM

    def q_body(qi, carry):
        qsl = pl.ds(qi * tile, tile)
        outs = []
        for hh in range(2):
            q = qa_scr[hh, qsl, :]

            def accumulate(state, s, vblk):
                m, l, acc = state
                m_new = jnp.maximum(m, jnp.max(s, axis=-1, keepdims=True))
                alpha = jnp.exp(m - m_new)
                e = jnp.exp(s - m_new)
                l = alpha * l + jnp.sum(e, axis=-1, keepdims=True)
                acc = alpha * acc + _dot(e.astype(BF16), vblk)
                return m_new, l, acc

            def k_body(kj, state, hh=hh, q=q):
                ksl = pl.ds(kj * tile, tile)
                return accumulate(state, _dot_nt(q, ka_scr[hh, ksl, :]), v_ref[ksl, :])

            init = (jnp.full((tile, 1), -jnp.inf, F32), jnp.zeros((tile, 1), F32), jnp.zeros((tile, LANES), F32))
            state = lax.fori_loop(0, qi, k_body, init)
            s_diag = jnp.where(causal, _dot_nt(q, ka_scr[hh, qsl, :]), -jnp.inf)
            _, l, acc = accumulate(state, s_diag, v_ref[qsl, :])
            outs.append(acc / l)
        o_ref[qsl, :] = jnp.where(head0, outs[0], outs[1]).astype(BF16)
        return carry

    lax.fori_loop(0, S // tile, q_body, 0)


def _fox_attention(qkv, flog, bias, B, S, tile=256):
    n_pairs = WIDTH_B // LANES
    base = 3 * (WIDTH_A // LANES)
    blk = lambda off: pl.BlockSpec((None, S, LANES), lambda b, p: (b, 0, off + p))
    return pl.pallas_call(
        functools.partial(_fox_body, tile=tile),
        grid=(B, n_pairs),
        in_specs=[
            blk(base), blk(base + n_pairs), blk(base + 2 * n_pairs),
            pl.BlockSpec((None, S, LANES), lambda b, p: (b, 0, 0)),
            pl.BlockSpec((1, LANES), lambda b, p: (0, 0)),
        ],
        out_specs=pl.BlockSpec((None, S, LANES), lambda b, p: (b, 0, p)),
        out_shape=jax.ShapeDtypeStruct((B, S, WIDTH_B), BF16),
        scratch_shapes=[
            pltpu.VMEM((S, LANES), F32),
            pltpu.VMEM((2, S, LANES), BF16), pltpu.VMEM((2, S, LANES), BF16),
        ],
        compiler_params=_params(("parallel", "parallel")),
        name="fox_attention",
    )(qkv, qkv, qkv, flog, bias)


def _outproj_ffn_body(x_ref, oa_ref, ob_ref, woa_ref, wob_ref, gain_ref, wg_ref, wu_ref, wd_ref, out_ref, h_scr):
    f = pl.program_id(1)

    @pl.when(f == 0)
    def _():
        x1 = x_ref[...] + _dot(oa_ref[...], woa_ref[...]) + _dot(ob_ref[...], wob_ref[...])
        out_ref[...] = x1
        h_scr[...] = _rmsnorm(x1, gain_ref[...]).astype(BF16)

    h = h_scr[...]
    g = _dot(h, wg_ref[...])
    u = _dot(h, wu_ref[...])
    a = (g * _sigmoid(g) * u).astype(BF16)
    out_ref[...] += _dot(a, wd_ref[...])


def _outproj_ffn(x2d, oa, ob, woa, wob, gain, wg, wu, wd, tm, tf):
    T = x2d.shape[0]
    F = wg.shape[1]
    return pl.pallas_call(
        _outproj_ffn_body,
        grid=(T // tm, F // tf),
        in_specs=[
            pl.BlockSpec((tm, D_MODEL), lambda i, f: (i, 0)),
            pl.BlockSpec((tm, WIDTH_A), lambda i, f: (i, 0)),
            pl.BlockSpec((tm, WIDTH_B), lambda i, f: (i, 0)),
            pl.BlockSpec((WIDTH_A, D_MODEL), lambda i, f: (0, 0)),
            pl.BlockSpec((WIDTH_B, D_MODEL), lambda i, f: (0, 0)),
            pl.BlockSpec((1, D_MODEL), lambda i, f: (0, 0)),
            pl.BlockSpec((D_MODEL, tf), lambda i, f: (0, f)),
            pl.BlockSpec((D_MODEL, tf), lambda i, f: (0, f)),
            pl.BlockSpec((tf, D_MODEL), lambda i, f: (f, 0)),
        ],
        out_specs=pl.BlockSpec((tm, D_MODEL), lambda i, f: (i, 0)),
        out_shape=jax.ShapeDtypeStruct((T, D_MODEL), F32),
        scratch_shapes=[pltpu.VMEM((tm, D_MODEL), BF16)],
        compiler_params=_params(("parallel", "arbitrary")),
        name="outproj_ffn",
    )(x2d, oa, ob, woa, wob, gain, wg, wu, wd)


def _inproj_c_body(x_ref, gain_ref, w_ref, qig_ref, f_ref):
    h = _rmsnorm(x_ref[...], gain_ref[...]).astype(BF16)
    n_qig = qig_ref.shape[1]
    for c in range(n_qig // 256):
        qig_ref[:, c * 256:(c + 1) * 256] = _dot(h, w_ref[:, c * 256:(c + 1) * 256]).astype(BF16)
    for c in range(f_ref.shape[1] // 256):
        f_ref[:, c * 256:(c + 1) * 256] = _dot(h, w_ref[:, n_qig + c * 256:n_qig + (c + 1) * 256])


def _inproj_c(x2d, gain, w, tm):
    T = x2d.shape[0]
    return pl.pallas_call(
        _inproj_c_body,
        grid=(T // tm,),
        in_specs=[
            pl.BlockSpec((tm, D_MODEL), lambda i: (i, 0)),
            pl.BlockSpec((1, D_MODEL), lambda i: (0, 0)),
            pl.BlockSpec((D_MODEL, 4 * D_MODEL), lambda i: (0, 0)),
        ],
        out_specs=[
            pl.BlockSpec((tm, 3 * D_MODEL), lambda i: (i, 0)),
            pl.BlockSpec((tm, D_MODEL), lambda i: (i, 0)),
        ],
        out_shape=[
            jax.ShapeDtypeStruct((T, 3 * D_MODEL), BF16),
            jax.ShapeDtypeStruct((T, D_MODEL), F32),
        ],
        compiler_params=_params(("parallel",)),
        name="inproj_c",
    )(x2d, gain, w)


def _hgrn_body(q_ref, i_ref, g_ref, f_ref, lb_ref, gn_ref, o_ref, state_scr):
    S = q_ref.shape[0]
    C = HGRN_CHUNK
    leaf = HGRN_LEAF
    lbv = lb_ref[...]
    gn = gn_ref[...]
    row = lax.broadcasted_iota(jnp.int32, (C, C), 0)
    col = lax.broadcasted_iota(jnp.int32, (C, C), 1)
    tri = (row >= col).astype(BF16)
    half = C // 2
    mask_a = (row >= half) & (col < half)
    mask_b = ((row // leaf) == (col // leaf) + 1) & ((row // half) == (col // half))
    mask_l = ((row // leaf) == (col // leaf)) & (col <= row)
    r = lax.broadcasted_iota(jnp.int32, (C, HGRN_DK), 0)
    scale = HGRN_DK ** -0.5
    state_scr[...] = jnp.zeros_like(state_scr)

    def chunk(ci, carry):
        sl = pl.ds(ci * C, C)
        f = lbv + (1.0 - lbv) * _sigmoid(f_ref[sl, :])
        k = 1.0 - f
        q = q_ref[sl, :].astype(F32) * scale
        v = i_ref[sl, :]
        b = _cumsum_rows(tri, jnp.log(f))
        p1, p2, p3 = b[leaf - 1:leaf, :], b[2 * leaf - 1:2 * leaf, :], b[3 * leaf - 1:3 * leaf, :]
        e_a = b - p2
        e_b = b - jnp.where(r < half, p1, p3)
        e_l = b - jnp.where(r < leaf, 0.0, jnp.where(r < 2 * leaf, p1, jnp.where(r < 3 * leaf, p2, p3)))
        s_a = _dot_nt((q * jnp.exp(jnp.minimum(e_a, 0.0))).astype(BF16),
                      (k * jnp.exp(jnp.minimum(-e_a, 0.0))).astype(BF16))
        s_b = _dot_nt((q * jnp.exp(jnp.minimum(e_b, 0.0))).astype(BF16),
                      (k * jnp.exp(jnp.minimum(-e_b, 0.0))).astype(BF16))
        s_l = _dot_nt((q * jnp.exp(e_l)).astype(BF16), (k * jnp.exp(-e_l)).astype(BF16))
        scores = jnp.where(mask_a, s_a, 0.0) + jnp.where(mask_b, s_b, 0.0) + jnp.where(mask_l, s_l, 0.0)
        o = _dot(scores.astype(BF16), v)
        state_t = state_scr[...]
        o = o + _dot_nt((q * jnp.exp(b)).astype(BF16), state_t.astype(BF16))
        b_last = b[C - 1:C, :]
        k_dec = (k * jnp.exp(b_last - b)).astype(BF16)
        v_t = v.astype(F32).T.astype(BF16)
        state_scr[...] = state_t * jnp.exp(b_last) + _dot(v_t, k_dec)
        y = _rmsnorm(o, gn)
        gate = g_ref[sl, :].astype(F32)
        o_ref[sl, :] = (y * (gate * _sigmoid(gate))).astype(BF16)
        return carry

    lax.fori_loop(0, S // C, chunk, 0)


def _hgrn(qig, flog, lb, gn, B, S):
    nh = N_HEADS_C
    blk = lambda off: pl.BlockSpec((None, S, HGRN_DK), lambda b, h: (b, 0, off + h))
    return pl.pallas_call(
        _hgrn_body,
        grid=(B, nh),
        in_specs=[
            blk(0), blk(nh), blk(2 * nh),
            pl.BlockSpec((None, S, HGRN_DK), lambda b, h: (b, 0, h)),
            pl.BlockSpec((None, 1, HGRN_DK), lambda b, h: (h, 0, 0)),
            pl.BlockSpec((1, HGRN_DK), lambda b, h: (0, 0)),
        ],
        out_specs=pl.BlockSpec((None, S, HGRN_DK), lambda b, h: (b, 0, h)),
        out_shape=jax.ShapeDtypeStruct((B, S, D_MODEL), BF16),
        scratch_shapes=[pltpu.VMEM((HGRN_DK, HGRN_DK), F32)],
        compiler_params=_params(("parallel", "parallel")),
        name="hgrn2",
    )(qig, qig, qig, flog, lb, gn)


def _outproj_router_body(x_ref, o_ref, w_ref, gain_ref, rhi_ref, rlo_ref, x3_ref, h_ref, idx_ref, gate_ref):
    x3 = x_ref[...] + _dot(o_ref[...], w_ref[...])
    x3_ref[...] = x3
    h = _rmsnorm(x3, gain_ref[...])
    h_ref[...] = h
    h_hi = h.astype(BF16)
    h_lo = (h - h_hi.astype(F32)).astype(BF16)
    logits = _dot(h_hi, rhi_ref[...]) + (_dot(h_lo, rhi_ref[...]) + _dot(h_hi, rlo_ref[...]))
    lane = lax.broadcasted_iota(jnp.int32, logits.shape, 1)
    lane_f = lane.astype(F32)
    lg = jnp.where(lane < N_EXPERTS, logits, -jnp.inf)
    m1 = jnp.max(lg, axis=-1, keepdims=True)
    i1 = jnp.min(jnp.where(lg == m1, lane_f, float(LANES)), axis=-1, keepdims=True)
    lg2 = jnp.where(lane_f == i1, -jnp.inf, lg)
    m2 = jnp.max(lg2, axis=-1, keepdims=True)
    i2 = jnp.min(jnp.where(lg2 == m2, lane_f, float(LANES)), axis=-1, keepdims=True)
    e2 = jnp.exp(m2 - m1)
    den = 1.0 + e2
    idx_ref[...] = jnp.where(lane == 0, i1, jnp.where(lane == 1, i2, 0.0)).astype(jnp.int32)
    gate_ref[...] = jnp.where(lane == 0, 1.0 / den, jnp.where(lane == 1, e2 / den, 0.0))


def _outproj_router(x2d, o, w, gain, rhi, rlo, tm):
    T = x2d.shape[0]
    return pl.pallas_call(
        _outproj_router_body,
        grid=(T // tm,),
        in_specs=[
            pl.BlockSpec((tm, D_MODEL), lambda i: (i, 0)),
            pl.BlockSpec((tm, D_MODEL), lambda i: (i, 0)),
            pl.BlockSpec((D_MODEL, D_MODEL), lambda i: (0, 0)),
            pl.BlockSpec((1, D_MODEL), lambda i: (0, 0)),
            pl.BlockSpec((D_MODEL, LANES), lambda i: (0, 0)),
            pl.BlockSpec((D_MODEL, LANES), lambda i: (0, 0)),
        ],
        out_specs=[
            pl.BlockSpec((tm, D_MODEL), lambda i: (i, 0)),
            pl.BlockSpec((tm, D_MODEL), lambda i: (i, 0)),
            pl.BlockSpec((tm, LANES), lambda i: (i, 0)),
            pl.BlockSpec((tm, LANES), lambda i: (i, 0)),
        ],
        out_shape=[
            jax.ShapeDtypeStruct((T, D_MODEL), F32),
            jax.ShapeDtypeStruct((T, D_MODEL), F32),
            jax.ShapeDtypeStruct((T, LANES), jnp.int32),
            jax.ShapeDtypeStruct((T, LANES), F32),
        ],
        compiler_params=_params(("parallel",)),
        name="outproj_router",
    )(x2d, o, w, gain, rhi, rlo)


def _row_copy(src_hbm, src_row, dst_ref, dst_row, sem):
    return pltpu.make_async_copy(src_hbm.at[pl.ds(src_row, 1), :], dst_ref.at[pl.ds(dst_row, 1), :], sem)


def _gather_body(tok_ref, h_hbm, out_ref, sem):
    tm = out_ref.shape[0]

    def issue(r, carry):
        _row_copy(h_hbm, tok_ref[r], out_ref, r, sem).start()
        return carry

    lax.fori_loop(0, tm, issue, 0)
    pltpu.make_async_copy(h_hbm.at[pl.ds(0, tm), :], out_ref, sem).wait()


def _moe_gather(row_token, h, tm):
    R = row_token.shape[0]
    return pl.pallas_call(
        _gather_body,
        grid=(R // tm,),
        in_specs=[
            pl.BlockSpec((tm,), lambda i: (i,), memory_space=pltpu.SMEM),
            pl.BlockSpec(memory_space=pl.ANY),
        ],
        out_specs=pl.BlockSpec((tm, D_MODEL), lambda i: (i, 0)),
        out_shape=jax.ShapeDtypeStruct((R, D_MODEL), F32),
        scratch_shapes=[pltpu.SemaphoreType.DMA(())],
        compiler_params=_params(("arbitrary",), disable_bounds_checks=True),
        name="moe_gather",
    )(row_token, h)


def _experts_body(te_ref, nu_ref, xs_ref, wg_ref, wu_ref, wd_ref, gate_ref, out_ref, xb_scr):
    i = pl.program_id(0)
    f = pl.program_id(1)
    nf = pl.num_programs(1)

    @pl.when(f == 0)
    def _():
        xb_scr[...] = xs_ref[...].astype(BF16)
        out_ref[...] = jnp.zeros_like(out_ref)

    @pl.when(i < nu_ref[0])
    def _():
        xb = xb_scr[...]
        g = _dot(xb, wg_ref[...])
        u = _dot(xb, wu_ref[...])
        a = (g * _sigmoid(g) * u).astype(BF16)
        out_ref[...] += _dot(a, wd_ref[...])

    @pl.when(f == nf - 1)
    def _():
        out_ref[...] = out_ref[...] * gate_ref[...]


def _moe_experts(tile_expert, n_used, xs, wg, wu, wd, row_gate, tm, tf):
    R = xs.shape[0]
    F = wg.shape[2]
    nf = F // tf

    def f_eff(i, f, nu):
        return jnp.where(i < nu[0], f, nf - 1)

    return pl.pallas_call(
        _experts_body,
        grid_spec=pltpu.PrefetchScalarGridSpec(
            num_scalar_prefetch=2,
            grid=(R // tm, nf),
            in_specs=[
                pl.BlockSpec((tm, D_MODEL), lambda i, f, te, nu: (i, 0)),
                pl.BlockSpec((None, D_MODEL, tf), lambda i, f, te, nu: (te[i], 0, f_eff(i, f, nu))),
                pl.BlockSpec((None, D_MODEL, tf), lambda i, f, te, nu: (te[i], 0, f_eff(i, f, nu))),
                pl.BlockSpec((None, tf, D_MODEL), lambda i, f, te, nu: (te[i], f_eff(i, f, nu), 0)),
                pl.BlockSpec((tm, 1), lambda i, f, te, nu: (i, 0)),
            ],
            out_specs=pl.BlockSpec((tm, D_MODEL), lambda i, f, te, nu: (i, 0)),
            scratch_shapes=[pltpu.VMEM((tm, D_MODEL), BF16)],
        ),
        out_shape=jax.ShapeDtypeStruct((R, D_MODEL), F32),
        compiler_params=_params(("arbitrary", "arbitrary")),
        name="moe_experts",
    )(tile_expert, n_used, xs, wg, wu, wd, row_gate)


def _combine_body(dest_ref, x_ref, ys_hbm, gain_ref, out_ref, buf, sem):
    tm = x_ref.shape[0]

    def issue(r, carry):
        for c in range(TOP_K):
            _row_copy(ys_hbm, dest_ref[TOP_K * r + c], buf.at[c], r, sem).start()
        return carry

    lax.fori_loop(0, tm, issue, 0)
    for c in range(TOP_K):
        pltpu.make_async_copy(ys_hbm.at[pl.ds(0, tm), :], buf.at[c], sem).wait()
    y = x_ref[...] + (buf[0] + buf[1])
    out_ref[...] = _rmsnorm(y, gain_ref[...])


def _moe_combine(dest, x3, ys, gain, tm):
    T = x3.shape[0]
    return pl.pallas_call(
        _combine_body,
        grid=(T // tm,),
        in_specs=[
            pl.BlockSpec((TOP_K * tm,), lambda i: (i,), memory_space=pltpu.SMEM),
            pl.BlockSpec((tm, D_MODEL), lambda i: (i, 0)),
            pl.BlockSpec(memory_space=pl.ANY),
            pl.BlockSpec((1, D_MODEL), lambda i: (0, 0)),
        ],
        out_specs=pl.BlockSpec((tm, D_MODEL), lambda i: (i, 0)),
        out_shape=jax.ShapeDtypeStruct((T, D_MODEL), F32),
        scratch_shapes=[pltpu.VMEM((TOP_K, tm, D_MODEL), F32), pltpu.SemaphoreType.DMA(())],
        compiler_params=_params(("arbitrary",), disable_bounds_checks=True),
        name="moe_combine",
    )(dest, x3, ys, gain)


def _routing_tables(idx, gates, tm):
    T = idx.shape[0]
    n_assign = TOP_K * T
    flat_e = idx.reshape(n_assign)
    onehot = (flat_e[:, None] == jnp.arange(N_EXPERTS, dtype=jnp.int32)[None, :]).astype(jnp.int32)
    rank = jnp.sum((jnp.cumsum(onehot, axis=0) - onehot) * onehot, axis=1)
    counts = jnp.sum(onehot, axis=0)
    padded = ((counts + tm - 1) // tm) * tm
    ends = jnp.cumsum(padded)
    starts = ends - padded
    dest = (starts[flat_e] + rank).astype(jnp.int32)
    n_rows = n_assign + N_EXPERTS * tm
    row_token = jnp.zeros((n_rows,), jnp.int32).at[dest].set(jnp.arange(n_assign, dtype=jnp.int32) // TOP_K)
    row_gate = jnp.zeros((n_rows,), F32).at[dest].set(gates.reshape(n_assign))
    tile_start = jnp.arange(n_rows // tm, dtype=jnp.int32) * tm
    tile_expert = jnp.minimum(jnp.searchsorted(ends, tile_start, side="right"), N_EXPERTS - 1).astype(jnp.int32)
    n_used = (ends[-1] // tm).astype(jnp.int32).reshape(1)
    return dest, row_token, row_gate.reshape(n_rows, 1), tile_expert, n_used


def _rope_tables(positions):
    B, S = positions.shape
    inv_freq = jnp.power(jnp.float32(ROPE_THETA), -jnp.arange(ROPE_HALF, dtype=F32) / ROPE_HALF)
    ang = positions.astype(F32)[..., None] * inv_freq
    cos, sin = jnp.cos(ang), jnp.sin(ang)
    rest = HEAD_DIM - ROPE_DIM
    cos_h = jnp.concatenate([cos, cos, jnp.ones((B, S, rest), F32)], axis=-1)
    sin_h = jnp.concatenate([-sin, sin, jnp.zeros((B, S, rest), F32)], axis=-1)
    reps = LANES // HEAD_DIM
    return (jnp.tile(cos_h, (1, 1, reps)).reshape(B * S, LANES),
            jnp.tile(sin_h, (1, 1, reps)).reshape(B * S, LANES))


def kernel(x, positions, norm_mix, norm_ffn, w_in_ab, fgate_bias, w_out_ab, w_in_c, lower_bounds, gnorm_c, w_out_c,
           w_gate_ffn, w_up_ffn, w_down_ffn, router, w_gate_moe, w_up_moe, w_down_moe, norm_final):
    B, S, D = x.shape
    T = B * S
    assert D == D_MODEL and S % 256 == 0 and T % 1024 == 0
    tm = 512
    x2d = x.reshape(T, D)

    w_ab = jnp.pad(w_in_ab[0], ((0, 0), (0, QKV_WIDTH + LANES - w_in_ab.shape[2]))).astype(BF16)
    cos, sin = _rope_tables(positions)
    qkv, flog = _inproj_ab(x2d, norm_mix[0:1], w_ab, cos, sin, tm)
    qkv = qkv.reshape(B, S, QKV_WIDTH)
    bias = jnp.pad(fgate_bias[0], (0, LANES - N_HEADS_B)).reshape(1, LANES)
    out_a = _dilated_attention(qkv, B, S).reshape(T, WIDTH_A)
    out_b = _fox_attention(qkv, flog.reshape(B, S, LANES), bias, B, S).reshape(T, WIDTH_B)
    w_o = w_out_ab[0].astype(BF16)
    d_ff = w_gate_ffn.shape[2]
    x2 = _outproj_ffn(x2d, out_a, out_b, w_o[:WIDTH_A], w_o[WIDTH_A:], norm_ffn[0:1],
                      w_gate_ffn[0].astype(BF16), w_up_ffn[0].astype(BF16), w_down_ffn[0].astype(BF16),
                      tm, d_ff // 2)

    lb_all = jnp.cumsum(jax.nn.softmax(lower_bounds.astype(F32), axis=0), axis=0)
    lb = (lb_all - lb_all[0:1])[1].reshape(N_HEADS_C, 1, HGRN_DK)
    wq, wf, wi, wg = jnp.split(w_in_c[0], 4, axis=-1)
    w_c = jnp.concatenate([wq, wi, wg, wf], axis=-1).astype(BF16)
    qig, flog_c = _inproj_c(x2, norm_mix[1:2], w_c, tm)
    o_c = _hgrn(qig.reshape(B, S, 3 * D), flog_c.reshape(B, S, D), lb, gnorm_c[0:1], B, S).reshape(T, D)

    r_pad = jnp.pad(router[0], ((0, 0), (0, LANES - N_EXPERTS)))
    r_hi = r_pad.astype(BF16)
    r_lo = (r_pad - r_hi.astype(F32)).astype(BF16)
    x3, h3, idx, gates = _outproj_router(x2, o_c, w_out_c[0].astype(BF16), norm_ffn[1:2], r_hi, r_lo, tm)
    tm_e = 1024
    dest, row_token, row_gate, tile_expert, n_used = _routing_tables(idx[:, :TOP_K], gates[:, :TOP_K], tm_e)
    xs = _moe_gather(row_token, h3, tm_e)
    ys = _moe_experts(tile_expert, n_used, xs, w_gate_moe[0].astype(BF16), w_up_moe[0].astype(BF16),
                      w_down_moe[0].astype(BF16), row_gate, tm_e, 512)
    out = _moe_combine(dest, x3, ys, norm_final.reshape(1, D), tm)
    return out.reshape(B, S, D)
```

```python
import functools

import jax
import jax.numpy as jnp
from jax import lax
from jax.experimental import pallas as pl
from jax.experimental.pallas import tpu as pltpu

F32 = jnp.float32
BF16 = jnp.bfloat16

D_MODEL = 1024
HEAD_DIM = 64
N_HEADS_A = 8
N_HEADS_B = 8
WIDTH_A = N_HEADS_A * HEAD_DIM
WIDTH_B = N_HEADS_B * HEAD_DIM
QKV_WIDTH = 3 * (WIDTH_A + WIDTH_B)
ROPE_THETA = 500000.0
ROPE_DIM = HEAD_DIM // 4
ROPE_HALF = ROPE_DIM // 2
ATT_BLOCK = 128
DILATIONS = (1, 4, 16)
N_HEADS_C = 8
HGRN_DK = 128
HGRN_CHUNK = 64
HGRN_LEAF = 16
N_EXPERTS = 8
TOP_K = 2
EPS = 1e-6

LANES = 128
SUBLANES = 8
VMEM_LIMIT = 56 * 1024 * 1024

NT_DIMS = (((1,), (1,)), ((), ()))


def _params(semantics, **kw):
    return pltpu.CompilerParams(dimension_semantics=semantics, vmem_limit_bytes=VMEM_LIMIT, **kw)


def _rmsnorm(x, gain):
    return x * lax.rsqrt(jnp.mean(x * x, axis=-1, keepdims=True) + EPS) * gain


def _sigmoid(x):
    return 1.0 / (1.0 + jnp.exp(-x))


def _split3(x):
    hi = x.astype(BF16)
    r1 = x - hi.astype(F32)
    mid = r1.astype(BF16)
    lo = (r1 - mid.astype(F32)).astype(BF16)
    return hi, mid, lo


def _dot(a, b):
    return jnp.dot(a, b, preferred_element_type=F32)


def _dot_nt(a, b):
    return lax.dot_general(a, b, NT_DIMS, preferred_element_type=F32)


def _cumsum_groups(x):
    n, w = x.shape
    rows = lax.broadcasted_iota(jnp.int32, (SUBLANES, w), 0)
    out, carry = [], None
    for g in range(n // SUBLANES):
        xg = x[SUBLANES * g:SUBLANES * (g + 1), :]
        for s in (1, 2, 4):
            xg = xg + jnp.where(rows >= s, pltpu.roll(xg, s, 0), 0.0)
        if carry is not None:
            xg = xg + carry
        carry = xg[SUBLANES - 1:SUBLANES, :]
        out.append(xg)
    return jnp.concatenate(out, axis=0)


def _inproj_ab_body(x_ref, gain_ref, w_ref, cos_ref, sin_ref, bias_ref, qkv_ref, cum_ref, carry_scr, *, tiles_per_seq):
    h = _rmsnorm(x_ref[...], gain_ref[...]).astype(BF16)
    cos = cos_ref[...]
    sin = sin_ref[...]
    lane = lax.broadcasted_iota(jnp.int32, cos.shape, 1)
    low = (lane & (HEAD_DIM - 1)) < ROPE_HALF
    scale = HEAD_DIM ** -0.5
    for c in range(QKV_WIDTH // 256):
        y = _dot(h, w_ref[:, c * 256:(c + 1) * 256])
        seg = c // 2
        for s in range(2):
            yy = y[:, s * LANES:(s + 1) * LANES]
            if seg in (0, 1):
                partner = jnp.where(low, pltpu.roll(yy, LANES - ROPE_HALF, 1), pltpu.roll(yy, ROPE_HALF, 1))
                yy = yy * cos + partner * sin
            if seg in (0, 3):
                yy = yy * scale
            qkv_ref[:, c * 256 + s * LANES:c * 256 + (s + 1) * LANES] = yy.astype(BF16)

    @pl.when(pl.program_id(0) % tiles_per_seq == 0)
    def _():
        carry_scr[...] = jnp.zeros_like(carry_scr)

    x = _dot(h, w_ref[:, QKV_WIDTH:QKV_WIDTH + LANES]) + bias_ref[...]
    logf = -(jnp.maximum(-x, 0.0) + jnp.log1p(jnp.exp(-jnp.abs(x))))
    cum = _cumsum_groups(logf) + carry_scr[0:1, :]
    cum_ref[...] = cum
    carry_scr[0:1, :] = cum[cum.shape[0] - 1:, :]


def _inproj_ab(x2d, gain, w, cos, sin, bias, tm, seq_len):
    T = x2d.shape[0]
    wn = w.shape[1]
    return pl.pallas_call(
        functools.partial(_inproj_ab_body, tiles_per_seq=seq_len // tm),
        grid=(T // tm,),
        in_specs=[
            pl.BlockSpec((tm, D_MODEL), lambda i: (i, 0)),
            pl.BlockSpec((1, D_MODEL), lambda i: (0, 0)),
            pl.BlockSpec((D_MODEL, wn), lambda i: (0, 0)),
            pl.BlockSpec((tm, LANES), lambda i: (i, 0)),
            pl.BlockSpec((tm, LANES), lambda i: (i, 0)),
            pl.BlockSpec((1, LANES), lambda i: (0, 0)),
        ],
        out_specs=[
            pl.BlockSpec((tm, QKV_WIDTH), lambda i: (i, 0)),
            pl.BlockSpec((tm, LANES), lambda i: (i, 0)),
        ],
        out_shape=[
            jax.ShapeDtypeStruct((T, QKV_WIDTH), BF16),
            jax.ShapeDtypeStruct((T, LANES), F32),
        ],
        scratch_shapes=[pltpu.VMEM((SUBLANES, LANES), F32)],
        compiler_params=_params(("arbitrary",)),
        name="inproj_ab",
    )(x2d, gain, w, cos, sin, bias)


def _dilated_body(q_ref, k_ref, v_ref, o_ref, qf, kf, vf, ob, lb):
    S = q_ref.shape[0]
    nb = ATT_BLOCK
    qf[...] = q_ref[...].astype(F32)
    kf[...] = k_ref[...].astype(F32)
    vf[...] = v_ref[...].astype(F32)
    head0 = lax.broadcasted_iota(jnp.int32, (nb, LANES), 1) < HEAD_DIM
    qi2 = lax.broadcasted_iota(jnp.int32, (2 * nb, 2 * nb), 0) & (nb - 1)
    kj2 = lax.broadcasted_iota(jnp.int32, (2 * nb, 2 * nb), 1)
    valid2 = (kj2 >= qi2) & (kj2 <= qi2 + nb)
    qi1 = lax.broadcasted_iota(jnp.int32, (2 * nb, nb), 0) & (nb - 1)
    kj1 = lax.broadcasted_iota(jnp.int32, (2 * nb, nb), 1)
    valid1 = kj1 <= qi1

    def rows(start, size, r):
        return pl.ds(start, size) if r == 1 else pl.ds(start, size, stride=r)

    def block(br, r, q0, k0, nk):
        qs = qf[rows(q0, nb, r), :]
        kb = kf[rows(k0, nk, r), :].astype(BF16)
        vb = vf[rows(k0, nk, r), :].astype(BF16)
        q2 = jnp.concatenate([jnp.where(head0, qs, 0.0), jnp.where(head0, 0.0, qs)], axis=0).astype(BF16)
        s = jnp.where(valid2 if nk == 2 * nb else valid1, _dot_nt(q2, kb), -jnp.inf)
        m = jnp.max(s, axis=-1, keepdims=True)
        e = jnp.exp(s - m)
        l = jnp.sum(e, axis=-1, keepdims=True)
        o = _dot(e.astype(BF16), vb) / l
        lse = jnp.broadcast_to(m + jnp.log(l), (2 * nb, LANES))
        ob[br, rows(q0, nb, r), :] = jnp.where(head0, o[:nb], o[nb:])
        lb[br, rows(q0, nb, r), :] = jnp.where(head0, lse[:nb], lse[nb:])

    for br, r in enumerate(DILATIONS):
        n_blocks = S // (r * nb)
        if n_blocks == 1:
            def first_only(c, carry, br=br, r=r):
                block(br, r, c, c, nb)
                return carry
            lax.fori_loop(0, r, first_only, 0, unroll=8)
        else:
            for c in range(r):
                block(br, r, c, c, nb)

                def later(n, carry, br=br, r=r, c=c):
                    block(br, r, n * (nb * r) + c, (n - 1) * (nb * r) + c, 2 * nb)
                    return carry
                lax.fori_loop(1, n_blocks, later, 0, unroll=5 if (n_blocks - 1) % 5 == 0 else 3)

    rows_per = 256
    for ch in range(S // rows_per):
        sl = pl.ds(ch * rows_per, rows_per)
        l0, l1, l2 = lb[0, sl, :], lb[1, sl, :], lb[2, sl, :]
        m = jnp.maximum(jnp.maximum(l0, l1), l2)
        w0, w1, w2 = jnp.exp(l0 - m), jnp.exp(l1 - m), jnp.exp(l2 - m)
        o = (w0 * ob[0, sl, :] + w1 * ob[1, sl, :] + w2 * ob[2, sl, :]) / (w0 + w1 + w2)
        o_ref[sl, :] = o.astype(BF16)


def _dilated_attention(qkv, B, S):
    n_pairs = WIDTH_A // LANES
    blk = lambda off: pl.BlockSpec((None, S, LANES), lambda b, p: (b, 0, off + p))
    return pl.pallas_call(
        _dilated_body,
        grid=(B, n_pairs),
        in_specs=[blk(0), blk(n_pairs), blk(2 * n_pairs)],
        out_specs=pl.BlockSpec((None, S, LANES), lambda b, p: (b, 0, p)),
        out_shape=jax.ShapeDtypeStruct((B, S, WIDTH_A), BF16),
        scratch_shapes=[
            pltpu.VMEM((S, LANES), F32), pltpu.VMEM((S, LANES), F32), pltpu.VMEM((S, LANES), F32),
            pltpu.VMEM((len(DILATIONS), S, LANES), F32), pltpu.VMEM((len(DILATIONS), S, LANES), F32),
        ],
        compiler_params=_params(("parallel", "parallel")),
        name="dilated_attention",
    )(qkv, qkv, qkv)


def _fox_body(q_ref, k_ref, v_ref, c_ref, o_ref, qa_scr, ka_scr, *, tq, tk):
    S = q_ref.shape[0]
    p = pl.program_id(1)
    rows_per = 256
    lane = lax.broadcasted_iota(jnp.int32, (rows_per, LANES), 1)

    def build(i, carry):
        sl = pl.ds(i * rows_per, rows_per)
        c = c_ref[sl, :]
        q = q_ref[sl, :].astype(F32)
        k = k_ref[sl, :].astype(F32)
        for hh in range(2):
            own = (lane < HEAD_DIM) if hh == 0 else (lane >= HEAD_DIM)
            a0 = HEAD_DIM if hh == 0 else 0
            ccol = jnp.sum(jnp.where(lane == 2 * p + hh, c, 0.0), axis=1, keepdims=True)
            hi, mid, lo = (t.astype(F32) for t in _split3(ccol))
            ones_q = (lane >= a0 + 3) & (lane < a0 + 6)
            ones_k = (lane >= a0) & (lane < a0 + 3)
            qaug = jnp.where(own, q, jnp.where(lane == a0, hi, jnp.where(lane == a0 + 1, mid, jnp.where(
                lane == a0 + 2, lo, jnp.where(ones_q, 1.0, 0.0)))))
            kaug = jnp.where(own, k, jnp.where(lane == a0 + 3, -hi, jnp.where(lane == a0 + 4, -mid, jnp.where(
                lane == a0 + 5, -lo, jnp.where(ones_k, 1.0, 0.0)))))
            qa_scr[hh, sl, :] = qaug.astype(BF16)
            ka_scr[hh, sl, :] = kaug.astype(BF16)
        return carry

    lax.fori_loop(0, S // rows_per, build, 0)

    row_t = lax.broadcasted_iota(jnp.int32, (tq, tk), 0)
    col_t = lax.broadcasted_iota(jnp.int32, (tq, tk), 1)
    head0 = lax.broadcasted_iota(jnp.int32, (tq, LANES), 1) < HEAD_DIM
    kt_per_q = tq // tk

    def update(state, s, vblk):
        m, l, acc = state
        m_new = jnp.maximum(m, jnp.max(s, axis=-1, keepdims=True))
        alpha = jnp.exp(m - m_new)
        e = jnp.exp(s - m_new)
        return (m_new, alpha * l + jnp.sum(e, axis=-1, keepdims=True),
                alpha * acc + _dot(e.astype(BF16), vblk))

    for qi in range(S // tq):
        qsl = pl.ds(qi * tq, tq)
        qs = [qa_scr[hh, qsl, :] for hh in range(2)]

        def k_body(kj, states, qs=qs):
            ksl = pl.ds(kj * tk, tk)
            vblk = v_ref[ksl, :]
            return tuple(update(states[hh], _dot_nt(qs[hh], ka_scr[hh, ksl, :]), vblk) for hh in range(2))

        init = (jnp.full((tq, 1), -jnp.inf, F32), jnp.zeros((tq, 1), F32), jnp.zeros((tq, LANES), F32))
        states = lax.fori_loop(0, qi * kt_per_q, k_body, (init, init))
        for d in range(kt_per_q):
            ksl = pl.ds((qi * kt_per_q + d) * tk, tk)
            causal = col_t + d * tk <= row_t
            vblk = v_ref[ksl, :]
            states = tuple(update(states[hh], jnp.where(causal, _dot_nt(qs[hh], ka_scr[hh, ksl, :]), -jnp.inf), vblk)
                           for hh in range(2))
        outs = [acc / l for (_, l, acc) in states]
        o_ref[qsl, :] = jnp.where(head0, outs[0], outs[1]).astype(BF16)


def _fox_attention(qkv, cum, B, S, tq=1024, tk=1024):
    n_pairs = WIDTH_B // LANES
    base = 3 * (WIDTH_A // LANES)
    blk = lambda off: pl.BlockSpec((None, S, LANES), lambda b, p: (b, 0, off + p))
    return pl.pallas_call(
        functools.partial(_fox_body, tq=tq, tk=tk),
        grid=(B, n_pairs),
        in_specs=[
            blk(base), blk(base + n_pairs), blk(base + 2 * n_pairs),
            pl.BlockSpec((None, S, LANES), lambda b, p: (b, 0, 0)),
        ],
        out_specs=pl.BlockSpec((None, S, LANES), lambda b, p: (b, 0, p)),
        out_shape=jax.ShapeDtypeStruct((B, S, WIDTH_B), BF16),
        scratch_shapes=[pltpu.VMEM((2, S, LANES), BF16), pltpu.VMEM((2, S, LANES), BF16)],
        compiler_params=_params(("parallel", "parallel")),
        name="fox_attention",
    )(qkv, qkv, qkv, cum)


def _outproj_ffn_body(x_ref, oa_ref, ob_ref, woa_ref, wob_ref, gain_ref, wg_ref, wu_ref, wd_ref, out_ref, h_scr):
    f = pl.program_id(1)

    @pl.when(f == 0)
    def _():
        x1 = x_ref[...] + _dot(oa_ref[...], woa_ref[...]) + _dot(ob_ref[...], wob_ref[...])
        out_ref[...] = x1
        h_scr[...] = _rmsnorm(x1, gain_ref[...]).astype(BF16)

    h = h_scr[...]
    g = _dot(h, wg_ref[...])
    u = _dot(h, wu_ref[...])
    a = (g * _sigmoid(g) * u).astype(BF16)
    out_ref[...] += _dot(a, wd_ref[...])


def _outproj_ffn(x2d, oa, ob, woa, wob, gain, wg, wu, wd, tm, tf):
    T = x2d.shape[0]
    F = wg.shape[1]
    return pl.pallas_call(
        _outproj_ffn_body,
        grid=(T // tm, F // tf),
        in_specs=[
            pl.BlockSpec((tm, D_MODEL), lambda i, f: (i, 0)),
            pl.BlockSpec((tm, WIDTH_A), lambda i, f: (i, 0)),
            pl.BlockSpec((tm, WIDTH_B), lambda i, f: (i, 0)),
            pl.BlockSpec((WIDTH_A, D_MODEL), lambda i, f: (0, 0)),
            pl.BlockSpec((WIDTH_B, D_MODEL), lambda i, f: (0, 0)),
            pl.BlockSpec((1, D_MODEL), lambda i, f: (0, 0)),
            pl.BlockSpec((D_MODEL, tf), lambda i, f: (0, f)),
            pl.BlockSpec((D_MODEL, tf), lambda i, f: (0, f)),
            pl.BlockSpec((tf, D_MODEL), lambda i, f: (f, 0)),
        ],
        out_specs=pl.BlockSpec((tm, D_MODEL), lambda i, f: (i, 0)),
        out_shape=jax.ShapeDtypeStruct((T, D_MODEL), F32),
        scratch_shapes=[pltpu.VMEM((tm, D_MODEL), BF16)],
        compiler_params=_params(("parallel", "arbitrary")),
        name="outproj_ffn",
    )(x2d, oa, ob, woa, wob, gain, wg, wu, wd)


def _inproj_c_body(x_ref, gain_ref, w_ref, qig_ref, f_ref):
    h = _rmsnorm(x_ref[...], gain_ref[...]).astype(BF16)
    n_qig = qig_ref.shape[1]
    for c in range(n_qig // 256):
        qig_ref[:, c * 256:(c + 1) * 256] = _dot(h, w_ref[:, c * 256:(c + 1) * 256]).astype(BF16)
    for c in range(f_ref.shape[1] // 256):
        f_ref[:, c * 256:(c + 1) * 256] = _dot(h, w_ref[:, n_qig + c * 256:n_qig + (c + 1) * 256])


def _inproj_c(x2d, gain, w, tm):
    T = x2d.shape[0]
    return pl.pallas_call(
        _inproj_c_body,
        grid=(T // tm,),
        in_specs=[
            pl.BlockSpec((tm, D_MODEL), lambda i: (i, 0)),
            pl.BlockSpec((1, D_MODEL), lambda i: (0, 0)),
            pl.BlockSpec((D_MODEL, 4 * D_MODEL), lambda i: (0, 0)),
        ],
        out_specs=[
            pl.BlockSpec((tm, 3 * D_MODEL), lambda i: (i, 0)),
            pl.BlockSpec((tm, D_MODEL), lambda i: (i, 0)),
        ],
        out_shape=[
            jax.ShapeDtypeStruct((T, 3 * D_MODEL), BF16),
            jax.ShapeDtypeStruct((T, D_MODEL), F32),
        ],
        compiler_params=_params(("parallel",)),
        name="inproj_c",
    )(x2d, gain, w)


def _hgrn_body(q_ref, i_ref, g_ref, f_ref, lb_ref, gn_ref, o_ref, state_scr, o_scr, ops0, ops1, dec0, dec1, *, heads):
    S = q_ref.shape[0]
    C = HGRN_CHUNK
    leaf = HGRN_LEAF
    n_chunks = S // C
    row = lax.broadcasted_iota(jnp.int32, (C, C), 0)
    col = lax.broadcasted_iota(jnp.int32, (C, C), 1)
    half = C // 2
    mask_a = (row >= half) & (col < half)
    mask_b = ((row // leaf) == (col // leaf) + 1) & ((row // half) == (col // half))
    mask_l = ((row // leaf) == (col // leaf)) & (col <= row)
    r = lax.broadcasted_iota(jnp.int32, (C, heads * HGRN_DK), 0)
    scale = HGRN_DK ** -0.5
    state_scr[...] = jnp.zeros_like(state_scr)

    def prepare(ci, ops, dec):
        sl = pl.ds(ci * C, C)
        lbv = lb_ref[...]
        f = lbv + (1.0 - lbv) * _sigmoid(f_ref[sl, :])
        k = 1.0 - f
        q = q_ref[sl, :].astype(F32) * scale
        b = _cumsum_groups(jnp.log(f))
        p1, p2, p3 = b[leaf - 1:leaf, :], b[2 * leaf - 1:2 * leaf, :], b[3 * leaf - 1:3 * leaf, :]
        b_last = b[C - 1:C, :]
        e_a = b - p2
        e_b = b - jnp.where(r < half, p1, p3)
        e_l = b - jnp.where(r < leaf, 0.0, jnp.where(r < 2 * leaf, p1, jnp.where(r < 3 * leaf, p2, p3)))
        ops[0] = (q * jnp.exp(jnp.minimum(e_a, 0.0))).astype(BF16)
        ops[1] = (k * jnp.exp(jnp.minimum(-e_a, 0.0))).astype(BF16)
        ops[2] = (q * jnp.exp(jnp.minimum(e_b, 0.0))).astype(BF16)
        ops[3] = (k * jnp.exp(jnp.minimum(-e_b, 0.0))).astype(BF16)
        ops[4] = (q * jnp.exp(e_l)).astype(BF16)
        ops[5] = (k * jnp.exp(-e_l)).astype(BF16)
        ops[6] = (q * jnp.exp(b)).astype(BF16)
        ops[7] = (k * jnp.exp(b_last - b)).astype(BF16)
        dec[0:1, :] = jnp.exp(b_last)

    def contract(ci, ops, dec):
        sl = pl.ds(ci * C, C)
        for hd in range(heads):
            cols = slice(hd * HGRN_DK, (hd + 1) * HGRN_DK)
            v = i_ref[sl, cols]
            scores = (jnp.where(mask_a, _dot_nt(ops[0, :, cols], ops[1, :, cols]), 0.0)
                      + jnp.where(mask_b, _dot_nt(ops[2, :, cols], ops[3, :, cols]), 0.0)
                      + jnp.where(mask_l, _dot_nt(ops[4, :, cols], ops[5, :, cols]), 0.0))
            state_t = state_scr[hd]
            o_scr[sl, cols] = _dot(scores.astype(BF16), v) + _dot_nt(ops[6, :, cols], state_t.astype(BF16))
            v_t = v.astype(F32).T.astype(BF16)
            state_scr[hd] = state_t * dec[0:1, cols] + _dot(v_t, ops[7, :, cols])

    prepare(0, ops0, dec0)

    def pair(j, carry):
        prepare(2 * j + 1, ops1, dec1)
        contract(2 * j, ops0, dec0)
        prepare(2 * j + 2, ops0, dec0)
        contract(2 * j + 1, ops1, dec1)
        return carry

    lax.fori_loop(0, n_chunks // 2 - 1, pair, 0)
    prepare(n_chunks - 1, ops1, dec1)
    contract(n_chunks - 2, ops0, dec0)
    contract(n_chunks - 1, ops1, dec1)

    gn = jnp.concatenate([gn_ref[...]] * heads, axis=1)
    rows_per = 128

    def finish(t, carry):
        sl = pl.ds(t * rows_per, rows_per)
        o = o_scr[sl, :]
        ys = []
        for hd in range(heads):
            oh = o[:, hd * HGRN_DK:(hd + 1) * HGRN_DK]
            ys.append(oh * lax.rsqrt(jnp.mean(oh * oh, axis=-1, keepdims=True) + EPS))
        gate = g_ref[sl, :].astype(F32)
        o_ref[sl, :] = (jnp.concatenate(ys, axis=1) * gn * (gate * _sigmoid(gate))).astype(BF16)
        return carry

    lax.fori_loop(0, S // rows_per, finish, 0)


def _hgrn(qig, flog, lb, gn, B, S, heads=4):
    ng = N_HEADS_C // heads
    w = heads * HGRN_DK
    blk = lambda off: pl.BlockSpec((None, S, w), lambda b, h: (b, 0, off + h))
    return pl.pallas_call(
        functools.partial(_hgrn_body, heads=heads),
        grid=(B, ng),
        in_specs=[
            blk(0), blk(ng), blk(2 * ng),
            pl.BlockSpec((None, S, w), lambda b, h: (b, 0, h)),
            pl.BlockSpec((1, w), lambda b, h: (0, h)),
            pl.BlockSpec((1, HGRN_DK), lambda b, h: (0, 0)),
        ],
        out_specs=pl.BlockSpec((None, S, w), lambda b, h: (b, 0, h)),
        out_shape=jax.ShapeDtypeStruct((B, S, D_MODEL), BF16),
        scratch_shapes=[
            pltpu.VMEM((heads, HGRN_DK, HGRN_DK), F32), pltpu.VMEM((S, w), F32),
            pltpu.VMEM((8, HGRN_CHUNK, w), BF16), pltpu.VMEM((8, HGRN_CHUNK, w), BF16),
            pltpu.VMEM((SUBLANES, w), F32), pltpu.VMEM((SUBLANES, w), F32),
        ],
        compiler_params=_params(("parallel", "parallel")),
        name="hgrn2",
    )(qig, qig, qig, flog, lb, gn)


def _outproj_router_body(x_ref, o_ref, w_ref, gain_ref, rhi_ref, rlo_ref, x3_ref, h_ref, idx_ref, gate_ref):
    x3 = x_ref[...] + _dot(o_ref[...], w_ref[...])
    x3_ref[...] = x3
    h = _rmsnorm(x3, gain_ref[...])
    h_ref[...] = h
    h_hi = h.astype(BF16)
    h_lo = (h - h_hi.astype(F32)).astype(BF16)
    logits = _dot(h_hi, rhi_ref[...]) + (_dot(h_lo, rhi_ref[...]) + _dot(h_hi, rlo_ref[...]))
    lane = lax.broadcasted_iota(jnp.int32, logits.shape, 1)
    lane_f = lane.astype(F32)
    lg = jnp.where(lane < N_EXPERTS, logits, -jnp.inf)
    m1 = jnp.max(lg, axis=-1, keepdims=True)
    i1 = jnp.min(jnp.where(lg == m1, lane_f, float(LANES)), axis=-1, keepdims=True)
    lg2 = jnp.where(lane_f == i1, -jnp.inf, lg)
    m2 = jnp.max(lg2, axis=-1, keepdims=True)
    i2 = jnp.min(jnp.where(lg2 == m2, lane_f, float(LANES)), axis=-1, keepdims=True)
    e2 = jnp.exp(m2 - m1)
    den = 1.0 + e2
    idx_ref[...] = jnp.where(lane == 0, i1, jnp.where(lane == 1, i2, 0.0)).astype(jnp.int32)
    gate_ref[...] = jnp.where(lane == 0, 1.0 / den, jnp.where(lane == 1, e2 / den, 0.0))


def _outproj_router(x2d, o, w, gain, rhi, rlo, tm):
    T = x2d.shape[0]
    return pl.pallas_call(
        _outproj_router_body,
        grid=(T // tm,),
        in_specs=[
            pl.BlockSpec((tm, D_MODEL), lambda i: (i, 0)),
            pl.BlockSpec((tm, D_MODEL), lambda i: (i, 0)),
            pl.BlockSpec((D_MODEL, D_MODEL), lambda i: (0, 0)),
            pl.BlockSpec((1, D_MODEL), lambda i: (0, 0)),
            pl.BlockSpec((D_MODEL, LANES), lambda i: (0, 0)),
            pl.BlockSpec((D_MODEL, LANES), lambda i: (0, 0)),
        ],
        out_specs=[
            pl.BlockSpec((tm, D_MODEL), lambda i: (i, 0)),
            pl.BlockSpec((tm, D_MODEL), lambda i: (i, 0)),
            pl.BlockSpec((tm, LANES), lambda i: (i, 0)),
            pl.BlockSpec((tm, LANES), lambda i: (i, 0)),
        ],
        out_shape=[
            jax.ShapeDtypeStruct((T, D_MODEL), F32),
            jax.ShapeDtypeStruct((T, D_MODEL), F32),
            jax.ShapeDtypeStruct((T, LANES), jnp.int32),
            jax.ShapeDtypeStruct((T, LANES), F32),
        ],
        compiler_params=_params(("parallel",)),
        name="outproj_router",
    )(x2d, o, w, gain, rhi, rlo)


def _scatter_body(dest_ref, h_ref, init_hbm, xs_hbm, sem):
    del init_hbm
    tm = h_ref.shape[0]

    def issue(r, carry):
        for c in range(TOP_K):
            pltpu.make_async_copy(h_ref.at[pl.ds(r, 1), :], xs_hbm.at[pl.ds(dest_ref[TOP_K * r + c], 1), :], sem).start()
        return carry

    lax.fori_loop(0, tm, issue, 0)
    for c in range(TOP_K):
        pltpu.make_async_copy(h_ref, xs_hbm.at[pl.ds(0, tm), :], sem).wait()


def _moe_scatter(dest, h, n_rows, tm):
    T = h.shape[0]
    return pl.pallas_call(
        _scatter_body,
        grid=(T // tm,),
        in_specs=[
            pl.BlockSpec((TOP_K * tm,), lambda i: (i,), memory_space=pltpu.SMEM),
            pl.BlockSpec((tm, D_MODEL), lambda i: (i, 0)),
            pl.BlockSpec(memory_space=pl.ANY),
        ],
        out_specs=pl.BlockSpec(memory_space=pl.ANY),
        out_shape=jax.ShapeDtypeStruct((n_rows, D_MODEL), F32),
        scratch_shapes=[pltpu.SemaphoreType.DMA(())],
        input_output_aliases={2: 0},
        compiler_params=_params(("arbitrary",), disable_bounds_checks=True),
        name="moe_scatter",
    )(dest, h, jnp.zeros((n_rows, D_MODEL), F32))


def _experts_body(te_ref, nu_ref, xs_ref, wg_ref, wu_ref, wd_ref, out_ref, xb_scr):
    i = pl.program_id(0)
    f = pl.program_id(1)

    @pl.when(f == 0)
    def _():
        xb_scr[...] = xs_ref[...].astype(BF16)
        out_ref[...] = jnp.zeros_like(out_ref)

    @pl.when(i < nu_ref[0])
    def _():
        xb = xb_scr[...]
        g = _dot(xb, wg_ref[...])
        u = _dot(xb, wu_ref[...])
        a = (g * _sigmoid(g) * u).astype(BF16)
        out_ref[...] += _dot(a, wd_ref[...])


def _moe_experts(tile_expert, n_used, xs, wg, wu, wd, tm, tf):
    R = xs.shape[0]
    F = wg.shape[2]
    nf = F // tf

    def f_eff(i, f, nu):
        return jnp.where(i < nu[0], f, nf - 1)

    return pl.pallas_call(
        _experts_body,
        grid_spec=pltpu.PrefetchScalarGridSpec(
            num_scalar_prefetch=2,
            grid=(R // tm, nf),
            in_specs=[
                pl.BlockSpec((tm, D_MODEL), lambda i, f, te, nu: (i, 0)),
                pl.BlockSpec((None, D_MODEL, tf), lambda i, f, te, nu: (te[i], 0, f_eff(i, f, nu))),
                pl.BlockSpec((None, D_MODEL, tf), lambda i, f, te, nu: (te[i], 0, f_eff(i, f, nu))),
                pl.BlockSpec((None, tf, D_MODEL), lambda i, f, te, nu: (te[i], f_eff(i, f, nu), 0)),
            ],
            out_specs=pl.BlockSpec((tm, D_MODEL), lambda i, f, te, nu: (i, 0)),
            scratch_shapes=[pltpu.VMEM((tm, D_MODEL), BF16)],
        ),
        out_shape=jax.ShapeDtypeStruct((R, D_MODEL), F32),
        compiler_params=_params(("arbitrary", "arbitrary")),
        name="moe_experts",
    )(tile_expert, n_used, xs, wg, wu, wd)


def _combine_body(dest_ref, x_ref, gate_ref, ys_hbm, gain_ref, out_ref, buf, sem):
    tm = x_ref.shape[0]

    def issue(r, carry):
        for c in range(TOP_K):
            pltpu.make_async_copy(ys_hbm.at[pl.ds(dest_ref[TOP_K * r + c], 1), :], buf.at[c, pl.ds(r, 1), :], sem).start()
        return carry

    lax.fori_loop(0, tm, issue, 0)
    for c in range(TOP_K):
        pltpu.make_async_copy(ys_hbm.at[pl.ds(0, tm), :], buf.at[c], sem).wait()
    gates = gate_ref[...]
    y = x_ref[...] + (gates[:, 0:1] * buf[0] + gates[:, 1:2] * buf[1])
    out_ref[...] = _rmsnorm(y, gain_ref[...])


def _moe_combine(dest, x3, gates, ys, gain, tm):
    T = x3.shape[0]
    return pl.pallas_call(
        _combine_body,
        grid=(T // tm,),
        in_specs=[
            pl.BlockSpec((TOP_K * tm,), lambda i: (i,), memory_space=pltpu.SMEM),
            pl.BlockSpec((tm, D_MODEL), lambda i: (i, 0)),
            pl.BlockSpec((tm, LANES), lambda i: (i, 0)),
            pl.BlockSpec(memory_space=pl.ANY),
            pl.BlockSpec((1, D_MODEL), lambda i: (0, 0)),
        ],
        out_specs=pl.BlockSpec((tm, D_MODEL), lambda i: (i, 0)),
        out_shape=jax.ShapeDtypeStruct((T, D_MODEL), F32),
        scratch_shapes=[pltpu.VMEM((TOP_K, tm, D_MODEL), F32), pltpu.SemaphoreType.DMA(())],
        compiler_params=_params(("arbitrary",), disable_bounds_checks=True),
        name="moe_combine",
    )(dest, x3, gates, ys, gain)


def _routing_tables(idx, tm):
    T = idx.shape[0]
    n_assign = TOP_K * T
    flat_e = idx.reshape(n_assign)
    onehot = (flat_e[:, None] == jnp.arange(N_EXPERTS, dtype=jnp.int32)[None, :]).astype(jnp.int32)
    rank = jnp.sum((jnp.cumsum(onehot, axis=0) - onehot) * onehot, axis=1)
    counts = jnp.sum(onehot, axis=0)
    padded = ((counts + tm - 1) // tm) * tm
    ends = jnp.cumsum(padded)
    starts = ends - padded
    dest = (jnp.sum(onehot * starts[None, :], axis=1) + rank).astype(jnp.int32)
    n_rows = n_assign + N_EXPERTS * tm
    tile_start = jnp.arange(n_rows // tm, dtype=jnp.int32) * tm
    tile_expert = jnp.minimum(jnp.sum((tile_start[:, None] >= ends[None, :]).astype(jnp.int32), axis=1), N_EXPERTS - 1)
    n_used = (ends[-1] // tm).astype(jnp.int32).reshape(1)
    return dest, n_rows, tile_expert.astype(jnp.int32), n_used


def _rope_tables(positions):
    B, S = positions.shape
    inv_freq = jnp.power(jnp.float32(ROPE_THETA), -jnp.arange(ROPE_HALF, dtype=F32) / ROPE_HALF)
    ang = positions.astype(F32)[..., None] * inv_freq
    cos, sin = jnp.cos(ang), jnp.sin(ang)
    rest = HEAD_DIM - ROPE_DIM
    cos_h = jnp.concatenate([cos, cos, jnp.ones((B, S, rest), F32)], axis=-1)
    sin_h = jnp.concatenate([-sin, sin, jnp.zeros((B, S, rest), F32)], axis=-1)
    reps = LANES // HEAD_DIM
    return (jnp.tile(cos_h, (1, 1, reps)).reshape(B * S, LANES),
            jnp.tile(sin_h, (1, 1, reps)).reshape(B * S, LANES))


def kernel(x, positions, norm_mix, norm_ffn, w_in_ab, fgate_bias, w_out_ab, w_in_c, lower_bounds, gnorm_c, w_out_c,
           w_gate_ffn, w_up_ffn, w_down_ffn, router, w_gate_moe, w_up_moe, w_down_moe, norm_final):
    B, S, D = x.shape
    T = B * S
    assert D == D_MODEL and S % 1024 == 0
    tm = 512
    x2d = x.reshape(T, D)

    w_ab = jnp.pad(w_in_ab[0], ((0, 0), (0, QKV_WIDTH + LANES - w_in_ab.shape[2]))).astype(BF16)
    cos, sin = _rope_tables(positions)
    bias = jnp.pad(fgate_bias[0], (0, LANES - N_HEADS_B)).reshape(1, LANES)
    qkv, cum = _inproj_ab(x2d, norm_mix[0:1], w_ab, cos, sin, bias, tm, S)
    qkv = qkv.reshape(B, S, QKV_WIDTH)
    out_a = _dilated_attention(qkv, B, S).reshape(T, WIDTH_A)
    out_b = _fox_attention(qkv, cum.reshape(B, S, LANES), B, S).reshape(T, WIDTH_B)
    w_o = w_out_ab[0].astype(BF16)
    d_ff = w_gate_ffn.shape[2]
    x2 = _outproj_ffn(x2d, out_a, out_b, w_o[:WIDTH_A], w_o[WIDTH_A:], norm_ffn[0:1],
                      w_gate_ffn[0].astype(BF16), w_up_ffn[0].astype(BF16), w_down_ffn[0].astype(BF16),
                      tm, d_ff // 2)

    lb_all = jnp.cumsum(jax.nn.softmax(lower_bounds.astype(F32), axis=0), axis=0)
    lb = (lb_all - lb_all[0:1])[1].reshape(1, D)
    wq, wf, wi, wg = jnp.split(w_in_c[0], 4, axis=-1)
    w_c = jnp.concatenate([wq, wi, wg, wf], axis=-1).astype(BF16)
    qig, flog_c = _inproj_c(x2, norm_mix[1:2], w_c, tm)
    o_c = _hgrn(qig.reshape(B, S, 3 * D), flog_c.reshape(B, S, D), lb, gnorm_c[0:1], B, S).reshape(T, D)

    r_pad = jnp.pad(router[0], ((0, 0), (0, LANES - N_EXPERTS)))
    r_hi = r_pad.astype(BF16)
    r_lo = (r_pad - r_hi.astype(F32)).astype(BF16)
    x3, h3, idx, gates = _outproj_router(x2, o_c, w_out_c[0].astype(BF16), norm_ffn[1:2], r_hi, r_lo, tm)
    tm_e = 1024
    dest, n_rows, tile_expert, n_used = _routing_tables(idx[:, :TOP_K], tm_e)
    xs = _moe_scatter(dest, h3, n_rows, tm)
    ys = _moe_experts(tile_expert, n_used, xs, w_gate_moe[0].astype(BF16), w_up_moe[0].astype(BF16),
                      w_down_moe[0].astype(BF16), tm_e, 512)
    out = _moe_combine(dest, x3, gates, ys, norm_final.reshape(1, D), tm)
    return out.reshape(B, S, D)
```

```python
import functools

import jax
import jax.numpy as jnp
from jax import lax
from jax.experimental import pallas as pl
from jax.experimental.pallas import tpu as pltpu

F32 = jnp.float32
BF16 = jnp.bfloat16

D_MODEL = 1024
HEAD_DIM = 64
N_HEADS_A = 8
N_HEADS_B = 8
WIDTH_A = N_HEADS_A * HEAD_DIM
WIDTH_B = N_HEADS_B * HEAD_DIM
QKV_WIDTH = 3 * (WIDTH_A + WIDTH_B)
ROPE_THETA = 500000.0
ROPE_DIM = HEAD_DIM // 4
ROPE_HALF = ROPE_DIM // 2
ATT_BLOCK = 128
DILATIONS = (1, 4, 16)
N_HEADS_C = 8
HGRN_DK = 128
HGRN_CHUNK = 64
HGRN_LEAF = 16
N_EXPERTS = 8
TOP_K = 2
EPS = 1e-6

LANES = 128
SUBLANES = 8
VMEM_LIMIT = 56 * 1024 * 1024

NT_DIMS = (((1,), (1,)), ((), ()))
TN_DIMS = (((0,), (0,)), ((), ()))


def _params(semantics, **kw):
    return pltpu.CompilerParams(dimension_semantics=semantics, vmem_limit_bytes=VMEM_LIMIT, **kw)


def _rmsnorm(x, gain):
    return x * lax.rsqrt(jnp.mean(x * x, axis=-1, keepdims=True) + EPS) * gain


def _sigmoid(x):
    return 1.0 / (1.0 + jnp.exp(-x))


def _split3(x):
    hi = x.astype(BF16)
    r1 = x - hi.astype(F32)
    mid = r1.astype(BF16)
    lo = (r1 - mid.astype(F32)).astype(BF16)
    return hi, mid, lo


def _dot(a, b):
    return jnp.dot(a, b, preferred_element_type=F32)


def _dot_nt(a, b):
    return lax.dot_general(a, b, NT_DIMS, preferred_element_type=F32)


def _cumsum_groups(x):
    n, w = x.shape
    rows = lax.broadcasted_iota(jnp.int32, (SUBLANES, w), 0)
    out, carry = [], None
    for g in range(n // SUBLANES):
        xg = x[SUBLANES * g:SUBLANES * (g + 1), :]
        for s in (1, 2, 4):
            xg = xg + jnp.where(rows >= s, pltpu.roll(xg, s, 0), 0.0)
        if carry is not None:
            xg = xg + carry
        carry = xg[SUBLANES - 1:SUBLANES, :]
        out.append(xg)
    return jnp.concatenate(out, axis=0)


def _inproj_ab_body(x_ref, gain_ref, w_ref, cos_ref, sin_ref, bias_ref, qkv_ref, cum_ref, carry_scr, *, tiles_per_seq):
    h = _rmsnorm(x_ref[...], gain_ref[...]).astype(BF16)
    cos = cos_ref[...]
    sin = sin_ref[...]
    lane = lax.broadcasted_iota(jnp.int32, cos.shape, 1)
    low = (lane & (HEAD_DIM - 1)) < ROPE_HALF
    scale = HEAD_DIM ** -0.5
    for c in range(QKV_WIDTH // 256):
        y = _dot(h, w_ref[:, c * 256:(c + 1) * 256])
        seg = c // 2
        for s in range(2):
            yy = y[:, s * LANES:(s + 1) * LANES]
            if seg in (0, 1):
                partner = jnp.where(low, pltpu.roll(yy, LANES - ROPE_HALF, 1), pltpu.roll(yy, ROPE_HALF, 1))
                yy = yy * cos + partner * sin
            if seg in (0, 3):
                yy = yy * scale
            qkv_ref[:, c * 256 + s * LANES:c * 256 + (s + 1) * LANES] = yy.astype(BF16)

    @pl.when(pl.program_id(0) % tiles_per_seq == 0)
    def _():
        carry_scr[...] = jnp.zeros_like(carry_scr)

    x = _dot(h, w_ref[:, QKV_WIDTH:QKV_WIDTH + LANES]) + bias_ref[...]
    logf = -(jnp.maximum(-x, 0.0) + jnp.log1p(jnp.exp(-jnp.abs(x))))
    cum = _cumsum_groups(logf) + carry_scr[0:1, :]
    cum_ref[...] = cum
    carry_scr[0:1, :] = cum[cum.shape[0] - 1:, :]


def _inproj_ab(x2d, gain, w, cos, sin, bias, tm, seq_len):
    T = x2d.shape[0]
    wn = w.shape[1]
    return pl.pallas_call(
        functools.partial(_inproj_ab_body, tiles_per_seq=seq_len // tm),
        grid=(T // tm,),
        in_specs=[
            pl.BlockSpec((tm, D_MODEL), lambda i: (i, 0)),
            pl.BlockSpec((1, D_MODEL), lambda i: (0, 0)),
            pl.BlockSpec((D_MODEL, wn), lambda i: (0, 0)),
            pl.BlockSpec((tm, LANES), lambda i: (i, 0)),
            pl.BlockSpec((tm, LANES), lambda i: (i, 0)),
            pl.BlockSpec((1, LANES), lambda i: (0, 0)),
        ],
        out_specs=[
            pl.BlockSpec((tm, QKV_WIDTH), lambda i: (i, 0)),
            pl.BlockSpec((tm, LANES), lambda i: (i, 0)),
        ],
        out_shape=[
            jax.ShapeDtypeStruct((T, QKV_WIDTH), BF16),
            jax.ShapeDtypeStruct((T, LANES), F32),
        ],
        scratch_shapes=[pltpu.VMEM((SUBLANES, LANES), F32)],
        compiler_params=_params(("arbitrary",)),
        name="inproj_ab",
    )(x2d, gain, w, cos, sin, bias)


def _dilated_body(q_ref, k_ref, v_ref, o_ref, qf, kf, vf, ob, lb):
    S = q_ref.shape[0]
    nb = ATT_BLOCK
    qf[...] = q_ref[...].astype(F32)
    kf[...] = k_ref[...].astype(F32)
    vf[...] = v_ref[...].astype(F32)
    head0 = lax.broadcasted_iota(jnp.int32, (nb, LANES), 1) < HEAD_DIM
    qi2 = lax.broadcasted_iota(jnp.int32, (2 * nb, 2 * nb), 0) & (nb - 1)
    kj2 = lax.broadcasted_iota(jnp.int32, (2 * nb, 2 * nb), 1)
    valid2 = (kj2 >= qi2) & (kj2 <= qi2 + nb)
    qi1 = lax.broadcasted_iota(jnp.int32, (2 * nb, nb), 0) & (nb - 1)
    kj1 = lax.broadcasted_iota(jnp.int32, (2 * nb, nb), 1)
    valid1 = kj1 <= qi1

    def rows(start, size, r):
        return pl.ds(start, size) if r == 1 else pl.ds(start, size, stride=r)

    def block(br, r, q0, k0, nk):
        qs = qf[rows(q0, nb, r), :]
        kb = kf[rows(k0, nk, r), :].astype(BF16)
        vb = vf[rows(k0, nk, r), :].astype(BF16)
        q2 = jnp.concatenate([jnp.where(head0, qs, 0.0), jnp.where(head0, 0.0, qs)], axis=0).astype(BF16)
        s = jnp.where(valid2 if nk == 2 * nb else valid1, _dot_nt(q2, kb), -jnp.inf)
        m = jnp.max(s, axis=-1, keepdims=True)
        e = jnp.exp(s - m)
        l = jnp.sum(e, axis=-1, keepdims=True)
        o = _dot(e.astype(BF16), vb) / l
        lse = jnp.broadcast_to(m + jnp.log(l), (2 * nb, LANES))
        ob[br, rows(q0, nb, r), :] = jnp.where(head0, o[:nb], o[nb:])
        lb[br, rows(q0, nb, r), :] = jnp.where(head0, lse[:nb], lse[nb:])

    for br, r in enumerate(DILATIONS):
        n_blocks = S // (r * nb)
        if n_blocks == 1:
            def first_only(c, carry, br=br, r=r):
                block(br, r, c, c, nb)
                return carry
            lax.fori_loop(0, r, first_only, 0, unroll=8)
        else:
            for c in range(r):
                block(br, r, c, c, nb)

                def later(n, carry, br=br, r=r, c=c):
                    block(br, r, n * (nb * r) + c, (n - 1) * (nb * r) + c, 2 * nb)
                    return carry
                lax.fori_loop(1, n_blocks, later, 0, unroll=5 if (n_blocks - 1) % 5 == 0 else 3)

    rows_per = 256
    for ch in range(S // rows_per):
        sl = pl.ds(ch * rows_per, rows_per)
        l0, l1, l2 = lb[0, sl, :], lb[1, sl, :], lb[2, sl, :]
        m = jnp.maximum(jnp.maximum(l0, l1), l2)
        w0, w1, w2 = jnp.exp(l0 - m), jnp.exp(l1 - m), jnp.exp(l2 - m)
        o = (w0 * ob[0, sl, :] + w1 * ob[1, sl, :] + w2 * ob[2, sl, :]) / (w0 + w1 + w2)
        o_ref[sl, :] = o.astype(BF16)


def _dilated_attention(qkv, B, S):
    n_pairs = WIDTH_A // LANES
    blk = lambda off: pl.BlockSpec((None, S, LANES), lambda b, p: (b, 0, off + p))
    return pl.pallas_call(
        _dilated_body,
        grid=(B, n_pairs),
        in_specs=[blk(0), blk(n_pairs), blk(2 * n_pairs)],
        out_specs=pl.BlockSpec((None, S, LANES), lambda b, p: (b, 0, p)),
        out_shape=jax.ShapeDtypeStruct((B, S, WIDTH_A), BF16),
        scratch_shapes=[
            pltpu.VMEM((S, LANES), F32), pltpu.VMEM((S, LANES), F32), pltpu.VMEM((S, LANES), F32),
            pltpu.VMEM((len(DILATIONS), S, LANES), F32), pltpu.VMEM((len(DILATIONS), S, LANES), F32),
        ],
        compiler_params=_params(("parallel", "parallel")),
        name="dilated_attention",
    )(qkv, qkv, qkv)


def _fox_body(q_ref, k_ref, v_ref, c_ref, o_ref, qa_scr, ka_scr, *, tq, tk):
    S = q_ref.shape[0]
    p = pl.program_id(1)
    rows_per = 256
    lane = lax.broadcasted_iota(jnp.int32, (rows_per, LANES), 1)

    def build(i, carry):
        sl = pl.ds(i * rows_per, rows_per)
        c = c_ref[sl, :]
        q = q_ref[sl, :].astype(F32)
        k = k_ref[sl, :].astype(F32)
        for hh in range(2):
            own = (lane < HEAD_DIM) if hh == 0 else (lane >= HEAD_DIM)
            a0 = HEAD_DIM if hh == 0 else 0
            ccol = jnp.sum(jnp.where(lane == 2 * p + hh, c, 0.0), axis=1, keepdims=True)
            hi, mid, lo = (t.astype(F32) for t in _split3(ccol))
            ones_q = (lane >= a0 + 3) & (lane < a0 + 6)
            ones_k = (lane >= a0) & (lane < a0 + 3)
            qaug = jnp.where(own, q, jnp.where(lane == a0, hi, jnp.where(lane == a0 + 1, mid, jnp.where(
                lane == a0 + 2, lo, jnp.where(ones_q, 1.0, 0.0)))))
            kaug = jnp.where(own, k, jnp.where(lane == a0 + 3, -hi, jnp.where(lane == a0 + 4, -mid, jnp.where(
                lane == a0 + 5, -lo, jnp.where(ones_k, 1.0, 0.0)))))
            qa_scr[hh, sl, :] = qaug.astype(BF16)
            ka_scr[hh, sl, :] = kaug.astype(BF16)
        return carry

    lax.fori_loop(0, S // rows_per, build, 0)

    row_t = lax.broadcasted_iota(jnp.int32, (tq, tk), 0)
    col_t = lax.broadcasted_iota(jnp.int32, (tq, tk), 1)
    head0 = lax.broadcasted_iota(jnp.int32, (tq, LANES), 1) < HEAD_DIM
    kt_per_q = tq // tk

    def update(state, s, vblk):
        m, l, acc = state
        m_new = jnp.maximum(m, jnp.max(s, axis=-1, keepdims=True))
        alpha = jnp.exp(m - m_new)
        e = jnp.exp(s - m_new)
        return (m_new, alpha * l + jnp.sum(e, axis=-1, keepdims=True),
                alpha * acc + _dot(e.astype(BF16), vblk))

    for qi in range(S // tq):
        qsl = pl.ds(qi * tq, tq)
        qs = [qa_scr[hh, qsl, :] for hh in range(2)]

        def k_body(kj, states, qs=qs):
            ksl = pl.ds(kj * tk, tk)
            vblk = v_ref[ksl, :]
            return tuple(update(states[hh], _dot_nt(qs[hh], ka_scr[hh, ksl, :]), vblk) for hh in range(2))

        init = (jnp.full((tq, 1), -jnp.inf, F32), jnp.zeros((tq, 1), F32), jnp.zeros((tq, LANES), F32))
        states = lax.fori_loop(0, qi * kt_per_q, k_body, (init, init))
        for d in range(kt_per_q):
            ksl = pl.ds((qi * kt_per_q + d) * tk, tk)
            causal = col_t + d * tk <= row_t
            vblk = v_ref[ksl, :]
            states = tuple(update(states[hh], jnp.where(causal, _dot_nt(qs[hh], ka_scr[hh, ksl, :]), -jnp.inf), vblk)
                           for hh in range(2))
        outs = [acc / l for (_, l, acc) in states]
        o_ref[qsl, :] = jnp.where(head0, outs[0], outs[1]).astype(BF16)


def _fox_attention(qkv, cum, B, S, tq=1024, tk=1024):
    n_pairs = WIDTH_B // LANES
    base = 3 * (WIDTH_A // LANES)
    blk = lambda off: pl.BlockSpec((None, S, LANES), lambda b, p: (b, 0, off + p))
    return pl.pallas_call(
        functools.partial(_fox_body, tq=tq, tk=tk),
        grid=(B, n_pairs),
        in_specs=[
            blk(base), blk(base + n_pairs), blk(base + 2 * n_pairs),
            pl.BlockSpec((None, S, LANES), lambda b, p: (b, 0, 0)),
        ],
        out_specs=pl.BlockSpec((None, S, LANES), lambda b, p: (b, 0, p)),
        out_shape=jax.ShapeDtypeStruct((B, S, WIDTH_B), BF16),
        scratch_shapes=[pltpu.VMEM((2, S, LANES), BF16), pltpu.VMEM((2, S, LANES), BF16)],
        compiler_params=_params(("parallel", "parallel")),
        name="fox_attention",
    )(qkv, qkv, qkv, cum)


def _outproj_ffn_body(x_ref, oa_ref, ob_ref, woa_ref, wob_ref, gain_ref, wg_ref, wu_ref, wd_ref, out_ref, h_scr):
    f = pl.program_id(1)

    @pl.when(f == 0)
    def _():
        x1 = x_ref[...] + _dot(oa_ref[...], woa_ref[...]) + _dot(ob_ref[...], wob_ref[...])
        out_ref[...] = x1
        h_scr[...] = _rmsnorm(x1, gain_ref[...]).astype(BF16)

    h = h_scr[...]
    g = _dot(h, wg_ref[...])
    u = _dot(h, wu_ref[...])
    a = (g * _sigmoid(g) * u).astype(BF16)
    out_ref[...] += _dot(a, wd_ref[...])


def _outproj_ffn(x2d, oa, ob, woa, wob, gain, wg, wu, wd, tm, tf):
    T = x2d.shape[0]
    F = wg.shape[1]
    return pl.pallas_call(
        _outproj_ffn_body,
        grid=(T // tm, F // tf),
        in_specs=[
            pl.BlockSpec((tm, D_MODEL), lambda i, f: (i, 0)),
            pl.BlockSpec((tm, WIDTH_A), lambda i, f: (i, 0)),
            pl.BlockSpec((tm, WIDTH_B), lambda i, f: (i, 0)),
            pl.BlockSpec((WIDTH_A, D_MODEL), lambda i, f: (0, 0)),
            pl.BlockSpec((WIDTH_B, D_MODEL), lambda i, f: (0, 0)),
            pl.BlockSpec((1, D_MODEL), lambda i, f: (0, 0)),
            pl.BlockSpec((D_MODEL, tf), lambda i, f: (0, f)),
            pl.BlockSpec((D_MODEL, tf), lambda i, f: (0, f)),
            pl.BlockSpec((tf, D_MODEL), lambda i, f: (f, 0)),
        ],
        out_specs=pl.BlockSpec((tm, D_MODEL), lambda i, f: (i, 0)),
        out_shape=jax.ShapeDtypeStruct((T, D_MODEL), F32),
        scratch_shapes=[pltpu.VMEM((tm, D_MODEL), BF16)],
        compiler_params=_params(("parallel", "arbitrary")),
        name="outproj_ffn",
    )(x2d, oa, ob, woa, wob, gain, wg, wu, wd)


def _inproj_c_body(x_ref, gain_ref, w_ref, qig_ref, f_ref):
    h = _rmsnorm(x_ref[...], gain_ref[...]).astype(BF16)
    n_qig = qig_ref.shape[1]
    for c in range(n_qig // 256):
        qig_ref[:, c * 256:(c + 1) * 256] = _dot(h, w_ref[:, c * 256:(c + 1) * 256]).astype(BF16)
    for c in range(f_ref.shape[1] // 256):
        f_ref[:, c * 256:(c + 1) * 256] = _dot(h, w_ref[:, n_qig + c * 256:n_qig + (c + 1) * 256])


def _inproj_c(x2d, gain, w, tm):
    T = x2d.shape[0]
    return pl.pallas_call(
        _inproj_c_body,
        grid=(T // tm,),
        in_specs=[
            pl.BlockSpec((tm, D_MODEL), lambda i: (i, 0)),
            pl.BlockSpec((1, D_MODEL), lambda i: (0, 0)),
            pl.BlockSpec((D_MODEL, 4 * D_MODEL), lambda i: (0, 0)),
        ],
        out_specs=[
            pl.BlockSpec((tm, 3 * D_MODEL), lambda i: (i, 0)),
            pl.BlockSpec((tm, D_MODEL), lambda i: (i, 0)),
        ],
        out_shape=[
            jax.ShapeDtypeStruct((T, 3 * D_MODEL), BF16),
            jax.ShapeDtypeStruct((T, D_MODEL), F32),
        ],
        compiler_params=_params(("parallel",)),
        name="inproj_c",
    )(x2d, gain, w)


def _hgrn_body(q_ref, i_ref, g_ref, f_ref, lb_ref, gn_ref, o_ref, state_scr, o_scr, ops0, ops1, dec0, dec1, *, heads):
    S = q_ref.shape[0]
    C = HGRN_CHUNK
    leaf = HGRN_LEAF
    n_chunks = S // C
    row = lax.broadcasted_iota(jnp.int32, (C, C), 0)
    col = lax.broadcasted_iota(jnp.int32, (C, C), 1)
    half = C // 2
    mask_a = (row >= half) & (col < half)
    mask_b = ((row // leaf) == (col // leaf) + 1) & ((row // half) == (col // half))
    mask_l = ((row // leaf) == (col // leaf)) & (col <= row)
    r = lax.broadcasted_iota(jnp.int32, (C, heads * HGRN_DK), 0)
    scale = HGRN_DK ** -0.5
    state_scr[...] = jnp.zeros_like(state_scr)

    def prepare(ci, ops, dec):
        sl = pl.ds(ci * C, C)
        lbv = lb_ref[...]
        f = lbv + (1.0 - lbv) * _sigmoid(f_ref[sl, :])
        k = 1.0 - f
        q = q_ref[sl, :].astype(F32) * scale
        b = _cumsum_groups(jnp.log(f))
        p1, p2, p3 = b[leaf - 1:leaf, :], b[2 * leaf - 1:2 * leaf, :], b[3 * leaf - 1:3 * leaf, :]
        b_last = b[C - 1:C, :]
        e_a = b - p2
        e_b = b - jnp.where(r < half, p1, p3)
        e_l = b - jnp.where(r < leaf, 0.0, jnp.where(r < 2 * leaf, p1, jnp.where(r < 3 * leaf, p2, p3)))
        ops[0] = (q * jnp.exp(jnp.minimum(e_a, 0.0))).astype(BF16)
        ops[1] = (k * jnp.exp(jnp.minimum(-e_a, 0.0))).astype(BF16)
        ops[2] = (q * jnp.exp(jnp.minimum(e_b, 0.0))).astype(BF16)
        ops[3] = (k * jnp.exp(jnp.minimum(-e_b, 0.0))).astype(BF16)
        ops[4] = (q * jnp.exp(e_l)).astype(BF16)
        ops[5] = (k * jnp.exp(-e_l)).astype(BF16)
        ops[6] = (q * jnp.exp(b)).astype(BF16)
        ops[7] = (k * jnp.exp(b_last - b)).astype(BF16)
        dec[0:1, :] = jnp.exp(b_last)

    def contract(ci, ops, dec):
        sl = pl.ds(ci * C, C)
        for hd in range(heads):
            cols = slice(hd * HGRN_DK, (hd + 1) * HGRN_DK)
            v = i_ref[sl, cols]
            scores = (jnp.where(mask_a, _dot_nt(ops[0, :, cols], ops[1, :, cols]), 0.0)
                      + jnp.where(mask_b, _dot_nt(ops[2, :, cols], ops[3, :, cols]), 0.0)
                      + jnp.where(mask_l, _dot_nt(ops[4, :, cols], ops[5, :, cols]), 0.0))
            state_t = state_scr[hd]
            o_scr[sl, cols] = _dot(scores.astype(BF16), v) + _dot_nt(ops[6, :, cols], state_t.astype(BF16))
            v_t = v.astype(F32).T.astype(BF16)
            state_scr[hd] = state_t * dec[0:1, cols] + _dot(v_t, ops[7, :, cols])

    prepare(0, ops0, dec0)

    def pair(j, carry):
        prepare(2 * j + 1, ops1, dec1)
        contract(2 * j, ops0, dec0)
        prepare(2 * j + 2, ops0, dec0)
        contract(2 * j + 1, ops1, dec1)
        return carry

    lax.fori_loop(0, n_chunks // 2 - 1, pair, 0)
    prepare(n_chunks - 1, ops1, dec1)
    contract(n_chunks - 2, ops0, dec0)
    contract(n_chunks - 1, ops1, dec1)

    gn = jnp.concatenate([gn_ref[...]] * heads, axis=1)
    rows_per = 128

    def finish(t, carry):
        sl = pl.ds(t * rows_per, rows_per)
        o = o_scr[sl, :]
        ys = []
        for hd in range(heads):
            oh = o[:, hd * HGRN_DK:(hd + 1) * HGRN_DK]
            ys.append(oh * lax.rsqrt(jnp.mean(oh * oh, axis=-1, keepdims=True) + EPS))
        gate = g_ref[sl, :].astype(F32)
        o_ref[sl, :] = (jnp.concatenate(ys, axis=1) * gn * (gate * _sigmoid(gate))).astype(BF16)
        return carry

    lax.fori_loop(0, S // rows_per, finish, 0)


def _hgrn(qig, flog, lb, gn, B, S, heads=4):
    ng = N_HEADS_C // heads
    w = heads * HGRN_DK
    blk = lambda off: pl.BlockSpec((None, S, w), lambda b, h: (b, 0, off + h))
    return pl.pallas_call(
        functools.partial(_hgrn_body, heads=heads),
        grid=(B, ng),
        in_specs=[
            blk(0), blk(ng), blk(2 * ng),
            pl.BlockSpec((None, S, w), lambda b, h: (b, 0, h)),
            pl.BlockSpec((1, w), lambda b, h: (0, h)),
            pl.BlockSpec((1, HGRN_DK), lambda b, h: (0, 0)),
        ],
        out_specs=pl.BlockSpec((None, S, w), lambda b, h: (b, 0, h)),
        out_shape=jax.ShapeDtypeStruct((B, S, D_MODEL), BF16),
        scratch_shapes=[
            pltpu.VMEM((heads, HGRN_DK, HGRN_DK), F32), pltpu.VMEM((S, w), F32),
            pltpu.VMEM((8, HGRN_CHUNK, w), BF16), pltpu.VMEM((8, HGRN_CHUNK, w), BF16),
            pltpu.VMEM((SUBLANES, w), F32), pltpu.VMEM((SUBLANES, w), F32),
        ],
        compiler_params=_params(("parallel", "parallel")),
        name="hgrn2",
    )(qig, qig, qig, flog, lb, gn)


SEG_ALIGN = SUBLANES
SEG_BITS = tuple(range(9, 2, -1))


def _local_rows(tm):
    return TOP_K * tm + N_EXPERTS * SEG_ALIGN


def _outproj_router_body(x_ref, o_ref, w_ref, gain_ref, rhi_ref, rlo_ref, x3_ref, xl_ref, pos_ref, gate_ref, seg_ref):
    tm = x_ref.shape[0]
    x3 = x_ref[...] + _dot(o_ref[...], w_ref[...])
    x3_ref[...] = x3
    h = _rmsnorm(x3, gain_ref[...])
    h_hi = h.astype(BF16)
    h_lo = (h - h_hi.astype(F32)).astype(BF16)
    logits = _dot(h_hi, rhi_ref[...]) + (_dot(h_lo, rhi_ref[...]) + _dot(h_hi, rlo_ref[...]))
    lane = lax.broadcasted_iota(jnp.int32, logits.shape, 1)
    lane_f = lane.astype(F32)
    lg = jnp.where(lane < N_EXPERTS, logits, -jnp.inf)
    m1 = jnp.max(lg, axis=-1, keepdims=True)
    i1 = jnp.min(jnp.where(lg == m1, lane_f, float(LANES)), axis=-1, keepdims=True)
    lg2 = jnp.where(lane_f == i1, -jnp.inf, lg)
    m2 = jnp.max(lg2, axis=-1, keepdims=True)
    i2 = jnp.min(jnp.where(lg2 == m2, lane_f, float(LANES)), axis=-1, keepdims=True)
    e2 = jnp.exp(m2 - m1)
    den = 1.0 + e2
    gate_ref[...] = jnp.where(lane == 0, 1.0 / den, jnp.where(lane == 1, e2 / den, 0.0))

    oh1 = (lane_f == i1).astype(F32)
    oh2 = (lane_f == i2).astype(F32)
    c1 = _cumsum_groups(oh1)
    c2 = _cumsum_groups(oh2)
    n1 = c1[tm - 1:tm, :]
    count = n1 + c2[tm - 1:tm, :]
    padded = jnp.floor((count + (SEG_ALIGN - 1.0)) * (1.0 / SEG_ALIGN)) * SEG_ALIGN
    run = jnp.broadcast_to(padded, (SUBLANES, LANES))
    lane8 = lax.broadcasted_iota(jnp.int32, (SUBLANES, LANES), 1)
    for s in (1, 2, 4):
        run = run + jnp.where(lane8 >= s, pltpu.roll(run, s, 1), 0.0)
    start = run[0:1, :] - padded
    pos1 = jnp.sum(oh1 * (start + c1 - 1.0), axis=-1, keepdims=True)
    pos2 = jnp.sum(oh2 * (start + n1 + c2 - 1.0), axis=-1, keepdims=True)
    pos_ref[...] = jnp.where(lane == 0, pos1, jnp.where(lane == 1, pos2, 0.0)).astype(jnp.int32)
    slot = lax.broadcasted_iota(jnp.int32, (tm, xl_ref.shape[0]), 1).astype(F32)
    perm_t = ((slot == pos1) | (slot == pos2)).astype(BF16)
    xl_ref[...] = lax.dot_general(perm_t, h_hi, TN_DIMS, preferred_element_type=F32)
    r8 = lax.broadcasted_iota(jnp.int32, (SUBLANES, LANES), 0)
    seg_ref[...] = jnp.where(r8 == 0, count, jnp.where(r8 == 1, padded, jnp.where(r8 == 2, start, 0.0))).astype(jnp.int32)


def _outproj_router(x2d, o, w, gain, rhi, rlo, tm):
    T = x2d.shape[0]
    n_tt = T // tm
    lr = _local_rows(tm)
    return pl.pallas_call(
        _outproj_router_body,
        grid=(n_tt,),
        in_specs=[
            pl.BlockSpec((tm, D_MODEL), lambda i: (i, 0)),
            pl.BlockSpec((tm, D_MODEL), lambda i: (i, 0)),
            pl.BlockSpec((D_MODEL, D_MODEL), lambda i: (0, 0)),
            pl.BlockSpec((1, D_MODEL), lambda i: (0, 0)),
            pl.BlockSpec((D_MODEL, LANES), lambda i: (0, 0)),
            pl.BlockSpec((D_MODEL, LANES), lambda i: (0, 0)),
        ],
        out_specs=[
            pl.BlockSpec((tm, D_MODEL), lambda i: (i, 0)),
            pl.BlockSpec((lr, D_MODEL), lambda i: (i, 0)),
            pl.BlockSpec((tm, LANES), lambda i: (i, 0)),
            pl.BlockSpec((tm, LANES), lambda i: (i, 0)),
            pl.BlockSpec((SUBLANES, LANES), lambda i: (i, 0)),
        ],
        out_shape=[
            jax.ShapeDtypeStruct((T, D_MODEL), F32),
            jax.ShapeDtypeStruct((n_tt * lr, D_MODEL), F32),
            jax.ShapeDtypeStruct((T, LANES), jnp.int32),
            jax.ShapeDtypeStruct((T, LANES), F32),
            jax.ShapeDtypeStruct((n_tt * SUBLANES, LANES), jnp.int32),
        ],
        compiler_params=_params(("parallel",)),
        name="outproj_router",
    )(x2d, o, w, gain, rhi, rlo)


def _run_copies(n, src, s0, dst, d0, sem):
    out = []
    for b in SEG_BITS:
        offs = (n >> (b + 1)) << (b + 1)
        cp = pltpu.make_async_copy(src.at[pl.ds(pl.multiple_of(s0 + offs, SEG_ALIGN), 1 << b), :],
                                   dst.at[pl.ds(pl.multiple_of(d0 + offs, SEG_ALIGN), 1 << b), :], sem)
        out.append((((n >> b) & 1) == 1, cp))
    return out


def _experts_body(te_ref, nu_ref, r0_ref, jlo_ref, jhi_ref, valid_ref, cs_ref, lp_ref, src_ref,
                  xl_hbm, wg_ref, wu_ref, wd_ref, out_ref, xbuf, xb_scr, a_scr, sem, *, n_tt):
    i = pl.program_id(0)
    f = pl.program_id(1)
    tm = out_ref.shape[0]

    def move(tile, slot, wait):
        base = te_ref[tile] * n_tt
        r0 = r0_ref[tile]

        def one_run(j, carry):
            c0 = cs_ref[base + j]
            lo = jnp.maximum(c0, r0)
            hi = jnp.minimum(c0 + lp_ref[base + j], r0 + tm)
            n = jnp.maximum(hi - lo, 0)
            for cond, cp in _run_copies(n, xl_hbm, src_ref[base + j] + (lo - c0), xbuf.at[slot], lo - r0, sem.at[slot]):
                @pl.when(cond)
                def _():
                    cp.wait() if wait else cp.start()
            return carry

        lax.fori_loop(jlo_ref[tile], jhi_ref[tile], one_run, 0)

    @pl.when(f == 0)
    def _():
        @pl.when(i == 0)
        def _():
            xbuf[...] = jnp.zeros_like(xbuf)
            move(0, 0, False)

        @pl.when(i < nu_ref[0])
        def _():
            move(i, i % 2, True)

        @pl.when(i + 1 < nu_ref[0])
        def _():
            move(i + 1, (i + 1) % 2, False)

        row = lax.broadcasted_iota(jnp.int32, (tm, D_MODEL), 0)
        xb_scr[...] = jnp.where(row < valid_ref[i], xbuf[i % 2], 0.0).astype(BF16)
        out_ref[...] = jnp.zeros_like(out_ref)

    @pl.when(i < nu_ref[0])
    def _():
        xb = xb_scr[...]
        for c in range(a_scr.shape[1] // 256):
            cols = slice(c * 256, (c + 1) * 256)
            g = _dot(xb, wg_ref[:, cols])
            u = _dot(xb, wu_ref[:, cols])
            a_scr[:, cols] = (g * _sigmoid(g) * u).astype(BF16)
        out_ref[...] += _dot(a_scr[...], wd_ref[...])


def _moe_experts(tabs, xl, wg, wu, wd, n_rows, n_tt, tm, tf):
    F = wg.shape[2]
    nf = F // tf

    def f_eff(i, f, nu):
        return jnp.where(i < nu[0], f, nf - 1)

    return pl.pallas_call(
        functools.partial(_experts_body, n_tt=n_tt),
        grid_spec=pltpu.PrefetchScalarGridSpec(
            num_scalar_prefetch=len(tabs),
            grid=(n_rows // tm, nf),
            in_specs=[
                pl.BlockSpec(memory_space=pl.ANY),
                pl.BlockSpec((None, D_MODEL, tf), lambda i, f, te, nu, *_: (te[i], 0, f_eff(i, f, nu))),
                pl.BlockSpec((None, D_MODEL, tf), lambda i, f, te, nu, *_: (te[i], 0, f_eff(i, f, nu))),
                pl.BlockSpec((None, tf, D_MODEL), lambda i, f, te, nu, *_: (te[i], f_eff(i, f, nu), 0)),
            ],
            out_specs=pl.BlockSpec((tm, D_MODEL), lambda i, f, *_: (i, 0)),
            scratch_shapes=[pltpu.VMEM((2, tm, D_MODEL), F32), pltpu.VMEM((tm, D_MODEL), BF16),
                            pltpu.VMEM((tm, tf), BF16), pltpu.SemaphoreType.DMA((2,))],
        ),
        out_shape=jax.ShapeDtypeStruct((n_rows, D_MODEL), F32),
        compiler_params=_params(("arbitrary", "arbitrary")),
        name="moe_experts",
    )(*tabs, xl, wg, wu, wd)


def _combine_body(row_ref, lp_ref, off_ref, x_ref, pos_ref, gate_ref, ys_hbm, gain_ref, out_ref, yl, sem):
    j = pl.program_id(0)
    tm = x_ref.shape[0]
    lr = yl.shape[1]

    def move(tile, slot, wait):
        for e in range(N_EXPERTS):
            t = tile * N_EXPERTS + e
            for cond, cp in _run_copies(lp_ref[t], ys_hbm, row_ref[t], yl.at[slot], off_ref[t], sem.at[slot]):
                @pl.when(cond)
                def _():
                    cp.wait() if wait else cp.start()

    @pl.when(j == 0)
    def _():
        yl[...] = jnp.zeros_like(yl)
        move(0, 0, False)

    move(j, j % 2, True)

    @pl.when(j + 1 < pl.num_programs(0))
    def _():
        move(j + 1, (j + 1) % 2, False)

    last = j * N_EXPERTS + N_EXPERTS - 1
    used = off_ref[last] + lp_ref[last]
    row = lax.broadcasted_iota(jnp.int32, (lr, D_MODEL), 0)
    y_sorted = jnp.where(row < used, yl[j % 2], 0.0).astype(BF16)
    pos = pos_ref[...]
    gates = gate_ref[...]
    slot = lax.broadcasted_iota(jnp.int32, (tm, lr), 1)
    weights = jnp.where(slot == pos[:, 0:1], gates[:, 0:1], jnp.where(slot == pos[:, 1:2], gates[:, 1:2], 0.0))
    y = x_ref[...] + _dot(weights.astype(BF16), y_sorted)
    out_ref[...] = _rmsnorm(y, gain_ref[...])


def _moe_combine(tabs, x3, pos, gates, ys, gain, tm):
    T = x3.shape[0]
    return pl.pallas_call(
        _combine_body,
        grid_spec=pltpu.PrefetchScalarGridSpec(
            num_scalar_prefetch=len(tabs),
            grid=(T // tm,),
            in_specs=[
                pl.BlockSpec((tm, D_MODEL), lambda i, *_: (i, 0)),
                pl.BlockSpec((tm, LANES), lambda i, *_: (i, 0)),
                pl.BlockSpec((tm, LANES), lambda i, *_: (i, 0)),
                pl.BlockSpec(memory_space=pl.ANY),
                pl.BlockSpec((1, D_MODEL), lambda i, *_: (0, 0)),
            ],
            out_specs=pl.BlockSpec((tm, D_MODEL), lambda i, *_: (i, 0)),
            scratch_shapes=[pltpu.VMEM((2, _local_rows(tm), D_MODEL), F32), pltpu.SemaphoreType.DMA((2,))],
        ),
        out_shape=jax.ShapeDtypeStruct((T, D_MODEL), F32),
        compiler_params=_params(("arbitrary",)),
        name="moe_combine",
    )(*tabs, x3, pos, gates, ys, gain)


def _routing_tables(seg, n_tt, tm, tm_e):
    seg = seg.reshape(n_tt, SUBLANES, LANES)
    lp = seg[:, 1, :N_EXPERTS]
    off = seg[:, 2, :N_EXPERTS]
    cs = jnp.cumsum(lp, axis=0) - lp
    total = jnp.sum(lp, axis=0)
    padded = ((total + tm_e - 1) // tm_e) * tm_e
    ends = jnp.cumsum(padded)
    starts = ends - padded
    n_rows = -(-(TOP_K * n_tt * tm + N_EXPERTS * (SEG_ALIGN - 1) * n_tt) // tm_e) * tm_e + N_EXPERTS * tm_e
    tile_start = jnp.arange(n_rows // tm_e, dtype=jnp.int32) * tm_e
    te = jnp.minimum(jnp.sum((tile_start[:, None] >= ends[None, :]).astype(jnp.int32), axis=1), N_EXPERTS - 1)
    n_used = (ends[-1] // tm_e).astype(jnp.int32).reshape(1)
    r0 = tile_start - starts[te]
    cs_t = cs[:, te]
    run_end_t = cs_t + lp[:, te]
    jlo = jnp.sum((run_end_t <= r0[None, :]).astype(jnp.int32), axis=0)
    jhi = jnp.sum((cs_t < (r0 + tm_e)[None, :]).astype(jnp.int32), axis=0)
    valid = jnp.clip(total[te] - r0, 0, tm_e)
    src = jnp.arange(n_tt, dtype=jnp.int32)[:, None] * _local_rows(tm) + off
    i32 = lambda a: a.astype(jnp.int32)
    expert_tabs = (i32(te), n_used, i32(r0), i32(jlo), i32(jhi), i32(valid),
                   i32(cs.T.reshape(-1)), i32(lp.T.reshape(-1)), i32(src.T.reshape(-1)))
    combine_tabs = (i32((starts[None, :] + cs).reshape(-1)), i32(lp.reshape(-1)), i32(off.reshape(-1)))
    return expert_tabs, combine_tabs, n_rows


def _rope_tables(positions):
    B, S = positions.shape
    inv_freq = jnp.power(jnp.float32(ROPE_THETA), -jnp.arange(ROPE_HALF, dtype=F32) / ROPE_HALF)
    ang = positions.astype(F32)[..., None] * inv_freq
    cos, sin = jnp.cos(ang), jnp.sin(ang)
    rest = HEAD_DIM - ROPE_DIM
    cos_h = jnp.concatenate([cos, cos, jnp.ones((B, S, rest), F32)], axis=-1)
    sin_h = jnp.concatenate([-sin, sin, jnp.zeros((B, S, rest), F32)], axis=-1)
    reps = LANES // HEAD_DIM
    return (jnp.tile(cos_h, (1, 1, reps)).reshape(B * S, LANES),
            jnp.tile(sin_h, (1, 1, reps)).reshape(B * S, LANES))


def kernel(x, positions, norm_mix, norm_ffn, w_in_ab, fgate_bias, w_out_ab, w_in_c, lower_bounds, gnorm_c, w_out_c,
           w_gate_ffn, w_up_ffn, w_down_ffn, router, w_gate_moe, w_up_moe, w_down_moe, norm_final):
    B, S, D = x.shape
    T = B * S
    assert D == D_MODEL and S % 1024 == 0
    tm = 512
    x2d = x.reshape(T, D)

    w_ab = jnp.pad(w_in_ab[0], ((0, 0), (0, QKV_WIDTH + LANES - w_in_ab.shape[2]))).astype(BF16)
    cos, sin = _rope_tables(positions)
    bias = jnp.pad(fgate_bias[0], (0, LANES - N_HEADS_B)).reshape(1, LANES)
    qkv, cum = _inproj_ab(x2d, norm_mix[0:1], w_ab, cos, sin, bias, tm, S)
    qkv = qkv.reshape(B, S, QKV_WIDTH)
    out_a = _dilated_attention(qkv, B, S).reshape(T, WIDTH_A)
    out_b = _fox_attention(qkv, cum.reshape(B, S, LANES), B, S).reshape(T, WIDTH_B)
    w_o = w_out_ab[0].astype(BF16)
    d_ff = w_gate_ffn.shape[2]
    x2 = _outproj_ffn(x2d, out_a, out_b, w_o[:WIDTH_A], w_o[WIDTH_A:], norm_ffn[0:1],
                      w_gate_ffn[0].astype(BF16), w_up_ffn[0].astype(BF16), w_down_ffn[0].astype(BF16),
                      tm, d_ff // 2)

    lb_all = jnp.cumsum(jax.nn.softmax(lower_bounds.astype(F32), axis=0), axis=0)
    lb = (lb_all - lb_all[0:1])[1].reshape(1, D)
    wq, wf, wi, wg = jnp.split(w_in_c[0], 4, axis=-1)
    w_c = jnp.concatenate([wq, wi, wg, wf], axis=-1).astype(BF16)
    qig, flog_c = _inproj_c(x2, norm_mix[1:2], w_c, tm)
    o_c = _hgrn(qig.reshape(B, S, 3 * D), flog_c.reshape(B, S, D), lb, gnorm_c[0:1], B, S).reshape(T, D)

    r_pad = jnp.pad(router[0], ((0, 0), (0, LANES - N_EXPERTS)))
    r_hi = r_pad.astype(BF16)
    r_lo = (r_pad - r_hi.astype(F32)).astype(BF16)
    x3, xl, pos, gates, seg = _outproj_router(x2, o_c, w_out_c[0].astype(BF16), norm_ffn[1:2], r_hi, r_lo, tm)
    tm_e = 1024
    expert_tabs, combine_tabs, n_rows = _routing_tables(seg, T // tm, tm, tm_e)
    ys = _moe_experts(expert_tabs, xl, w_gate_moe[0].astype(BF16), w_up_moe[0].astype(BF16),
                      w_down_moe[0].astype(BF16), n_rows, T // tm, tm_e, w_gate_moe.shape[3] // 2)
    out = _moe_combine(combine_tabs, x3, pos, gates, ys, norm_final.reshape(1, D), tm)
    return out.reshape(B, S, D)
```

```python
import functools

import jax
import jax.numpy as jnp
from jax import lax
from jax.experimental import pallas as pl
from jax.experimental.pallas import tpu as pltpu

F32 = jnp.float32
BF16 = jnp.bfloat16

D_MODEL = 1024
HEAD_DIM = 64
N_HEADS_A = 8
N_HEADS_B = 8
WIDTH_A = N_HEADS_A * HEAD_DIM
WIDTH_B = N_HEADS_B * HEAD_DIM
QKV_WIDTH = 3 * (WIDTH_A + WIDTH_B)
ROPE_THETA = 500000.0
ROPE_DIM = HEAD_DIM // 4
ROPE_HALF = ROPE_DIM // 2
ATT_BLOCK = 128
DILATIONS = (1, 4, 16)
N_HEADS_C = 8
HGRN_DK = 128
HGRN_CHUNK = 64
HGRN_LEAF = 16
N_EXPERTS = 8
TOP_K = 2
EPS = 1e-6

LANES = 128
SUBLANES = 8
VMEM_LIMIT = 56 * 1024 * 1024

NT_DIMS = (((1,), (1,)), ((), ()))


def _params(semantics, **kw):
    return pltpu.CompilerParams(dimension_semantics=semantics, vmem_limit_bytes=VMEM_LIMIT, **kw)


def _rmsnorm(x, gain):
    return x * lax.rsqrt(jnp.mean(x * x, axis=-1, keepdims=True) + EPS) * gain


def _sigmoid(x):
    return 1.0 / (1.0 + jnp.exp(-x))


def _split3(x):
    hi = x.astype(BF16)
    r1 = x - hi.astype(F32)
    mid = r1.astype(BF16)
    lo = (r1 - mid.astype(F32)).astype(BF16)
    return hi, mid, lo


def _dot(a, b):
    return jnp.dot(a, b, preferred_element_type=F32)


def _dot_nt(a, b):
    return lax.dot_general(a, b, NT_DIMS, preferred_element_type=F32)


def _cumsum_groups(x):
    n, w = x.shape
    rows = lax.broadcasted_iota(jnp.int32, (SUBLANES, w), 0)
    out, carry = [], None
    for g in range(n // SUBLANES):
        xg = x[SUBLANES * g:SUBLANES * (g + 1), :]
        for s in (1, 2, 4):
            xg = xg + jnp.where(rows >= s, pltpu.roll(xg, s, 0), 0.0)
        if carry is not None:
            xg = xg + carry
        carry = xg[SUBLANES - 1:SUBLANES, :]
        out.append(xg)
    return jnp.concatenate(out, axis=0)


def _inproj_ab_body(x_ref, gain_ref, w_ref, cos_ref, sin_ref, bias_ref, qkv_ref, cum_ref, carry_scr, *, tiles_per_seq):
    h = _rmsnorm(x_ref[...], gain_ref[...]).astype(BF16)

    @pl.when(pl.program_id(0) % tiles_per_seq == 0)
    def _():
        carry_scr[...] = jnp.zeros_like(carry_scr)

    x = _dot(h, w_ref[:, QKV_WIDTH:QKV_WIDTH + LANES]) + bias_ref[...]
    logf = -(jnp.maximum(-x, 0.0) + jnp.log1p(jnp.exp(-jnp.abs(x))))
    cum = _cumsum_groups(logf) + carry_scr[0:1, :]
    cum_ref[...] = cum
    carry_scr[0:1, :] = cum[cum.shape[0] - 1:, :]

    cos = cos_ref[...]
    sin = sin_ref[...]
    lane = lax.broadcasted_iota(jnp.int32, cos.shape, 1)
    low = (lane & (HEAD_DIM - 1)) < ROPE_HALF
    scale = HEAD_DIM ** -0.5
    for c in range(QKV_WIDTH // 256):
        y = _dot(h, w_ref[:, c * 256:(c + 1) * 256])
        seg = c // 2
        for s in range(2):
            yy = y[:, s * LANES:(s + 1) * LANES]
            if seg in (0, 1):
                partner = jnp.where(low, pltpu.roll(yy, LANES - ROPE_HALF, 1), pltpu.roll(yy, ROPE_HALF, 1))
                yy = yy * cos + partner * sin
            if seg in (0, 3):
                yy = yy * scale
            qkv_ref[:, c * 256 + s * LANES:c * 256 + (s + 1) * LANES] = yy.astype(BF16)


def _inproj_ab(x2d, gain, w, cos, sin, bias, tm, seq_len):
    T = x2d.shape[0]
    wn = w.shape[1]
    return pl.pallas_call(
        functools.partial(_inproj_ab_body, tiles_per_seq=seq_len // tm),
        grid=(T // tm,),
        in_specs=[
            pl.BlockSpec((tm, D_MODEL), lambda i: (i, 0)),
            pl.BlockSpec((1, D_MODEL), lambda i: (0, 0)),
            pl.BlockSpec((D_MODEL, wn), lambda i: (0, 0)),
            pl.BlockSpec((tm, LANES), lambda i: (i, 0)),
            pl.BlockSpec((tm, LANES), lambda i: (i, 0)),
            pl.BlockSpec((1, LANES), lambda i: (0, 0)),
        ],
        out_specs=[
            pl.BlockSpec((tm, QKV_WIDTH), lambda i: (i, 0)),
            pl.BlockSpec((tm, LANES), lambda i: (i, 0)),
        ],
        out_shape=[
            jax.ShapeDtypeStruct((T, QKV_WIDTH), BF16),
            jax.ShapeDtypeStruct((T, LANES), F32),
        ],
        scratch_shapes=[pltpu.VMEM((SUBLANES, LANES), F32)],
        compiler_params=_params(("arbitrary",)),
        name="inproj_ab",
    )(x2d, gain, w, cos, sin, bias)


def _dilated_body(q_ref, k_ref, v_ref, o_ref, qf, kf, vf, ob, lb):
    S = q_ref.shape[0]
    nb = ATT_BLOCK
    qf[...] = q_ref[...].astype(F32)
    kf[...] = k_ref[...].astype(F32)
    vf[...] = v_ref[...].astype(F32)
    head0 = lax.broadcasted_iota(jnp.int32, (nb, LANES), 1) < HEAD_DIM
    qi2 = lax.broadcasted_iota(jnp.int32, (2 * nb, 2 * nb), 0) & (nb - 1)
    kj2 = lax.broadcasted_iota(jnp.int32, (2 * nb, 2 * nb), 1)
    valid2 = (kj2 >= qi2) & (kj2 <= qi2 + nb)
    qi1 = lax.broadcasted_iota(jnp.int32, (2 * nb, nb), 0) & (nb - 1)
    kj1 = lax.broadcasted_iota(jnp.int32, (2 * nb, nb), 1)
    valid1 = kj1 <= qi1

    def rows(start, size, r):
        return pl.ds(start, size) if r == 1 else pl.ds(start, size, stride=r)

    def block(br, r, q0, k0, nk):
        qs = qf[rows(q0, nb, r), :]
        kb = kf[rows(k0, nk, r), :].astype(BF16)
        vb = vf[rows(k0, nk, r), :].astype(BF16)
        q2 = jnp.concatenate([jnp.where(head0, qs, 0.0), jnp.where(head0, 0.0, qs)], axis=0).astype(BF16)
        s = jnp.where(valid2 if nk == 2 * nb else valid1, _dot_nt(q2, kb), -jnp.inf)
        m = jnp.max(s, axis=-1, keepdims=True)
        e = jnp.exp(s - m)
        l = jnp.sum(e, axis=-1, keepdims=True)
        o = _dot(e.astype(BF16), vb) / l
        lse = jnp.broadcast_to(m + jnp.log(l), (2 * nb, LANES))
        ob[br, rows(q0, nb, r), :] = jnp.where(head0, o[:nb], o[nb:])
        lb[br, rows(q0, nb, r), :] = jnp.where(head0, lse[:nb], lse[nb:])

    for br, r in enumerate(DILATIONS):
        n_blocks = S // (r * nb)
        if n_blocks == 1:
            def first_only(c, carry, br=br, r=r):
                block(br, r, c, c, nb)
                return carry
            lax.fori_loop(0, r, first_only, 0, unroll=8)
        else:
            for c in range(r):
                block(br, r, c, c, nb)

                def later(n, carry, br=br, r=r, c=c):
                    block(br, r, n * (nb * r) + c, (n - 1) * (nb * r) + c, 2 * nb)
                    return carry
                lax.fori_loop(1, n_blocks, later, 0, unroll=5 if (n_blocks - 1) % 5 == 0 else 3)

    rows_per = 256
    for ch in range(S // rows_per):
        sl = pl.ds(ch * rows_per, rows_per)
        l0, l1, l2 = lb[0, sl, :], lb[1, sl, :], lb[2, sl, :]
        m = jnp.maximum(jnp.maximum(l0, l1), l2)
        w0, w1, w2 = jnp.exp(l0 - m), jnp.exp(l1 - m), jnp.exp(l2 - m)
        o = (w0 * ob[0, sl, :] + w1 * ob[1, sl, :] + w2 * ob[2, sl, :]) / (w0 + w1 + w2)
        o_ref[sl, :] = o.astype(BF16)


def _dilated_attention(qkv, B, S):
    n_pairs = WIDTH_A // LANES
    blk = lambda off: pl.BlockSpec((None, S, LANES), lambda b, p: (b, 0, off + p))
    return pl.pallas_call(
        _dilated_body,
        grid=(B, n_pairs),
        in_specs=[blk(0), blk(n_pairs), blk(2 * n_pairs)],
        out_specs=pl.BlockSpec((None, S, LANES), lambda b, p: (b, 0, p)),
        out_shape=jax.ShapeDtypeStruct((B, S, WIDTH_A), BF16),
        scratch_shapes=[
            pltpu.VMEM((S, LANES), F32), pltpu.VMEM((S, LANES), F32), pltpu.VMEM((S, LANES), F32),
            pltpu.VMEM((len(DILATIONS), S, LANES), F32), pltpu.VMEM((len(DILATIONS), S, LANES), F32),
        ],
        compiler_params=_params(("parallel", "parallel")),
        name="dilated_attention",
    )(qkv, qkv, qkv)


def _fox_body(q_ref, k_ref, v_ref, c_ref, o_ref, qa_scr, ka_scr, *, tq, tk):
    S = q_ref.shape[0]
    p = pl.program_id(1)
    rows_per = 256
    lane = lax.broadcasted_iota(jnp.int32, (rows_per, LANES), 1)

    def build(i, carry):
        sl = pl.ds(i * rows_per, rows_per)
        c = c_ref[sl, :]
        q = q_ref[sl, :].astype(F32)
        k = k_ref[sl, :].astype(F32)
        for hh in range(2):
            own = (lane < HEAD_DIM) if hh == 0 else (lane >= HEAD_DIM)
            a0 = HEAD_DIM if hh == 0 else 0
            ccol = jnp.sum(jnp.where(lane == 2 * p + hh, c, 0.0), axis=1, keepdims=True)
            hi, mid, lo = (t.astype(F32) for t in _split3(ccol))
            ones_q = (lane >= a0 + 3) & (lane < a0 + 6)
            ones_k = (lane >= a0) & (lane < a0 + 3)
            qaug = jnp.where(own, q, jnp.where(lane == a0, hi, jnp.where(lane == a0 + 1, mid, jnp.where(
                lane == a0 + 2, lo, jnp.where(ones_q, 1.0, 0.0)))))
            kaug = jnp.where(own, k, jnp.where(lane == a0 + 3, -hi, jnp.where(lane == a0 + 4, -mid, jnp.where(
                lane == a0 + 5, -lo, jnp.where(ones_k, 1.0, 0.0)))))
            qa_scr[hh, sl, :] = qaug.astype(BF16)
            ka_scr[hh, sl, :] = kaug.astype(BF16)
        return carry

    lax.fori_loop(0, S // rows_per, build, 0)

    row_t = lax.broadcasted_iota(jnp.int32, (tq, tk), 0)
    col_t = lax.broadcasted_iota(jnp.int32, (tq, tk), 1)
    head0 = lax.broadcasted_iota(jnp.int32, (tq, LANES), 1) < HEAD_DIM
    kt_per_q = tq // tk

    def update(state, s, vblk):
        m, l, acc = state
        m_new = jnp.maximum(m, jnp.max(s, axis=-1, keepdims=True))
        alpha = jnp.exp(m - m_new)
        e = jnp.exp(s - m_new)
        return (m_new, alpha * l + jnp.sum(e, axis=-1, keepdims=True),
                alpha * acc + _dot(e.astype(BF16), vblk))

    for qi in range(S // tq):
        qsl = pl.ds(qi * tq, tq)
        qs = [qa_scr[hh, qsl, :] for hh in range(2)]

        def k_body(kj, states, qs=qs):
            ksl = pl.ds(kj * tk, tk)
            vblk = v_ref[ksl, :]
            return tuple(update(states[hh], _dot_nt(qs[hh], ka_scr[hh, ksl, :]), vblk) for hh in range(2))

        init = (jnp.full((tq, 1), -jnp.inf, F32), jnp.zeros((tq, 1), F32), jnp.zeros((tq, LANES), F32))
        states = lax.fori_loop(0, qi * kt_per_q, k_body, (init, init))
        for d in range(kt_per_q):
            ksl = pl.ds((qi * kt_per_q + d) * tk, tk)
            causal = col_t + d * tk <= row_t
            vblk = v_ref[ksl, :]
            states = tuple(update(states[hh], jnp.where(causal, _dot_nt(qs[hh], ka_scr[hh, ksl, :]), -jnp.inf), vblk)
                           for hh in range(2))
        outs = [acc / l for (_, l, acc) in states]
        o_ref[qsl, :] = jnp.where(head0, outs[0], outs[1]).astype(BF16)


def _fox_attention(qkv, cum, B, S, tq=1024, tk=1024):
    n_pairs = WIDTH_B // LANES
    base = 3 * (WIDTH_A // LANES)
    blk = lambda off: pl.BlockSpec((None, S, LANES), lambda b, p: (b, 0, off + p))
    return pl.pallas_call(
        functools.partial(_fox_body, tq=tq, tk=tk),
        grid=(B, n_pairs),
        in_specs=[
            blk(base), blk(base + n_pairs), blk(base + 2 * n_pairs),
            pl.BlockSpec((None, S, LANES), lambda b, p: (b, 0, 0)),
        ],
        out_specs=pl.BlockSpec((None, S, LANES), lambda b, p: (b, 0, p)),
        out_shape=jax.ShapeDtypeStruct((B, S, WIDTH_B), BF16),
        scratch_shapes=[pltpu.VMEM((2, S, LANES), BF16), pltpu.VMEM((2, S, LANES), BF16)],
        compiler_params=_params(("parallel", "parallel")),
        name="fox_attention",
    )(qkv, qkv, qkv, cum)


def _outproj_ffn_body(x_ref, oa_ref, ob_ref, woa_ref, wob_ref, gain_ref, wg_ref, wu_ref, wd_ref, out_ref, a_scr):
    x1 = x_ref[...] + _dot(oa_ref[...], woa_ref[...]) + _dot(ob_ref[...], wob_ref[...])
    h = _rmsnorm(x1, gain_ref[...]).astype(BF16)
    for c in range(a_scr.shape[1] // 256):
        cols = slice(c * 256, (c + 1) * 256)
        g = _dot(h, wg_ref[:, cols])
        u = _dot(h, wu_ref[:, cols])
        a_scr[:, cols] = (g * _sigmoid(g) * u).astype(BF16)
    out_ref[...] = x1 + _dot(a_scr[...], wd_ref[...])


def _outproj_ffn(x2d, oa, ob, woa, wob, gain, wg, wu, wd, tm):
    T = x2d.shape[0]
    F = wg.shape[1]
    once = dict(pipeline_mode=pl.Buffered(1))
    return pl.pallas_call(
        _outproj_ffn_body,
        grid=(T // tm,),
        in_specs=[
            pl.BlockSpec((tm, D_MODEL), lambda i: (i, 0)),
            pl.BlockSpec((tm, WIDTH_A), lambda i: (i, 0)),
            pl.BlockSpec((tm, WIDTH_B), lambda i: (i, 0)),
            pl.BlockSpec((WIDTH_A, D_MODEL), lambda i: (0, 0), **once),
            pl.BlockSpec((WIDTH_B, D_MODEL), lambda i: (0, 0), **once),
            pl.BlockSpec((1, D_MODEL), lambda i: (0, 0)),
            pl.BlockSpec((D_MODEL, F), lambda i: (0, 0), **once),
            pl.BlockSpec((D_MODEL, F), lambda i: (0, 0), **once),
            pl.BlockSpec((F, D_MODEL), lambda i: (0, 0), **once),
        ],
        out_specs=pl.BlockSpec((tm, D_MODEL), lambda i: (i, 0)),
        out_shape=jax.ShapeDtypeStruct((T, D_MODEL), F32),
        scratch_shapes=[pltpu.VMEM((tm, F), BF16)],
        compiler_params=_params(("parallel",)),
        name="outproj_ffn",
    )(x2d, oa, ob, woa, wob, gain, wg, wu, wd)


def _inproj_c_body(x_ref, gain_ref, w_ref, qig_ref, f_ref):
    h = _rmsnorm(x_ref[...], gain_ref[...]).astype(BF16)
    n_qig = qig_ref.shape[1]
    for c in range(n_qig // 256):
        qig_ref[:, c * 256:(c + 1) * 256] = _dot(h, w_ref[:, c * 256:(c + 1) * 256]).astype(BF16)
    for c in range(f_ref.shape[1] // 256):
        f_ref[:, c * 256:(c + 1) * 256] = _dot(h, w_ref[:, n_qig + c * 256:n_qig + (c + 1) * 256])


def _inproj_c(x2d, gain, w, tm):
    T = x2d.shape[0]
    return pl.pallas_call(
        _inproj_c_body,
        grid=(T // tm,),
        in_specs=[
            pl.BlockSpec((tm, D_MODEL), lambda i: (i, 0)),
            pl.BlockSpec((1, D_MODEL), lambda i: (0, 0)),
            pl.BlockSpec((D_MODEL, 4 * D_MODEL), lambda i: (0, 0)),
        ],
        out_specs=[
            pl.BlockSpec((tm, 3 * D_MODEL), lambda i: (i, 0)),
            pl.BlockSpec((tm, D_MODEL), lambda i: (i, 0)),
        ],
        out_shape=[
            jax.ShapeDtypeStruct((T, 3 * D_MODEL), BF16),
            jax.ShapeDtypeStruct((T, D_MODEL), F32),
        ],
        compiler_params=_params(("parallel",)),
        name="inproj_c",
    )(x2d, gain, w)


def _hgrn_body(q_ref, i_ref, g_ref, f_ref, lb_ref, gn_ref, o_ref, state_scr, o_scr, ops0, ops1, dec0, dec1, *, heads):
    S = q_ref.shape[0]
    C = HGRN_CHUNK
    half = C // 2
    quarter = C // 4
    assert quarter == HGRN_LEAF
    n_chunks = S // C
    row = lax.broadcasted_iota(jnp.int32, (C, C), 0)
    col = lax.broadcasted_iota(jnp.int32, (C, C), 1)
    mask_cross = (row >= half) & (col < half)
    mask_same = ((row // half) == (col // half)) & (col <= row)
    r = lax.broadcasted_iota(jnp.int32, (C, heads * HGRN_DK), 0)
    scale = HGRN_DK ** -0.5
    gn = jnp.concatenate([gn_ref[...]] * heads, axis=1)
    state_scr[...] = jnp.zeros_like(state_scr)

    def prepare(ci, ops, dec):
        sl = pl.ds(ci * C, C)
        lbv = lb_ref[...]
        f = lbv + (1.0 - lbv) * _sigmoid(f_ref[sl, :])
        k = 1.0 - f
        q = q_ref[sl, :].astype(F32) * scale
        b = _cumsum_groups(jnp.log(f))
        b_last = b[C - 1:C, :]
        e_cross = b - b[half - 1:half, :]
        e_same = b - jnp.where(r < half, b[quarter - 1:quarter, :], b[half + quarter - 1:half + quarter, :])
        ops[0] = (q * jnp.exp(jnp.minimum(e_cross, 0.0))).astype(BF16)
        ops[1] = (k * jnp.exp(jnp.minimum(-e_cross, 0.0))).astype(BF16)
        ops[2] = (q * jnp.exp(e_same)).astype(BF16)
        ops[3] = (k * jnp.exp(-e_same)).astype(BF16)
        ops[4] = (q * jnp.exp(b)).astype(BF16)
        ops[5] = (k * jnp.exp(b_last - b)).astype(BF16)
        dec[0:1, :] = jnp.exp(b_last)

    def contract(ci, ops, dec):
        sl = pl.ds(ci * C, C)
        for hd in range(heads):
            cols = slice(hd * HGRN_DK, (hd + 1) * HGRN_DK)
            v = i_ref[sl, cols]
            scores = (jnp.where(mask_cross, _dot_nt(ops[0, :, cols], ops[1, :, cols]), 0.0)
                      + jnp.where(mask_same, _dot_nt(ops[2, :, cols], ops[3, :, cols]), 0.0))
            state_t = state_scr[hd]
            o_scr[sl, cols] = _dot(scores.astype(BF16), v) + _dot_nt(ops[4, :, cols], state_t.astype(BF16))
            v_t = v.astype(F32).T.astype(BF16)
            state_scr[hd] = state_t * dec[0:1, cols] + _dot(v_t, ops[5, :, cols])

    def finish(ci):
        sl = pl.ds(ci * C, C)
        o = o_scr[sl, :]
        ys = []
        for hd in range(heads):
            oh = o[:, hd * HGRN_DK:(hd + 1) * HGRN_DK]
            ys.append(oh * lax.rsqrt(jnp.mean(oh * oh, axis=-1, keepdims=True) + EPS))
        gate = g_ref[sl, :].astype(F32)
        o_ref[sl, :] = (jnp.concatenate(ys, axis=1) * gn * (gate * _sigmoid(gate))).astype(BF16)

    prepare(0, ops0, dec0)

    def pair(j, carry):
        prepare(2 * j + 1, ops1, dec1)
        contract(2 * j, ops0, dec0)
        prepare(2 * j + 2, ops0, dec0)
        contract(2 * j + 1, ops1, dec1)
        finish(jnp.maximum(2 * j - 1, 0))
        finish(2 * j)
        return carry

    lax.fori_loop(0, n_chunks // 2 - 1, pair, 0)
    prepare(n_chunks - 1, ops1, dec1)
    contract(n_chunks - 2, ops0, dec0)
    contract(n_chunks - 1, ops1, dec1)
    for ci in range(n_chunks - 3, n_chunks):
        finish(ci)


def _hgrn(qig, flog, lb, gn, B, S, heads=4):
    ng = N_HEADS_C // heads
    w = heads * HGRN_DK
    blk = lambda off: pl.BlockSpec((None, S, w), lambda b, h: (b, 0, off + h))
    return pl.pallas_call(
        functools.partial(_hgrn_body, heads=heads),
        grid=(B, ng),
        in_specs=[
            blk(0), blk(ng), blk(2 * ng),
            pl.BlockSpec((None, S, w), lambda b, h: (b, 0, h)),
            pl.BlockSpec((1, w), lambda b, h: (0, h)),
            pl.BlockSpec((1, HGRN_DK), lambda b, h: (0, 0)),
        ],
        out_specs=pl.BlockSpec((None, S, w), lambda b, h: (b, 0, h)),
        out_shape=jax.ShapeDtypeStruct((B, S, D_MODEL), BF16),
        scratch_shapes=[
            pltpu.VMEM((heads, HGRN_DK, HGRN_DK), F32), pltpu.VMEM((S, w), F32),
            pltpu.VMEM((6, HGRN_CHUNK, w), BF16), pltpu.VMEM((6, HGRN_CHUNK, w), BF16),
            pltpu.VMEM((SUBLANES, w), F32), pltpu.VMEM((SUBLANES, w), F32),
        ],
        compiler_params=_params(("parallel", "parallel")),
        name="hgrn2",
    )(qig, qig, qig, flog, lb, gn)


SEG_ALIGN = SUBLANES
SEG_BITS = tuple(range(9, 2, -1))


def _local_rows(tm):
    return TOP_K * tm + N_EXPERTS * SEG_ALIGN


def _outproj_router_body(x_ref, o_ref, w_ref, gain_ref, r_ref, x3_ref, xl_ref, pos_ref, gate_ref, seg_ref):
    tm = x_ref.shape[0]
    x3 = x_ref[...] + _dot(o_ref[...], w_ref[...])
    x3_ref[...] = x3
    h = _rmsnorm(x3, gain_ref[...])
    h_bf = h.astype(BF16)
    logits = _dot(h_bf, r_ref[...])
    lane = lax.broadcasted_iota(jnp.int32, logits.shape, 1)
    lane_f = lane.astype(F32)
    lg = jnp.where(lane < N_EXPERTS, logits, -jnp.inf)
    m1 = jnp.max(lg, axis=-1, keepdims=True)
    i1 = jnp.min(jnp.where(lg == m1, lane_f, float(LANES)), axis=-1, keepdims=True)
    lg2 = jnp.where(lane_f == i1, -jnp.inf, lg)
    m2 = jnp.max(lg2, axis=-1, keepdims=True)
    i2 = jnp.min(jnp.where(lg2 == m2, lane_f, float(LANES)), axis=-1, keepdims=True)
    e2 = jnp.exp(m2 - m1)
    den = 1.0 + e2
    gate_ref[...] = jnp.where(lane == 0, 1.0 / den, jnp.where(lane == 1, e2 / den, 0.0))

    oh1 = (lane_f == i1).astype(F32)
    oh2 = (lane_f == i2).astype(F32)
    c1 = _cumsum_groups(oh1)
    c2 = _cumsum_groups(oh2)
    n1 = c1[tm - 1:tm, :]
    count = n1 + c2[tm - 1:tm, :]
    padded = jnp.floor((count + (SEG_ALIGN - 1.0)) * (1.0 / SEG_ALIGN)) * SEG_ALIGN
    run = jnp.broadcast_to(padded, (SUBLANES, LANES))
    lane8 = lax.broadcasted_iota(jnp.int32, (SUBLANES, LANES), 1)
    for s in (1, 2, 4):
        run = run + jnp.where(lane8 >= s, pltpu.roll(run, s, 1), 0.0)
    start = run[0:1, :] - padded
    pos1 = jnp.sum(oh1 * (start + c1 - 1.0), axis=-1, keepdims=True)
    pos2 = jnp.sum(oh2 * (start + n1 + c2 - 1.0), axis=-1, keepdims=True)
    pos_ref[...] = jnp.where(lane == 0, pos1, jnp.where(lane == 1, pos2, 0.0)).astype(jnp.int32)
    pos1_row = jnp.broadcast_to(pos1, (tm, LANES)).T[0:1, :]
    pos2_row = jnp.broadcast_to(pos2, (tm, LANES)).T[0:1, :]
    slot = lax.broadcasted_iota(jnp.int32, (xl_ref.shape[0], tm), 0).astype(F32)
    perm = ((slot == pos1_row) | (slot == pos2_row)).astype(BF16)
    xl_ref[...] = _dot(perm, h_bf)
    r8 = lax.broadcasted_iota(jnp.int32, (SUBLANES, LANES), 0)
    seg_ref[...] = jnp.where(r8 == 0, count, jnp.where(r8 == 1, padded, jnp.where(r8 == 2, start, 0.0))).astype(jnp.int32)


def _outproj_router(x2d, o, w, gain, r, tm):
    T = x2d.shape[0]
    n_tt = T // tm
    lr = _local_rows(tm)
    return pl.pallas_call(
        _outproj_router_body,
        grid=(n_tt,),
        in_specs=[
            pl.BlockSpec((tm, D_MODEL), lambda i: (i, 0)),
            pl.BlockSpec((tm, D_MODEL), lambda i: (i, 0)),
            pl.BlockSpec((D_MODEL, D_MODEL), lambda i: (0, 0)),
            pl.BlockSpec((1, D_MODEL), lambda i: (0, 0)),
            pl.BlockSpec((D_MODEL, LANES), lambda i: (0, 0)),
        ],
        out_specs=[
            pl.BlockSpec((tm, D_MODEL), lambda i: (i, 0)),
            pl.BlockSpec((lr, D_MODEL), lambda i: (i, 0)),
            pl.BlockSpec((tm, LANES), lambda i: (i, 0)),
            pl.BlockSpec((tm, LANES), lambda i: (i, 0)),
            pl.BlockSpec((SUBLANES, LANES), lambda i: (i, 0)),
        ],
        out_shape=[
            jax.ShapeDtypeStruct((T, D_MODEL), F32),
            jax.ShapeDtypeStruct((n_tt * lr, D_MODEL), F32),
            jax.ShapeDtypeStruct((T, LANES), jnp.int32),
            jax.ShapeDtypeStruct((T, LANES), F32),
            jax.ShapeDtypeStruct((n_tt * SUBLANES, LANES), jnp.int32),
        ],
        compiler_params=_params(("parallel",)),
        name="outproj_router",
    )(x2d, o, w, gain, r)


def _run_copies(n, src, s0, dst, d0, sem):
    out = []
    for b in SEG_BITS:
        offs = (n >> (b + 1)) << (b + 1)
        cp = pltpu.make_async_copy(src.at[pl.ds(pl.multiple_of(s0 + offs, SEG_ALIGN), 1 << b), :],
                                   dst.at[pl.ds(pl.multiple_of(d0 + offs, SEG_ALIGN), 1 << b), :], sem)
        out.append((((n >> b) & 1) == 1, cp))
    return out


def _experts_body(te_ref, nu_ref, r0_ref, jlo_ref, jhi_ref, valid_ref, cs_ref, lp_ref, src_ref,
                  xl_hbm, wg_ref, wu_ref, wd_ref, out_ref, xbuf, xb_scr, a_scr, sem, *, n_tt):
    i = pl.program_id(0)
    f = pl.program_id(1)
    tm = out_ref.shape[0]

    def move(tile, slot, wait):
        base = te_ref[tile] * n_tt
        r0 = r0_ref[tile]

        def one_run(j, carry):
            c0 = cs_ref[base + j]
            lo = jnp.maximum(c0, r0)
            hi = jnp.minimum(c0 + lp_ref[base + j], r0 + tm)
            n = jnp.maximum(hi - lo, 0)
            for cond, cp in _run_copies(n, xl_hbm, src_ref[base + j] + (lo - c0), xbuf.at[slot], lo - r0, sem.at[slot]):
                @pl.when(cond)
                def _():
                    cp.wait() if wait else cp.start()
            return carry

        lax.fori_loop(jlo_ref[tile], jhi_ref[tile], one_run, 0)

    @pl.when(f == 0)
    def _():
        @pl.when(i == 0)
        def _():
            xbuf[...] = jnp.zeros_like(xbuf)
            move(0, 0, False)

        @pl.when(i < nu_ref[0])
        def _():
            move(i, i % 2, True)

        @pl.when(i + 1 < nu_ref[0])
        def _():
            move(i + 1, (i + 1) % 2, False)

        row = lax.broadcasted_iota(jnp.int32, (tm, D_MODEL), 0)
        xb_scr[...] = jnp.where(row < valid_ref[i], xbuf[i % 2], 0.0).astype(BF16)
        out_ref[...] = jnp.zeros_like(out_ref)

    @pl.when(i < nu_ref[0])
    def _():
        xb = xb_scr[...]
        for c in range(a_scr.shape[1] // 256):
            cols = slice(c * 256, (c + 1) * 256)
            g = _dot(xb, wg_ref[:, cols])
            u = _dot(xb, wu_ref[:, cols])
            a_scr[:, cols] = (g * _sigmoid(g) * u).astype(BF16)
        out_ref[...] += _dot(a_scr[...], wd_ref[...])


def _moe_experts(tabs, xl, wg, wu, wd, n_rows, n_tt, tm, tf):
    F = wg.shape[2]
    nf = F // tf

    def f_eff(i, f, nu):
        return jnp.where(i < nu[0], f, nf - 1)

    return pl.pallas_call(
        functools.partial(_experts_body, n_tt=n_tt),
        grid_spec=pltpu.PrefetchScalarGridSpec(
            num_scalar_prefetch=len(tabs),
            grid=(n_rows // tm, nf),
            in_specs=[
                pl.BlockSpec(memory_space=pl.ANY),
                pl.BlockSpec((None, D_MODEL, tf), lambda i, f, te, nu, *_: (te[i], 0, f_eff(i, f, nu))),
                pl.BlockSpec((None, D_MODEL, tf), lambda i, f, te, nu, *_: (te[i], 0, f_eff(i, f, nu))),
                pl.BlockSpec((None, tf, D_MODEL), lambda i, f, te, nu, *_: (te[i], f_eff(i, f, nu), 0)),
            ],
            out_specs=pl.BlockSpec((tm, D_MODEL), lambda i, f, *_: (i, 0)),
            scratch_shapes=[pltpu.VMEM((2, tm, D_MODEL), F32), pltpu.VMEM((tm, D_MODEL), BF16),
                            pltpu.VMEM((tm, tf), BF16), pltpu.SemaphoreType.DMA((2,))],
        ),
        out_shape=jax.ShapeDtypeStruct((n_rows, D_MODEL), F32),
        compiler_params=_params(("arbitrary", "arbitrary")),
        name="moe_experts",
    )(*tabs, xl, wg, wu, wd)


def _combine_body(row_ref, lp_ref, off_ref, x_ref, pos_ref, gate_ref, ys_hbm, gain_ref, out_ref, yl, sem):
    j = pl.program_id(0)
    tm = x_ref.shape[0]
    lr = yl.shape[1]

    def move(tile, slot, wait):
        for e in range(N_EXPERTS):
            t = tile * N_EXPERTS + e
            for cond, cp in _run_copies(lp_ref[t], ys_hbm, row_ref[t], yl.at[slot], off_ref[t], sem.at[slot]):
                @pl.when(cond)
                def _():
                    cp.wait() if wait else cp.start()

    @pl.when(j == 0)
    def _():
        yl[...] = jnp.zeros_like(yl)
        move(0, 0, False)

    move(j, j % 2, True)

    @pl.when(j + 1 < pl.num_programs(0))
    def _():
        move(j + 1, (j + 1) % 2, False)

    last = j * N_EXPERTS + N_EXPERTS - 1
    used = off_ref[last] + lp_ref[last]
    row = lax.broadcasted_iota(jnp.int32, (lr, D_MODEL), 0)
    y_sorted = jnp.where(row < used, yl[j % 2], 0.0).astype(BF16)
    pos = pos_ref[...]
    gates = gate_ref[...]
    slot = lax.broadcasted_iota(jnp.int32, (tm, lr), 1)
    weights = jnp.where(slot == pos[:, 0:1], gates[:, 0:1], jnp.where(slot == pos[:, 1:2], gates[:, 1:2], 0.0))
    y = x_ref[...] + _dot(weights.astype(BF16), y_sorted)
    out_ref[...] = _rmsnorm(y, gain_ref[...])


def _moe_combine(tabs, x3, pos, gates, ys, gain, tm):
    T = x3.shape[0]
    return pl.pallas_call(
        _combine_body,
        grid_spec=pltpu.PrefetchScalarGridSpec(
            num_scalar_prefetch=len(tabs),
            grid=(T // tm,),
            in_specs=[
                pl.BlockSpec((tm, D_MODEL), lambda i, *_: (i, 0)),
                pl.BlockSpec((tm, LANES), lambda i, *_: (i, 0)),
                pl.BlockSpec((tm, LANES), lambda i, *_: (i, 0)),
                pl.BlockSpec(memory_space=pl.ANY),
                pl.BlockSpec((1, D_MODEL), lambda i, *_: (0, 0)),
            ],
            out_specs=pl.BlockSpec((tm, D_MODEL), lambda i, *_: (i, 0)),
            scratch_shapes=[pltpu.VMEM((2, _local_rows(tm), D_MODEL), F32), pltpu.SemaphoreType.DMA((2,))],
        ),
        out_shape=jax.ShapeDtypeStruct((T, D_MODEL), F32),
        compiler_params=_params(("arbitrary",)),
        name="moe_combine",
    )(*tabs, x3, pos, gates, ys, gain)


def _routing_tables(seg, n_tt, tm, tm_e):
    seg = seg.reshape(n_tt, SUBLANES, LANES)
    lp = seg[:, 1, :N_EXPERTS]
    off = seg[:, 2, :N_EXPERTS]
    cs = jnp.cumsum(lp, axis=0) - lp
    total = jnp.sum(lp, axis=0)
    padded = ((total + tm_e - 1) // tm_e) * tm_e
    ends = jnp.cumsum(padded)
    starts = ends - padded
    n_rows = -(-(TOP_K * n_tt * tm + N_EXPERTS * (SEG_ALIGN - 1) * n_tt) // tm_e) * tm_e + N_EXPERTS * tm_e
    tile_start = jnp.arange(n_rows // tm_e, dtype=jnp.int32) * tm_e
    te = jnp.minimum(jnp.sum((tile_start[:, None] >= ends[None, :]).astype(jnp.int32), axis=1), N_EXPERTS - 1)
    n_used = (ends[-1] // tm_e).astype(jnp.int32).reshape(1)
    r0 = tile_start - starts[te]
    cs_t = cs[:, te]
    run_end_t = cs_t + lp[:, te]
    jlo = jnp.sum((run_end_t <= r0[None, :]).astype(jnp.int32), axis=0)
    jhi = jnp.sum((cs_t < (r0 + tm_e)[None, :]).astype(jnp.int32), axis=0)
    valid = jnp.clip(total[te] - r0, 0, tm_e)
    src = jnp.arange(n_tt, dtype=jnp.int32)[:, None] * _local_rows(tm) + off
    i32 = lambda a: a.astype(jnp.int32)
    expert_tabs = (i32(te), n_used, i32(r0), i32(jlo), i32(jhi), i32(valid),
                   i32(cs.T.reshape(-1)), i32(lp.T.reshape(-1)), i32(src.T.reshape(-1)))
    combine_tabs = (i32((starts[None, :] + cs).reshape(-1)), i32(lp.reshape(-1)), i32(off.reshape(-1)))
    return expert_tabs, combine_tabs, n_rows


def _rope_tables(positions):
    B, S = positions.shape
    inv_freq = jnp.power(jnp.float32(ROPE_THETA), -jnp.arange(ROPE_HALF, dtype=F32) / ROPE_HALF)
    ang = positions.astype(F32)[..., None] * inv_freq
    cos, sin = jnp.cos(ang), jnp.sin(ang)
    rest = HEAD_DIM - ROPE_DIM
    cos_h = jnp.concatenate([cos, cos, jnp.ones((B, S, rest), F32)], axis=-1)
    sin_h = jnp.concatenate([-sin, sin, jnp.zeros((B, S, rest), F32)], axis=-1)
    reps = LANES // HEAD_DIM
    return (jnp.tile(cos_h, (1, 1, reps)).reshape(B * S, LANES),
            jnp.tile(sin_h, (1, 1, reps)).reshape(B * S, LANES))


def kernel(x, positions, norm_mix, norm_ffn, w_in_ab, fgate_bias, w_out_ab, w_in_c, lower_bounds, gnorm_c, w_out_c,
           w_gate_ffn, w_up_ffn, w_down_ffn, router, w_gate_moe, w_up_moe, w_down_moe, norm_final):
    B, S, D = x.shape
    T = B * S
    assert D == D_MODEL and S % 1024 == 0
    tm = 512
    x2d = x.reshape(T, D)

    w_ab = jnp.pad(w_in_ab[0], ((0, 0), (0, QKV_WIDTH + LANES - w_in_ab.shape[2]))).astype(BF16)
    cos, sin = _rope_tables(positions)
    bias = jnp.pad(fgate_bias[0], (0, LANES - N_HEADS_B)).reshape(1, LANES)
    qkv, cum = _inproj_ab(x2d, norm_mix[0:1], w_ab, cos, sin, bias, tm, S)
    qkv = qkv.reshape(B, S, QKV_WIDTH)
    out_a = _dilated_attention(qkv, B, S).reshape(T, WIDTH_A)
    out_b = _fox_attention(qkv, cum.reshape(B, S, LANES), B, S).reshape(T, WIDTH_B)
    w_o = w_out_ab[0].astype(BF16)
    x2 = _outproj_ffn(x2d, out_a, out_b, w_o[:WIDTH_A], w_o[WIDTH_A:], norm_ffn[0:1],
                      w_gate_ffn[0].astype(BF16), w_up_ffn[0].astype(BF16), w_down_ffn[0].astype(BF16), tm)

    lb_all = jnp.cumsum(jax.nn.softmax(lower_bounds.astype(F32), axis=0), axis=0)
    lb = (lb_all - lb_all[0:1])[1].reshape(1, D)
    wq, wf, wi, wg = jnp.split(w_in_c[0], 4, axis=-1)
    w_c = jnp.concatenate([wq, wi, wg, wf], axis=-1).astype(BF16)
    qig, flog_c = _inproj_c(x2, norm_mix[1:2], w_c, tm)
    o_c = _hgrn(qig.reshape(B, S, 3 * D), flog_c.reshape(B, S, D), lb, gnorm_c[0:1], B, S).reshape(T, D)

    r_pad = jnp.pad(router[0], ((0, 0), (0, LANES - N_EXPERTS))).astype(BF16)
    x3, xl, pos, gates, seg = _outproj_router(x2, o_c, w_out_c[0].astype(BF16), norm_ffn[1:2], r_pad, tm)
    tm_e = 1024
    expert_tabs, combine_tabs, n_rows = _routing_tables(seg, T // tm, tm, tm_e)
    ys = _moe_experts(expert_tabs, xl, w_gate_moe[0].astype(BF16), w_up_moe[0].astype(BF16),
                      w_down_moe[0].astype(BF16), n_rows, T // tm, tm_e, w_gate_moe.shape[3] // 2)
    out = _moe_combine(combine_tabs, x3, pos, gates, ys, norm_final.reshape(1, D), tm)
    return out.reshape(B, S, D)
```

```python
import functools

import jax
import jax.numpy as jnp
from jax import lax
from jax.experimental import pallas as pl
from jax.experimental.pallas import tpu as pltpu

F32 = jnp.float32
BF16 = jnp.bfloat16

D_MODEL = 1024
HEAD_DIM = 64
N_HEADS_A = 8
N_HEADS_B = 8
WIDTH_A = N_HEADS_A * HEAD_DIM
WIDTH_B = N_HEADS_B * HEAD_DIM
QKV_WIDTH = 3 * (WIDTH_A + WIDTH_B)
ROPE_THETA = 500000.0
ROPE_DIM = HEAD_DIM // 4
ROPE_HALF = ROPE_DIM // 2
ATT_BLOCK = 128
DILATIONS = (1, 4, 16)
N_HEADS_C = 8
HGRN_DK = 128
HGRN_CHUNK = 64
HGRN_LEAF = 16
N_EXPERTS = 8
TOP_K = 2
EPS = 1e-6
LOG2E = 1.4426950408889634

LANES = 128
SUBLANES = 8
VMEM_LIMIT = 56 * 1024 * 1024

NT_DIMS = (((1,), (1,)), ((), ()))


def _params(semantics, **kw):
    return pltpu.CompilerParams(dimension_semantics=semantics, vmem_limit_bytes=VMEM_LIMIT, **kw)


def _rmsnorm(x, gain):
    return x * lax.rsqrt(jnp.mean(x * x, axis=-1, keepdims=True) + EPS) * gain


def _sigmoid(x):
    return 1.0 / (1.0 + jnp.exp(-x))


def _split3(x):
    hi = x.astype(BF16)
    r1 = x - hi.astype(F32)
    mid = r1.astype(BF16)
    lo = (r1 - mid.astype(F32)).astype(BF16)
    return hi, mid, lo


def _dot(a, b):
    return jnp.dot(a, b, preferred_element_type=F32)


def _dot_nt(a, b):
    return lax.dot_general(a, b, NT_DIMS, preferred_element_type=F32)


def _cumsum_groups(x):
    n, w = x.shape
    rows = lax.broadcasted_iota(jnp.int32, (SUBLANES, w), 0)
    out, carry = [], None
    for g in range(n // SUBLANES):
        xg = x[SUBLANES * g:SUBLANES * (g + 1), :]
        for s in (1, 2, 4):
            xg = xg + jnp.where(rows >= s, pltpu.roll(xg, s, 0), 0.0)
        if carry is not None:
            xg = xg + carry
        carry = xg[SUBLANES - 1:SUBLANES, :]
        out.append(xg)
    return jnp.concatenate(out, axis=0)


def _inproj_ab_body(x_ref, gain_ref, w_ref, cos_ref, sin_ref, bias_ref, qkv_ref, cum_ref, carry_scr, *, tiles_per_seq):
    h = _rmsnorm(x_ref[...], gain_ref[...]).astype(BF16)

    @pl.when(pl.program_id(0) % tiles_per_seq == 0)
    def _():
        carry_scr[...] = jnp.zeros_like(carry_scr)

    x = _dot(h, w_ref[:, QKV_WIDTH:QKV_WIDTH + LANES]) + bias_ref[...]
    logf = -(jnp.maximum(-x, 0.0) + jnp.log1p(jnp.exp(-jnp.abs(x))))
    cum = _cumsum_groups(logf * LOG2E) + carry_scr[0:1, :]
    cum_ref[...] = cum
    carry_scr[0:1, :] = cum[cum.shape[0] - 1:, :]

    cos = cos_ref[...]
    sin = sin_ref[...]
    lane = lax.broadcasted_iota(jnp.int32, cos.shape, 1)
    low = (lane & (HEAD_DIM - 1)) < ROPE_HALF
    scale = HEAD_DIM ** -0.5 * LOG2E
    for c in range(QKV_WIDTH // 256):
        y = _dot(h, w_ref[:, c * 256:(c + 1) * 256])
        seg = c // 2
        for s in range(2):
            yy = y[:, s * LANES:(s + 1) * LANES]
            if seg in (0, 1):
                partner = jnp.where(low, pltpu.roll(yy, LANES - ROPE_HALF, 1), pltpu.roll(yy, ROPE_HALF, 1))
                yy = yy * cos + partner * sin
            if seg in (0, 3):
                yy = yy * scale
            qkv_ref[:, c * 256 + s * LANES:c * 256 + (s + 1) * LANES] = yy.astype(BF16)


def _inproj_ab(x2d, gain, w, cos, sin, bias, tm, seq_len):
    T = x2d.shape[0]
    wn = w.shape[1]
    return pl.pallas_call(
        functools.partial(_inproj_ab_body, tiles_per_seq=seq_len // tm),
        grid=(T // tm,),
        in_specs=[
            pl.BlockSpec((tm, D_MODEL), lambda i: (i, 0)),
            pl.BlockSpec((1, D_MODEL), lambda i: (0, 0)),
            pl.BlockSpec((D_MODEL, wn), lambda i: (0, 0)),
            pl.BlockSpec((tm, LANES), lambda i: (i, 0)),
            pl.BlockSpec((tm, LANES), lambda i: (i, 0)),
            pl.BlockSpec((1, LANES), lambda i: (0, 0)),
        ],
        out_specs=[
            pl.BlockSpec((tm, QKV_WIDTH), lambda i: (i, 0)),
            pl.BlockSpec((tm, LANES), lambda i: (i, 0)),
        ],
        out_shape=[
            jax.ShapeDtypeStruct((T, QKV_WIDTH), BF16),
            jax.ShapeDtypeStruct((T, LANES), F32),
        ],
        scratch_shapes=[pltpu.VMEM((SUBLANES, LANES), F32)],
        compiler_params=_params(("arbitrary",)),
        name="inproj_ab",
    )(x2d, gain, w, cos, sin, bias)


def _dilated_body(q_ref, k_ref, v_ref, o_ref, qf, kf, vf, ob, lb):
    S = q_ref.shape[0]
    nb = ATT_BLOCK
    qf[...] = q_ref[...].astype(F32)
    kf[...] = k_ref[...].astype(F32)
    vf[...] = v_ref[...].astype(F32)
    head0 = lax.broadcasted_iota(jnp.int32, (nb, LANES), 1) < HEAD_DIM
    qi2 = lax.broadcasted_iota(jnp.int32, (2 * nb, 2 * nb), 0) & (nb - 1)
    kj2 = lax.broadcasted_iota(jnp.int32, (2 * nb, 2 * nb), 1)
    valid2 = (kj2 >= qi2) & (kj2 <= qi2 + nb)
    qi1 = lax.broadcasted_iota(jnp.int32, (2 * nb, nb), 0) & (nb - 1)
    kj1 = lax.broadcasted_iota(jnp.int32, (2 * nb, nb), 1)
    valid1 = kj1 <= qi1

    def rows(start, size, r):
        return pl.ds(start, size) if r == 1 else pl.ds(start, size, stride=r)

    def block(br, r, q0, k0, nk):
        qs = qf[rows(q0, nb, r), :]
        kb = kf[rows(k0, nk, r), :].astype(BF16)
        vb = vf[rows(k0, nk, r), :].astype(BF16)
        q2 = jnp.concatenate([jnp.where(head0, qs, 0.0), jnp.where(head0, 0.0, qs)], axis=0).astype(BF16)
        s = jnp.where(valid2 if nk == 2 * nb else valid1, _dot_nt(q2, kb), -jnp.inf)
        m = jnp.max(s, axis=-1, keepdims=True)
        e = jnp.exp2(s - m)
        l = jnp.sum(e, axis=-1, keepdims=True)
        o = _dot(e.astype(BF16), vb) / l
        lse = jnp.broadcast_to(m + jnp.log2(l), (2 * nb, LANES))
        ob[br, rows(q0, nb, r), :] = jnp.where(head0, o[:nb], o[nb:])
        lb[br, rows(q0, nb, r), :] = jnp.where(head0, lse[:nb], lse[nb:])

    for br, r in enumerate(DILATIONS):
        n_blocks = S // (r * nb)
        if n_blocks == 1:
            def first_only(c, carry, br=br, r=r):
                block(br, r, c, c, nb)
                return carry
            lax.fori_loop(0, r, first_only, 0, unroll=True)
        else:
            for c in range(r):
                block(br, r, c, c, nb)

                def later(n, carry, br=br, r=r, c=c):
                    block(br, r, n * (nb * r) + c, (n - 1) * (nb * r) + c, 2 * nb)
                    return carry
                lax.fori_loop(1, n_blocks, later, 0, unroll=True)

    rows_per = 256
    for ch in range(S // rows_per):
        sl = pl.ds(ch * rows_per, rows_per)
        l0, l1, l2 = lb[0, sl, :], lb[1, sl, :], lb[2, sl, :]
        m = jnp.maximum(jnp.maximum(l0, l1), l2)
        w0, w1, w2 = jnp.exp2(l0 - m), jnp.exp2(l1 - m), jnp.exp2(l2 - m)
        o = (w0 * ob[0, sl, :] + w1 * ob[1, sl, :] + w2 * ob[2, sl, :]) / (w0 + w1 + w2)
        o_ref[sl, :] = o.astype(BF16)


def _dilated_attention(qkv, B, S):
    n_pairs = WIDTH_A // LANES
    blk = lambda off: pl.BlockSpec((None, S, LANES), lambda b, p: (b, 0, off + p))
    return pl.pallas_call(
        _dilated_body,
        grid=(B, n_pairs),
        in_specs=[blk(0), blk(n_pairs), blk(2 * n_pairs)],
        out_specs=pl.BlockSpec((None, S, LANES), lambda b, p: (b, 0, p)),
        out_shape=jax.ShapeDtypeStruct((B, S, WIDTH_A), BF16),
        scratch_shapes=[
            pltpu.VMEM((S, LANES), F32), pltpu.VMEM((S, LANES), F32), pltpu.VMEM((S, LANES), F32),
            pltpu.VMEM((len(DILATIONS), S, LANES), F32), pltpu.VMEM((len(DILATIONS), S, LANES), F32),
        ],
        compiler_params=_params(("parallel", "parallel")),
        name="dilated_attention",
    )(qkv, qkv, qkv)


def _fox_body(q_ref, k_ref, v_ref, c_ref, o_ref, qa_scr, ka_scr, *, tk):
    S = q_ref.shape[0]
    p = pl.program_id(1)
    rows_per = 256
    lane = lax.broadcasted_iota(jnp.int32, (rows_per, LANES), 1)

    def build(i, carry):
        sl = pl.ds(i * rows_per, rows_per)
        c = c_ref[sl, :]
        q = q_ref[sl, :].astype(F32)
        k = k_ref[sl, :].astype(F32)
        for hh in range(2):
            own = (lane < HEAD_DIM) if hh == 0 else (lane >= HEAD_DIM)
            a0 = HEAD_DIM if hh == 0 else 0
            ccol = jnp.sum(jnp.where(lane == 2 * p + hh, c, 0.0), axis=1, keepdims=True)
            hi, mid, lo = (t.astype(F32) for t in _split3(ccol))
            ones_q = (lane >= a0 + 3) & (lane < a0 + 6)
            ones_k = (lane >= a0) & (lane < a0 + 3)
            qaug = jnp.where(own, q, jnp.where(lane == a0, hi, jnp.where(lane == a0 + 1, mid, jnp.where(
                lane == a0 + 2, lo, jnp.where(ones_q, 1.0, 0.0)))))
            kaug = jnp.where(own, k, jnp.where(lane == a0 + 3, -hi, jnp.where(lane == a0 + 4, -mid, jnp.where(
                lane == a0 + 5, -lo, jnp.where(ones_k, 1.0, 0.0)))))
            qa_scr[hh, sl, :] = qaug.astype(BF16)
            ka_scr[hh, sl, :] = kaug.astype(BF16)
        return carry

    lax.fori_loop(0, S // rows_per, build, 0)

    n_blocks = S // tk
    row_t = lax.broadcasted_iota(jnp.int32, (tk, tk), 0)
    col_t = lax.broadcasted_iota(jnp.int32, (tk, tk), 1)
    causal = col_t <= row_t
    head0 = lax.broadcasted_iota(jnp.int32, (tk, LANES), 1) < HEAD_DIM
    state = [[None] * n_blocks for _ in range(2)]
    for j in range(n_blocks):
        ksl = pl.ds(j * tk, tk)
        vblk = v_ref[ksl, :]
        for hh in range(2):
            s_all = _dot_nt(qa_scr[hh, pl.ds(j * tk, S - j * tk), :], ka_scr[hh, ksl, :])
            es, scales = [], []
            for rb in range(j, n_blocks):
                s = s_all[(rb - j) * tk:(rb - j + 1) * tk, :]
                if rb == j:
                    s = jnp.where(causal, s, -jnp.inf)
                if j == 0:
                    m_new = jnp.max(s, axis=-1, keepdims=True)
                    alpha = None
                else:
                    m_old = state[hh][rb][0]
                    m_new = jnp.maximum(m_old, jnp.max(s, axis=-1, keepdims=True))
                    alpha = jnp.exp2(m_old - m_new)
                e = jnp.exp2(s - m_new)
                es.append(e.astype(BF16))
                scales.append((m_new, alpha, jnp.sum(e, axis=-1, keepdims=True)))
            pv_all = _dot(jnp.concatenate(es, axis=0) if len(es) > 1 else es[0], vblk)
            for rb in range(j, n_blocks):
                m_new, alpha, rowsum = scales[rb - j]
                pv = pv_all[(rb - j) * tk:(rb - j + 1) * tk, :]
                if alpha is None:
                    state[hh][rb] = (m_new, rowsum, pv)
                else:
                    _, l_old, acc_old = state[hh][rb]
                    state[hh][rb] = (m_new, alpha * l_old + rowsum, alpha * acc_old + pv)
        outs = [state[hh][j][2] / state[hh][j][1] for hh in range(2)]
        o_ref[pl.ds(j * tk, tk), :] = jnp.where(head0, outs[0], outs[1]).astype(BF16)


def _fox_attention(qkv, cum, B, S, tk=512):
    n_pairs = WIDTH_B // LANES
    base = 3 * (WIDTH_A // LANES)
    blk = lambda off: pl.BlockSpec((None, S, LANES), lambda b, p: (b, 0, off + p))
    return pl.pallas_call(
        functools.partial(_fox_body, tk=tk),
        grid=(B, n_pairs),
        in_specs=[
            blk(base), blk(base + n_pairs), blk(base + 2 * n_pairs),
            pl.BlockSpec((None, S, LANES), lambda b, p: (b, 0, 0)),
        ],
        out_specs=pl.BlockSpec((None, S, LANES), lambda b, p: (b, 0, p)),
        out_shape=jax.ShapeDtypeStruct((B, S, WIDTH_B), BF16),
        scratch_shapes=[pltpu.VMEM((2, S, LANES), BF16), pltpu.VMEM((2, S, LANES), BF16)],
        compiler_params=_params(("parallel", "parallel")),
        name="fox_attention",
    )(qkv, qkv, qkv, cum)


def _outproj_ffn_body(x_ref, oa_ref, ob_ref, woa_ref, wob_ref, gain_ref, wg_ref, wu_ref, wd_ref, out_ref, a_scr):
    x1 = x_ref[...] + _dot(oa_ref[...], woa_ref[...]) + _dot(ob_ref[...], wob_ref[...])
    h = _rmsnorm(x1, gain_ref[...]).astype(BF16)
    for c in range(a_scr.shape[1] // 256):
        cols = slice(c * 256, (c + 1) * 256)
        g = _dot(h, wg_ref[:, cols])
        u = _dot(h, wu_ref[:, cols])
        a_scr[:, cols] = (g * _sigmoid(g) * u).astype(BF16)
    out_ref[...] = x1 + _dot(a_scr[...], wd_ref[...])


def _outproj_ffn(x2d, oa, ob, woa, wob, gain, wg, wu, wd, tm):
    T = x2d.shape[0]
    F = wg.shape[1]
    once = dict(pipeline_mode=pl.Buffered(1))
    return pl.pallas_call(
        _outproj_ffn_body,
        grid=(T // tm,),
        in_specs=[
            pl.BlockSpec((tm, D_MODEL), lambda i: (i, 0)),
            pl.BlockSpec((tm, WIDTH_A), lambda i: (i, 0)),
            pl.BlockSpec((tm, WIDTH_B), lambda i: (i, 0)),
            pl.BlockSpec((WIDTH_A, D_MODEL), lambda i: (0, 0), **once),
            pl.BlockSpec((WIDTH_B, D_MODEL), lambda i: (0, 0), **once),
            pl.BlockSpec((1, D_MODEL), lambda i: (0, 0)),
            pl.BlockSpec((D_MODEL, F), lambda i: (0, 0), **once),
            pl.BlockSpec((D_MODEL, F), lambda i: (0, 0), **once),
            pl.BlockSpec((F, D_MODEL), lambda i: (0, 0), **once),
        ],
        out_specs=pl.BlockSpec((tm, D_MODEL), lambda i: (i, 0)),
        out_shape=jax.ShapeDtypeStruct((T, D_MODEL), F32),
        scratch_shapes=[pltpu.VMEM((tm, F), BF16)],
        compiler_params=_params(("parallel",)),
        name="outproj_ffn",
    )(x2d, oa, ob, woa, wob, gain, wg, wu, wd)


def _inproj_c_body(x_ref, gain_ref, w_ref, qig_ref, f_ref):
    h = _rmsnorm(x_ref[...], gain_ref[...]).astype(BF16)
    n_qig = qig_ref.shape[1]
    for c in range(n_qig // 256):
        qig_ref[:, c * 256:(c + 1) * 256] = _dot(h, w_ref[:, c * 256:(c + 1) * 256]).astype(BF16)
    for c in range(f_ref.shape[1] // 256):
        f_ref[:, c * 256:(c + 1) * 256] = _dot(h, w_ref[:, n_qig + c * 256:n_qig + (c + 1) * 256])


def _inproj_c(x2d, gain, w, tm):
    T = x2d.shape[0]
    return pl.pallas_call(
        _inproj_c_body,
        grid=(T // tm,),
        in_specs=[
            pl.BlockSpec((tm, D_MODEL), lambda i: (i, 0)),
            pl.BlockSpec((1, D_MODEL), lambda i: (0, 0)),
            pl.BlockSpec((D_MODEL, 4 * D_MODEL), lambda i: (0, 0)),
        ],
        out_specs=[
            pl.BlockSpec((tm, 3 * D_MODEL), lambda i: (i, 0)),
            pl.BlockSpec((tm, D_MODEL), lambda i: (i, 0)),
        ],
        out_shape=[
            jax.ShapeDtypeStruct((T, 3 * D_MODEL), BF16),
            jax.ShapeDtypeStruct((T, D_MODEL), F32),
        ],
        compiler_params=_params(("parallel",)),
        name="inproj_c",
    )(x2d, gain, w)


def _hgrn_body(q_ref, i_ref, g_ref, f_ref, lb_ref, gn_ref, o_ref, state_scr, o_scr, ops0, ops1, dec0, dec1, *, heads):
    S = q_ref.shape[0]
    C = HGRN_CHUNK
    half = C // 2
    quarter = C // 4
    assert quarter == HGRN_LEAF
    n_chunks = S // C
    row = lax.broadcasted_iota(jnp.int32, (C, C), 0)
    col = lax.broadcasted_iota(jnp.int32, (C, C), 1)
    mask_cross = (row >= half) & (col < half)
    mask_same = ((row // half) == (col // half)) & (col <= row)
    r = lax.broadcasted_iota(jnp.int32, (C, heads * HGRN_DK), 0)
    scale = HGRN_DK ** -0.5
    gn = jnp.concatenate([gn_ref[...]] * heads, axis=1)
    state_scr[...] = jnp.zeros_like(state_scr)

    def prepare(ci, ops, dec):
        sl = pl.ds(ci * C, C)
        lbv = lb_ref[...]
        f = lbv + (1.0 - lbv) * _sigmoid(f_ref[sl, :])
        k = 1.0 - f
        q = q_ref[sl, :].astype(F32) * scale
        b = _cumsum_groups(jnp.log2(f))
        b_last = b[C - 1:C, :]
        e_cross = b - b[half - 1:half, :]
        e_same = b - jnp.where(r < half, b[quarter - 1:quarter, :], b[half + quarter - 1:half + quarter, :])
        ops[0] = (q * jnp.exp2(jnp.minimum(e_cross, 0.0))).astype(BF16)
        ops[1] = (k * jnp.exp2(jnp.minimum(-e_cross, 0.0))).astype(BF16)
        ops[2] = (q * jnp.exp2(e_same)).astype(BF16)
        ops[3] = (k * jnp.exp2(-e_same)).astype(BF16)
        ops[4] = (q * jnp.exp2(b)).astype(BF16)
        ops[5] = (k * jnp.exp2(b_last - b)).astype(BF16)
        dec[0:1, :] = jnp.exp2(b_last)

    def contract(ci, ops, dec):
        sl = pl.ds(ci * C, C)
        for hd in range(heads):
            cols = slice(hd * HGRN_DK, (hd + 1) * HGRN_DK)
            v = i_ref[sl, cols]
            scores = (jnp.where(mask_cross, _dot_nt(ops[0, :, cols], ops[1, :, cols]), 0.0)
                      + jnp.where(mask_same, _dot_nt(ops[2, :, cols], ops[3, :, cols]), 0.0))
            state_t = state_scr[hd]
            o_scr[sl, cols] = _dot(scores.astype(BF16), v) + _dot_nt(ops[4, :, cols], state_t.astype(BF16))
            v_t = v.astype(F32).T.astype(BF16)
            state_scr[hd] = state_t * dec[0:1, cols] + _dot(v_t, ops[5, :, cols])

    def finish(ci):
        sl = pl.ds(ci * C, C)
        o = o_scr[sl, :]
        ys = []
        for hd in range(heads):
            oh = o[:, hd * HGRN_DK:(hd + 1) * HGRN_DK]
            ys.append(oh * lax.rsqrt(jnp.mean(oh * oh, axis=-1, keepdims=True) + EPS))
        gate = g_ref[sl, :].astype(F32)
        o_ref[sl, :] = (jnp.concatenate(ys, axis=1) * gn * (gate * _sigmoid(gate))).astype(BF16)

    prepare(0, ops0, dec0)

    def pair(j, carry):
        prepare(2 * j + 1, ops1, dec1)
        contract(2 * j, ops0, dec0)
        prepare(2 * j + 2, ops0, dec0)
        contract(2 * j + 1, ops1, dec1)
        finish(jnp.maximum(2 * j - 1, 0))
        finish(2 * j)
        return carry

    lax.fori_loop(0, n_chunks // 2 - 1, pair, 0)
    prepare(n_chunks - 1, ops1, dec1)
    contract(n_chunks - 2, ops0, dec0)
    contract(n_chunks - 1, ops1, dec1)
    for ci in range(n_chunks - 3, n_chunks):
        finish(ci)


def _hgrn(qig, flog, lb, gn, B, S, heads=4):
    ng = N_HEADS_C // heads
    w = heads * HGRN_DK
    blk = lambda off: pl.BlockSpec((None, S, w), lambda b, h: (b, 0, off + h))
    return pl.pallas_call(
        functools.partial(_hgrn_body, heads=heads),
        grid=(B, ng),
        in_specs=[
            blk(0), blk(ng), blk(2 * ng),
            pl.BlockSpec((None, S, w), lambda b, h: (b, 0, h)),
            pl.BlockSpec((1, w), lambda b, h: (0, h)),
            pl.BlockSpec((1, HGRN_DK), lambda b, h: (0, 0)),
        ],
        out_specs=pl.BlockSpec((None, S, w), lambda b, h: (b, 0, h)),
        out_shape=jax.ShapeDtypeStruct((B, S, D_MODEL), BF16),
        scratch_shapes=[
            pltpu.VMEM((heads, HGRN_DK, HGRN_DK), F32), pltpu.VMEM((S, w), F32),
            pltpu.VMEM((6, HGRN_CHUNK, w), BF16), pltpu.VMEM((6, HGRN_CHUNK, w), BF16),
            pltpu.VMEM((SUBLANES, w), F32), pltpu.VMEM((SUBLANES, w), F32),
        ],
        compiler_params=_params(("parallel", "parallel")),
        name="hgrn2",
    )(qig, qig, qig, flog, lb, gn)


SEG_ALIGN = SUBLANES
SEG_BITS = tuple(range(9, 2, -1))


def _local_rows(tm):
    return TOP_K * tm + N_EXPERTS * SEG_ALIGN


def _outproj_router_body(x_ref, o_ref, w_ref, gain_ref, r_ref, x3_ref, xl_ref, pos_ref, gate_ref, seg_ref):
    tm = x_ref.shape[0]
    x3 = x_ref[...] + _dot(o_ref[...], w_ref[...])
    x3_ref[...] = x3
    h = _rmsnorm(x3, gain_ref[...])
    h_bf = h.astype(BF16)
    logits = _dot(h_bf, r_ref[...])
    lane = lax.broadcasted_iota(jnp.int32, logits.shape, 1)
    lane_f = lane.astype(F32)
    lg = jnp.where(lane < N_EXPERTS, logits, -jnp.inf)
    m1 = jnp.max(lg, axis=-1, keepdims=True)
    i1 = jnp.min(jnp.where(lg == m1, lane_f, float(LANES)), axis=-1, keepdims=True)
    lg2 = jnp.where(lane_f == i1, -jnp.inf, lg)
    m2 = jnp.max(lg2, axis=-1, keepdims=True)
    i2 = jnp.min(jnp.where(lg2 == m2, lane_f, float(LANES)), axis=-1, keepdims=True)
    e2 = jnp.exp(m2 - m1)
    den = 1.0 + e2
    gate_ref[...] = jnp.where(lane == 0, 1.0 / den, jnp.where(lane == 1, e2 / den, 0.0))

    oh1 = (lane_f == i1).astype(F32)
    oh2 = (lane_f == i2).astype(F32)
    c1 = _cumsum_groups(oh1)
    c2 = _cumsum_groups(oh2)
    n1 = c1[tm - 1:tm, :]
    count = n1 + c2[tm - 1:tm, :]
    padded = jnp.floor((count + (SEG_ALIGN - 1.0)) * (1.0 / SEG_ALIGN)) * SEG_ALIGN
    run = jnp.broadcast_to(padded, (SUBLANES, LANES))
    lane8 = lax.broadcasted_iota(jnp.int32, (SUBLANES, LANES), 1)
    for s in (1, 2, 4):
        run = run + jnp.where(lane8 >= s, pltpu.roll(run, s, 1), 0.0)
    start = run[0:1, :] - padded
    pos1 = jnp.sum(oh1 * (start + c1 - 1.0), axis=-1, keepdims=True)
    pos2 = jnp.sum(oh2 * (start + n1 + c2 - 1.0), axis=-1, keepdims=True)
    pos_ref[...] = jnp.where(lane == 0, pos1, jnp.where(lane == 1, pos2, 0.0)).astype(jnp.int32)
    pos1_row = jnp.broadcast_to(pos1, (tm, LANES)).T[0:1, :]
    pos2_row = jnp.broadcast_to(pos2, (tm, LANES)).T[0:1, :]
    slot = lax.broadcasted_iota(jnp.int32, (xl_ref.shape[0], tm), 0).astype(F32)
    perm = ((slot == pos1_row) | (slot == pos2_row)).astype(BF16)
    xl_ref[...] = _dot(perm, h_bf)
    r8 = lax.broadcasted_iota(jnp.int32, (SUBLANES, LANES), 0)
    seg_ref[...] = jnp.where(r8 == 0, count, jnp.where(r8 == 1, padded, jnp.where(r8 == 2, start, 0.0))).astype(jnp.int32)


def _outproj_router(x2d, o, w, gain, r, tm):
    T = x2d.shape[0]
    n_tt = T // tm
    lr = _local_rows(tm)
    return pl.pallas_call(
        _outproj_router_body,
        grid=(n_tt,),
        in_specs=[
            pl.BlockSpec((tm, D_MODEL), lambda i: (i, 0)),
            pl.BlockSpec((tm, D_MODEL), lambda i: (i, 0)),
            pl.BlockSpec((D_MODEL, D_MODEL), lambda i: (0, 0)),
            pl.BlockSpec((1, D_MODEL), lambda i: (0, 0)),
            pl.BlockSpec((D_MODEL, LANES), lambda i: (0, 0)),
        ],
        out_specs=[
            pl.BlockSpec((tm, D_MODEL), lambda i: (i, 0)),
            pl.BlockSpec((lr, D_MODEL), lambda i: (i, 0)),
            pl.BlockSpec((tm, LANES), lambda i: (i, 0)),
            pl.BlockSpec((tm, LANES), lambda i: (i, 0)),
            pl.BlockSpec((SUBLANES, LANES), lambda i: (i, 0)),
        ],
        out_shape=[
            jax.ShapeDtypeStruct((T, D_MODEL), F32),
            jax.ShapeDtypeStruct((n_tt * lr, D_MODEL), F32),
            jax.ShapeDtypeStruct((T, LANES), jnp.int32),
            jax.ShapeDtypeStruct((T, LANES), F32),
            jax.ShapeDtypeStruct((n_tt * SUBLANES, LANES), jnp.int32),
        ],
        compiler_params=_params(("parallel",)),
        name="outproj_router",
    )(x2d, o, w, gain, r)


def _run_copies(n, src, s0, dst, d0, sem):
    out = []
    for b in SEG_BITS:
        offs = (n >> (b + 1)) << (b + 1)
        cp = pltpu.make_async_copy(src.at[pl.ds(pl.multiple_of(s0 + offs, SEG_ALIGN), 1 << b), :],
                                   dst.at[pl.ds(pl.multiple_of(d0 + offs, SEG_ALIGN), 1 << b), :], sem)
        out.append((((n >> b) & 1) == 1, cp))
    return out


def _experts_body(te_ref, nu_ref, r0_ref, jlo_ref, jhi_ref, valid_ref, cs_ref, lp_ref, src_ref,
                  xl_hbm, wg_ref, wu_ref, wd_ref, out_ref, xbuf, xb_scr, a_scr, sem, *, n_tt):
    i = pl.program_id(0)
    f = pl.program_id(1)
    tm = out_ref.shape[0]

    def move(tile, slot, wait):
        base = te_ref[tile] * n_tt
        r0 = r0_ref[tile]

        def one_run(j, carry):
            c0 = cs_ref[base + j]
            lo = jnp.maximum(c0, r0)
            hi = jnp.minimum(c0 + lp_ref[base + j], r0 + tm)
            n = jnp.maximum(hi - lo, 0)
            for cond, cp in _run_copies(n, xl_hbm, src_ref[base + j] + (lo - c0), xbuf.at[slot], lo - r0, sem.at[slot]):
                @pl.when(cond)
                def _():
                    cp.wait() if wait else cp.start()
            return carry

        lax.fori_loop(jlo_ref[tile], jhi_ref[tile], one_run, 0)

    @pl.when(f == 0)
    def _():
        @pl.when(i == 0)
        def _():
            xbuf[...] = jnp.zeros_like(xbuf)
            move(0, 0, False)

        @pl.when(i < nu_ref[0])
        def _():
            move(i, i % 2, True)

        @pl.when(i + 1 < nu_ref[0])
        def _():
            move(i + 1, (i + 1) % 2, False)

        row = lax.broadcasted_iota(jnp.int32, (tm, D_MODEL), 0)
        xb_scr[...] = jnp.where(row < valid_ref[i], xbuf[i % 2], 0.0).astype(BF16)
        out_ref[...] = jnp.zeros_like(out_ref)

    @pl.when(i < nu_ref[0])
    def _():
        xb = xb_scr[...]
        for c in range(a_scr.shape[1] // 256):
            cols = slice(c * 256, (c + 1) * 256)
            g = _dot(xb, wg_ref[:, cols])
            u = _dot(xb, wu_ref[:, cols])
            a_scr[:, cols] = (g * _sigmoid(g) * u).astype(BF16)
        out_ref[...] += _dot(a_scr[...], wd_ref[...])


def _moe_experts(tabs, xl, wg, wu, wd, n_rows, n_tt, tm, tf):
    F = wg.shape[2]
    nf = F // tf

    def f_eff(i, f, nu):
        return jnp.where(i < nu[0], f, nf - 1)

    return pl.pallas_call(
        functools.partial(_experts_body, n_tt=n_tt),
        grid_spec=pltpu.PrefetchScalarGridSpec(
            num_scalar_prefetch=len(tabs),
            grid=(n_rows // tm, nf),
            in_specs=[
                pl.BlockSpec(memory_space=pl.ANY),
                pl.BlockSpec((None, D_MODEL, tf), lambda i, f, te, nu, *_: (te[i], 0, f_eff(i, f, nu))),
                pl.BlockSpec((None, D_MODEL, tf), lambda i, f, te, nu, *_: (te[i], 0, f_eff(i, f, nu))),
                pl.BlockSpec((None, tf, D_MODEL), lambda i, f, te, nu, *_: (te[i], f_eff(i, f, nu), 0)),
            ],
            out_specs=pl.BlockSpec((tm, D_MODEL), lambda i, f, *_: (i, 0)),
            scratch_shapes=[pltpu.VMEM((2, tm, D_MODEL), F32), pltpu.VMEM((tm, D_MODEL), BF16),
                            pltpu.VMEM((tm, tf), BF16), pltpu.SemaphoreType.DMA((2,))],
        ),
        out_shape=jax.ShapeDtypeStruct((n_rows, D_MODEL), F32),
        compiler_params=_params(("arbitrary", "arbitrary")),
        name="moe_experts",
    )(*tabs, xl, wg, wu, wd)


def _combine_body(row_ref, lp_ref, off_ref, x_ref, pos_ref, gate_ref, ys_hbm, gain_ref, out_ref, yl, sem):
    j = pl.program_id(0)
    tm = x_ref.shape[0]
    lr = yl.shape[1]

    def move(tile, slot, wait):
        for e in range(N_EXPERTS):
            t = tile * N_EXPERTS + e
            for cond, cp in _run_copies(lp_ref[t], ys_hbm, row_ref[t], yl.at[slot], off_ref[t], sem.at[slot]):
                @pl.when(cond)
                def _():
                    cp.wait() if wait else cp.start()

    @pl.when(j == 0)
    def _():
        yl[...] = jnp.zeros_like(yl)
        move(0, 0, False)

    move(j, j % 2, True)

    @pl.when(j + 1 < pl.num_programs(0))
    def _():
        move(j + 1, (j + 1) % 2, False)

    last = j * N_EXPERTS + N_EXPERTS - 1
    used = off_ref[last] + lp_ref[last]
    row = lax.broadcasted_iota(jnp.int32, (lr, D_MODEL), 0)
    y_sorted = jnp.where(row < used, yl[j % 2], 0.0).astype(BF16)
    pos = pos_ref[...]
    gates = gate_ref[...]
    slot = lax.broadcasted_iota(jnp.int32, (tm, lr), 1)
    weights = jnp.where(slot == pos[:, 0:1], gates[:, 0:1], jnp.where(slot == pos[:, 1:2], gates[:, 1:2], 0.0))
    y = x_ref[...] + _dot(weights.astype(BF16), y_sorted)
    out_ref[...] = _rmsnorm(y, gain_ref[...])


def _moe_combine(tabs, x3, pos, gates, ys, gain, tm):
    T = x3.shape[0]
    return pl.pallas_call(
        _combine_body,
        grid_spec=pltpu.PrefetchScalarGridSpec(
            num_scalar_prefetch=len(tabs),
            grid=(T // tm,),
            in_specs=[
                pl.BlockSpec((tm, D_MODEL), lambda i, *_: (i, 0)),
                pl.BlockSpec((tm, LANES), lambda i, *_: (i, 0)),
                pl.BlockSpec((tm, LANES), lambda i, *_: (i, 0)),
                pl.BlockSpec(memory_space=pl.ANY),
                pl.BlockSpec((1, D_MODEL), lambda i, *_: (0, 0)),
            ],
            out_specs=pl.BlockSpec((tm, D_MODEL), lambda i, *_: (i, 0)),
            scratch_shapes=[pltpu.VMEM((2, _local_rows(tm), D_MODEL), F32), pltpu.SemaphoreType.DMA((2,))],
        ),
        out_shape=jax.ShapeDtypeStruct((T, D_MODEL), F32),
        compiler_params=_params(("arbitrary",)),
        name="moe_combine",
    )(*tabs, x3, pos, gates, ys, gain)


def _routing_tables(seg, n_tt, tm, tm_e):
    seg = seg.reshape(n_tt, SUBLANES, LANES)
    lp = seg[:, 1, :N_EXPERTS]
    off = seg[:, 2, :N_EXPERTS]
    cs = jnp.cumsum(lp, axis=0) - lp
    total = jnp.sum(lp, axis=0)
    padded = ((total + tm_e - 1) // tm_e) * tm_e
    ends = jnp.cumsum(padded)
    starts = ends - padded
    n_rows = -(-(TOP_K * n_tt * tm + N_EXPERTS * (SEG_ALIGN - 1) * n_tt) // tm_e) * tm_e + N_EXPERTS * tm_e
    tile_start = jnp.arange(n_rows // tm_e, dtype=jnp.int32) * tm_e
    te = jnp.minimum(jnp.sum((tile_start[:, None] >= ends[None, :]).astype(jnp.int32), axis=1), N_EXPERTS - 1)
    n_used = (ends[-1] // tm_e).astype(jnp.int32).reshape(1)
    r0 = tile_start - starts[te]
    cs_t = cs[:, te]
    run_end_t = cs_t + lp[:, te]
    jlo = jnp.sum((run_end_t <= r0[None, :]).astype(jnp.int32), axis=0)
    jhi = jnp.sum((cs_t < (r0 + tm_e)[None, :]).astype(jnp.int32), axis=0)
    valid = jnp.clip(total[te] - r0, 0, tm_e)
    src = jnp.arange(n_tt, dtype=jnp.int32)[:, None] * _local_rows(tm) + off
    i32 = lambda a: a.astype(jnp.int32)
    expert_tabs = (i32(te), n_used, i32(r0), i32(jlo), i32(jhi), i32(valid),
                   i32(cs.T.reshape(-1)), i32(lp.T.reshape(-1)), i32(src.T.reshape(-1)))
    combine_tabs = (i32((starts[None, :] + cs).reshape(-1)), i32(lp.reshape(-1)), i32(off.reshape(-1)))
    return expert_tabs, combine_tabs, n_rows


def _rope_tables(positions):
    B, S = positions.shape
    inv_freq = jnp.power(jnp.float32(ROPE_THETA), -jnp.arange(ROPE_HALF, dtype=F32) / ROPE_HALF)
    ang = positions.astype(F32)[..., None] * inv_freq
    cos, sin = jnp.cos(ang), jnp.sin(ang)
    rest = HEAD_DIM - ROPE_DIM
    cos_h = jnp.concatenate([cos, cos, jnp.ones((B, S, rest), F32)], axis=-1)
    sin_h = jnp.concatenate([-sin, sin, jnp.zeros((B, S, rest), F32)], axis=-1)
    reps = LANES // HEAD_DIM
    return (jnp.tile(cos_h, (1, 1, reps)).reshape(B * S, LANES),
            jnp.tile(sin_h, (1, 1, reps)).reshape(B * S, LANES))


def kernel(x, positions, norm_mix, norm_ffn, w_in_ab, fgate_bias, w_out_ab, w_in_c, lower_bounds, gnorm_c, w_out_c,
           w_gate_ffn, w_up_ffn, w_down_ffn, router, w_gate_moe, w_up_moe, w_down_moe, norm_final):
    B, S, D = x.shape
    T = B * S
    assert D == D_MODEL and S % 1024 == 0
    tm = 512
    x2d = x.reshape(T, D)

    w_ab = jnp.pad(w_in_ab[0], ((0, 0), (0, QKV_WIDTH + LANES - w_in_ab.shape[2]))).astype(BF16)
    cos, sin = _rope_tables(positions)
    bias = jnp.pad(fgate_bias[0], (0, LANES - N_HEADS_B)).reshape(1, LANES)
    qkv, cum = _inproj_ab(x2d, norm_mix[0:1], w_ab, cos, sin, bias, tm, S)
    qkv = qkv.reshape(B, S, QKV_WIDTH)
    out_a = _dilated_attention(qkv, B, S).reshape(T, WIDTH_A)
    out_b = _fox_attention(qkv, cum.reshape(B, S, LANES), B, S).reshape(T, WIDTH_B)
    w_o = w_out_ab[0].astype(BF16)
    x2 = _outproj_ffn(x2d, out_a, out_b, w_o[:WIDTH_A], w_o[WIDTH_A:], norm_ffn[0:1],
                      w_gate_ffn[0].astype(BF16), w_up_ffn[0].astype(BF16), w_down_ffn[0].astype(BF16), tm)

    lb_all = jnp.cumsum(jax.nn.softmax(lower_bounds.astype(F32), axis=0), axis=0)
    lb = (lb_all - lb_all[0:1])[1].reshape(1, D)
    wq, wf, wi, wg = jnp.split(w_in_c[0], 4, axis=-1)
    w_c = jnp.concatenate([wq, wi, wg, wf], axis=-1).astype(BF16)
    qig, flog_c = _inproj_c(x2, norm_mix[1:2], w_c, tm)
    o_c = _hgrn(qig.reshape(B, S, 3 * D), flog_c.reshape(B, S, D), lb, gnorm_c[0:1], B, S).reshape(T, D)

    r_pad = jnp.pad(router[0], ((0, 0), (0, LANES - N_EXPERTS))).astype(BF16)
    x3, xl, pos, gates, seg = _outproj_router(x2, o_c, w_out_c[0].astype(BF16), norm_ffn[1:2], r_pad, tm)
    tm_e = 1024
    expert_tabs, combine_tabs, n_rows = _routing_tables(seg, T // tm, tm, tm_e)
    ys = _moe_experts(expert_tabs, xl, w_gate_moe[0].astype(BF16), w_up_moe[0].astype(BF16),
                      w_down_moe[0].astype(BF16), n_rows, T // tm, tm_e, w_gate_moe.shape[3] // 2)
    out = _moe_combine(combine_tabs, x3, pos, gates, ys, norm_final.reshape(1, D), tm)
    return out.reshape(B, S, D)
```

```python
import functools

import jax
import jax.numpy as jnp
import numpy as np
from jax import lax
from jax.experimental import pallas as pl
from jax.experimental.pallas import tpu as pltpu

F32 = jnp.float32
BF16 = jnp.bfloat16

D_MODEL = 1024
HEAD_DIM = 64
N_HEADS_A = 8
N_HEADS_B = 8
WIDTH_A = N_HEADS_A * HEAD_DIM
WIDTH_B = N_HEADS_B * HEAD_DIM
QKV_WIDTH = 3 * (WIDTH_A + WIDTH_B)
ROPE_THETA = 500000.0
ROPE_DIM = HEAD_DIM // 4
ROPE_HALF = ROPE_DIM // 2
ATT_BLOCK = 128
DILATIONS = (1, 4, 16)
N_HEADS_C = 8
HGRN_DK = 128
HGRN_CHUNK = 64
HGRN_LEAF = 16
N_EXPERTS = 8
TOP_K = 2
EPS = 1e-6
LOG2E = 1.4426950408889634

LANES = 128
SUBLANES = 8
VMEM_LIMIT = 56 * 1024 * 1024

NT_DIMS = (((1,), (1,)), ((), ()))


def _params(semantics, **kw):
    return pltpu.CompilerParams(dimension_semantics=semantics, vmem_limit_bytes=VMEM_LIMIT, **kw)


def _rmsnorm(x, gain):
    return x * lax.rsqrt(jnp.mean(x * x, axis=-1, keepdims=True) + EPS) * gain


def _sigmoid(x):
    return 1.0 / (1.0 + jnp.exp(-x))


def _split3(x):
    hi = x.astype(BF16)
    r1 = x - hi.astype(F32)
    mid = r1.astype(BF16)
    lo = (r1 - mid.astype(F32)).astype(BF16)
    return hi, mid, lo


def _dot(a, b):
    return jnp.dot(a, b, preferred_element_type=F32)


def _dot_nt(a, b):
    return lax.dot_general(a, b, NT_DIMS, preferred_element_type=F32)


def _cumsum_groups(x):
    n, w = x.shape
    rows = lax.broadcasted_iota(jnp.int32, (SUBLANES, w), 0)
    out, carry = [], None
    for g in range(n // SUBLANES):
        xg = x[SUBLANES * g:SUBLANES * (g + 1), :]
        for s in (1, 2, 4):
            xg = xg + jnp.where(rows >= s, pltpu.roll(xg, s, 0), 0.0)
        if carry is not None:
            xg = xg + carry
        carry = xg[SUBLANES - 1:SUBLANES, :]
        out.append(xg)
    return jnp.concatenate(out, axis=0)


def _inproj_ab_body(x_ref, gain_ref, w_ref, cos_ref, sin_ref, bias_ref, qkv_ref, cum_ref, carry_scr, *, tiles_per_seq):
    h = _rmsnorm(x_ref[...], gain_ref[...]).astype(BF16)

    @pl.when(pl.program_id(0) % tiles_per_seq == 0)
    def _():
        carry_scr[...] = jnp.zeros_like(carry_scr)

    x = _dot(h, w_ref[:, QKV_WIDTH:QKV_WIDTH + LANES]) + bias_ref[...]
    logf = -(jnp.maximum(-x, 0.0) + jnp.log1p(jnp.exp(-jnp.abs(x))))
    cum = _cumsum_groups(logf * LOG2E) + carry_scr[0:1, :]
    cum_ref[...] = cum
    carry_scr[0:1, :] = cum[cum.shape[0] - 1:, :]

    cos = cos_ref[...]
    sin = sin_ref[...]
    lane = lax.broadcasted_iota(jnp.int32, cos.shape, 1)
    low = (lane & (HEAD_DIM - 1)) < ROPE_HALF
    scale = HEAD_DIM ** -0.5 * LOG2E
    for c in range(QKV_WIDTH // 256):
        y = _dot(h, w_ref[:, c * 256:(c + 1) * 256])
        seg = c // 2
        for s in range(2):
            yy = y[:, s * LANES:(s + 1) * LANES]
            if seg in (0, 1):
                partner = jnp.where(low, pltpu.roll(yy, LANES - ROPE_HALF, 1), pltpu.roll(yy, ROPE_HALF, 1))
                yy = yy * cos + partner * sin
            if seg in (0, 3):
                yy = yy * scale
            qkv_ref[:, c * 256 + s * LANES:c * 256 + (s + 1) * LANES] = yy.astype(BF16)


def _inproj_ab(x2d, gain, w, cos, sin, bias, tm, seq_len):
    T = x2d.shape[0]
    wn = w.shape[1]
    return pl.pallas_call(
        functools.partial(_inproj_ab_body, tiles_per_seq=seq_len // tm),
        grid=(T // tm,),
        in_specs=[
            pl.BlockSpec((tm, D_MODEL), lambda i: (i, 0)),
            pl.BlockSpec((1, D_MODEL), lambda i: (0, 0)),
            pl.BlockSpec((D_MODEL, wn), lambda i: (0, 0)),
            pl.BlockSpec((tm, LANES), lambda i: (i, 0)),
            pl.BlockSpec((tm, LANES), lambda i: (i, 0)),
            pl.BlockSpec((1, LANES), lambda i: (0, 0)),
        ],
        out_specs=[
            pl.BlockSpec((tm, QKV_WIDTH), lambda i: (i, 0)),
            pl.BlockSpec((tm, LANES), lambda i: (i, 0)),
        ],
        out_shape=[
            jax.ShapeDtypeStruct((T, QKV_WIDTH), BF16),
            jax.ShapeDtypeStruct((T, LANES), F32),
        ],
        scratch_shapes=[pltpu.VMEM((SUBLANES, LANES), F32)],
        compiler_params=_params(("arbitrary",)),
        name="inproj_ab",
    )(x2d, gain, w, cos, sin, bias)


def _dilated_body(q_ref, k_ref, v_ref, o_ref, qf, kf, vf, ob, lb):
    S = q_ref.shape[0]
    nb = ATT_BLOCK
    qf[...] = q_ref[...].astype(F32)
    kf[...] = k_ref[...].astype(F32)
    vf[...] = v_ref[...].astype(F32)
    head0 = lax.broadcasted_iota(jnp.int32, (nb, LANES), 1) < HEAD_DIM
    qi2 = lax.broadcasted_iota(jnp.int32, (2 * nb, 2 * nb), 0) & (nb - 1)
    kj2 = lax.broadcasted_iota(jnp.int32, (2 * nb, 2 * nb), 1)
    valid2 = (kj2 >= qi2) & (kj2 <= qi2 + nb)
    qi1 = lax.broadcasted_iota(jnp.int32, (2 * nb, nb), 0) & (nb - 1)
    kj1 = lax.broadcasted_iota(jnp.int32, (2 * nb, nb), 1)
    valid1 = kj1 <= qi1

    def rows(start, size, r):
        return pl.ds(start, size) if r == 1 else pl.ds(start, size, stride=r)

    def block(br, r, q0, k0, nk):
        qs = qf[rows(q0, nb, r), :]
        kb = kf[rows(k0, nk, r), :].astype(BF16)
        vb = vf[rows(k0, nk, r), :].astype(BF16)
        q2 = jnp.concatenate([jnp.where(head0, qs, 0.0), jnp.where(head0, 0.0, qs)], axis=0).astype(BF16)
        s = jnp.where(valid2 if nk == 2 * nb else valid1, _dot_nt(q2, kb), -jnp.inf)
        m = jnp.max(s, axis=-1, keepdims=True)
        e = jnp.exp2(s - m)
        l = jnp.sum(e, axis=-1, keepdims=True)
        o = _dot(e.astype(BF16), vb) / l
        lse = jnp.broadcast_to(m + jnp.log2(l), (2 * nb, LANES))
        ob[br, rows(q0, nb, r), :] = jnp.where(head0, o[:nb], o[nb:])
        lb[br, rows(q0, nb, r), :] = jnp.where(head0, lse[:nb], lse[nb:])

    for br, r in enumerate(DILATIONS):
        n_blocks = S // (r * nb)
        if n_blocks == 1:
            def first_only(c, carry, br=br, r=r):
                block(br, r, c, c, nb)
                return carry
            lax.fori_loop(0, r, first_only, 0, unroll=True)
        else:
            for c in range(r):
                block(br, r, c, c, nb)

                def later(n, carry, br=br, r=r, c=c):
                    block(br, r, n * (nb * r) + c, (n - 1) * (nb * r) + c, 2 * nb)
                    return carry
                lax.fori_loop(1, n_blocks, later, 0, unroll=True)

    rows_per = 256
    for ch in range(S // rows_per):
        sl = pl.ds(ch * rows_per, rows_per)
        l0, l1, l2 = lb[0, sl, :], lb[1, sl, :], lb[2, sl, :]
        m = jnp.maximum(jnp.maximum(l0, l1), l2)
        w0, w1, w2 = jnp.exp2(l0 - m), jnp.exp2(l1 - m), jnp.exp2(l2 - m)
        o = (w0 * ob[0, sl, :] + w1 * ob[1, sl, :] + w2 * ob[2, sl, :]) / (w0 + w1 + w2)
        o_ref[sl, :] = o.astype(BF16)


def _dilated_attention(qkv, B, S):
    n_pairs = WIDTH_A // LANES
    blk = lambda off: pl.BlockSpec((None, S, LANES), lambda b, p: (b, 0, off + p))
    return pl.pallas_call(
        _dilated_body,
        grid=(B, n_pairs),
        in_specs=[blk(0), blk(n_pairs), blk(2 * n_pairs)],
        out_specs=pl.BlockSpec((None, S, LANES), lambda b, p: (b, 0, p)),
        out_shape=jax.ShapeDtypeStruct((B, S, WIDTH_A), BF16),
        scratch_shapes=[
            pltpu.VMEM((S, LANES), F32), pltpu.VMEM((S, LANES), F32), pltpu.VMEM((S, LANES), F32),
            pltpu.VMEM((len(DILATIONS), S, LANES), F32), pltpu.VMEM((len(DILATIONS), S, LANES), F32),
        ],
        compiler_params=_params(("parallel", "parallel")),
        name="dilated_attention",
    )(qkv, qkv, qkv)


def _fox_body(q_ref, k_ref, v_ref, c_ref, sel_ref, o_ref, qa_scr, ka_scr, *, tk):
    S = q_ref.shape[0]
    rows_per = 256
    lane = lax.broadcasted_iota(jnp.int32, (rows_per, LANES), 1)

    def build(i, carry):
        sl = pl.ds(i * rows_per, rows_per)
        q = q_ref[sl, :].astype(F32)
        k = k_ref[sl, :].astype(F32)
        extra = _dot(jnp.concatenate(_split3(c_ref[sl, :]), axis=1), sel_ref[...])
        for hh in range(2):
            own = (lane < HEAD_DIM) if hh == 0 else (lane >= HEAD_DIM)
            a0 = HEAD_DIM if hh == 0 else 0
            first = (lane >= a0) & (lane < a0 + 3)
            second = (lane >= a0 + 3) & (lane < a0 + 6)
            ex = extra[:, hh * LANES:(hh + 1) * LANES]
            qaug = jnp.where(own, q, jnp.where(first, ex, jnp.where(second, 1.0, 0.0)))
            kaug = jnp.where(own, k, jnp.where(second, ex, jnp.where(first, 1.0, 0.0)))
            qa_scr[hh, sl, :] = qaug.astype(BF16)
            ka_scr[hh, sl, :] = kaug.astype(BF16)
        return carry

    lax.fori_loop(0, S // rows_per, build, 0, unroll=True)

    n_blocks = S // tk
    row_t = lax.broadcasted_iota(jnp.int32, (tk, tk), 0)
    col_t = lax.broadcasted_iota(jnp.int32, (tk, tk), 1)
    causal = col_t <= row_t
    head0 = lax.broadcasted_iota(jnp.int32, (tk, LANES), 1) < HEAD_DIM
    state = [[None] * n_blocks for _ in range(2)]
    for j in range(n_blocks):
        ksl = pl.ds(j * tk, tk)
        vblk = v_ref[ksl, :]
        for hh in range(2):
            s_all = _dot_nt(qa_scr[hh, pl.ds(j * tk, S - j * tk), :], ka_scr[hh, ksl, :])
            es, scales = [], []
            for rb in range(j, n_blocks):
                s = s_all[(rb - j) * tk:(rb - j + 1) * tk, :]
                if rb == j:
                    s = jnp.where(causal, s, -jnp.inf)
                if j == 0:
                    m_new = jnp.max(s, axis=-1, keepdims=True)
                    alpha = None
                else:
                    m_old = state[hh][rb][0]
                    m_new = jnp.maximum(m_old, jnp.max(s, axis=-1, keepdims=True))
                    alpha = jnp.exp2(m_old - m_new)
                e = jnp.exp2(s - m_new)
                es.append(e.astype(BF16))
                scales.append((m_new, alpha, jnp.sum(e, axis=-1, keepdims=True)))
            pv_all = _dot(jnp.concatenate(es, axis=0) if len(es) > 1 else es[0], vblk)
            for rb in range(j, n_blocks):
                m_new, alpha, rowsum = scales[rb - j]
                pv = pv_all[(rb - j) * tk:(rb - j + 1) * tk, :]
                if alpha is None:
                    state[hh][rb] = (m_new, rowsum, pv)
                else:
                    _, l_old, acc_old = state[hh][rb]
                    state[hh][rb] = (m_new, alpha * l_old + rowsum, alpha * acc_old + pv)
        outs = [state[hh][j][2] / state[hh][j][1] for hh in range(2)]
        o_ref[pl.ds(j * tk, tk), :] = jnp.where(head0, outs[0], outs[1]).astype(BF16)


def _fox_selection(n_pairs):
    sel = np.zeros((n_pairs, 3 * LANES, 2 * LANES), np.float32)
    for p in range(n_pairs):
        for hh in range(2):
            a0 = hh * LANES + (HEAD_DIM if hh == 0 else 0)
            for piece in range(3):
                sel[p, piece * LANES + 2 * p + hh, a0 + piece] = 1.0
                sel[p, piece * LANES + 2 * p + hh, a0 + 3 + piece] = -1.0
    return jnp.asarray(sel, BF16)


def _fox_attention(qkv, cum, B, S, tk=512):
    n_pairs = WIDTH_B // LANES
    base = 3 * (WIDTH_A // LANES)
    blk = lambda off: pl.BlockSpec((None, S, LANES), lambda b, p: (b, 0, off + p))
    return pl.pallas_call(
        functools.partial(_fox_body, tk=tk),
        grid=(B, n_pairs),
        in_specs=[
            blk(base), blk(base + n_pairs), blk(base + 2 * n_pairs),
            pl.BlockSpec((None, S, LANES), lambda b, p: (b, 0, 0)),
            pl.BlockSpec((None, 3 * LANES, 2 * LANES), lambda b, p: (p, 0, 0)),
        ],
        out_specs=pl.BlockSpec((None, S, LANES), lambda b, p: (b, 0, p)),
        out_shape=jax.ShapeDtypeStruct((B, S, WIDTH_B), BF16),
        scratch_shapes=[pltpu.VMEM((2, S, LANES), BF16), pltpu.VMEM((2, S, LANES), BF16)],
        compiler_params=_params(("parallel", "parallel")),
        name="fox_attention",
    )(qkv, qkv, qkv, cum, _fox_selection(n_pairs))


def _outproj_ffn_body(x_ref, oa_ref, ob_ref, woa_ref, wob_ref, gain_ref, wg_ref, wu_ref, wd_ref, out_ref, a_scr):
    x1 = x_ref[...] + _dot(oa_ref[...], woa_ref[...]) + _dot(ob_ref[...], wob_ref[...])
    h = _rmsnorm(x1, gain_ref[...]).astype(BF16)
    for c in range(a_scr.shape[1] // 256):
        cols = slice(c * 256, (c + 1) * 256)
        g = _dot(h, wg_ref[:, cols])
        u = _dot(h, wu_ref[:, cols])
        a_scr[:, cols] = (g * _sigmoid(g) * u).astype(BF16)
    out_ref[...] = x1 + _dot(a_scr[...], wd_ref[...])


def _outproj_ffn(x2d, oa, ob, woa, wob, gain, wg, wu, wd, tm):
    T = x2d.shape[0]
    F = wg.shape[1]
    once = dict(pipeline_mode=pl.Buffered(1))
    return pl.pallas_call(
        _outproj_ffn_body,
        grid=(T // tm,),
        in_specs=[
            pl.BlockSpec((tm, D_MODEL), lambda i: (i, 0)),
            pl.BlockSpec((tm, WIDTH_A), lambda i: (i, 0)),
            pl.BlockSpec((tm, WIDTH_B), lambda i: (i, 0)),
            pl.BlockSpec((WIDTH_A, D_MODEL), lambda i: (0, 0), **once),
            pl.BlockSpec((WIDTH_B, D_MODEL), lambda i: (0, 0), **once),
            pl.BlockSpec((1, D_MODEL), lambda i: (0, 0)),
            pl.BlockSpec((D_MODEL, F), lambda i: (0, 0), **once),
            pl.BlockSpec((D_MODEL, F), lambda i: (0, 0), **once),
            pl.BlockSpec((F, D_MODEL), lambda i: (0, 0), **once),
        ],
        out_specs=pl.BlockSpec((tm, D_MODEL), lambda i: (i, 0)),
        out_shape=jax.ShapeDtypeStruct((T, D_MODEL), F32),
        scratch_shapes=[pltpu.VMEM((tm, F), BF16)],
        compiler_params=_params(("parallel",)),
        name="outproj_ffn",
    )(x2d, oa, ob, woa, wob, gain, wg, wu, wd)


def _inproj_c_body(x_ref, gain_ref, w_ref, qig_ref, f_ref):
    h = _rmsnorm(x_ref[...], gain_ref[...]).astype(BF16)
    n_qig = qig_ref.shape[1]
    for c in range(n_qig // 256):
        qig_ref[:, c * 256:(c + 1) * 256] = _dot(h, w_ref[:, c * 256:(c + 1) * 256]).astype(BF16)
    for c in range(f_ref.shape[1] // 256):
        f_ref[:, c * 256:(c + 1) * 256] = _dot(h, w_ref[:, n_qig + c * 256:n_qig + (c + 1) * 256])


def _inproj_c(x2d, gain, w, tm):
    T = x2d.shape[0]
    return pl.pallas_call(
        _inproj_c_body,
        grid=(T // tm,),
        in_specs=[
            pl.BlockSpec((tm, D_MODEL), lambda i: (i, 0)),
            pl.BlockSpec((1, D_MODEL), lambda i: (0, 0)),
            pl.BlockSpec((D_MODEL, 4 * D_MODEL), lambda i: (0, 0)),
        ],
        out_specs=[
            pl.BlockSpec((tm, 3 * D_MODEL), lambda i: (i, 0)),
            pl.BlockSpec((tm, D_MODEL), lambda i: (i, 0)),
        ],
        out_shape=[
            jax.ShapeDtypeStruct((T, 3 * D_MODEL), BF16),
            jax.ShapeDtypeStruct((T, D_MODEL), F32),
        ],
        compiler_params=_params(("parallel",)),
        name="inproj_c",
    )(x2d, gain, w)


def _hgrn_body(q_ref, i_ref, g_ref, f_ref, lb_ref, gn_ref, o_ref, state_scr, o_scr, ops0, ops1, dec0, dec1, *, heads):
    S = q_ref.shape[0]
    C = HGRN_CHUNK
    half = C // 2
    quarter = C // 4
    assert quarter == HGRN_LEAF
    n_chunks = S // C
    row = lax.broadcasted_iota(jnp.int32, (C, C), 0)
    col = lax.broadcasted_iota(jnp.int32, (C, C), 1)
    mask_cross = (row >= half) & (col < half)
    mask_same = ((row // half) == (col // half)) & (col <= row)
    r = lax.broadcasted_iota(jnp.int32, (C, heads * HGRN_DK), 0)
    scale = HGRN_DK ** -0.5
    gn = jnp.concatenate([gn_ref[...]] * heads, axis=1)
    state_scr[...] = jnp.zeros_like(state_scr)

    def prepare(ci, ops, dec):
        sl = pl.ds(ci * C, C)
        lbv = lb_ref[...]
        f = lbv + (1.0 - lbv) * _sigmoid(f_ref[sl, :])
        k = 1.0 - f
        q = q_ref[sl, :].astype(F32) * scale
        b = _cumsum_groups(jnp.log2(f))
        b_last = b[C - 1:C, :]
        e_cross = b - b[half - 1:half, :]
        e_same = b - jnp.where(r < half, b[quarter - 1:quarter, :], b[half + quarter - 1:half + quarter, :])
        ops[0] = (q * jnp.exp2(jnp.minimum(e_cross, 0.0))).astype(BF16)
        ops[1] = (k * jnp.exp2(jnp.minimum(-e_cross, 0.0))).astype(BF16)
        ops[2] = (q * jnp.exp2(e_same)).astype(BF16)
        ops[3] = (k * jnp.exp2(-e_same)).astype(BF16)
        ops[4] = (q * jnp.exp2(b)).astype(BF16)
        ops[5] = (k * jnp.exp2(b_last - b)).astype(BF16)
        dec[0:1, :] = jnp.exp2(b_last)

    def contract(ci, ops, dec):
        sl = pl.ds(ci * C, C)
        for hd in range(heads):
            cols = slice(hd * HGRN_DK, (hd + 1) * HGRN_DK)
            v = i_ref[sl, cols]
            scores = (jnp.where(mask_cross, _dot_nt(ops[0, :, cols], ops[1, :, cols]), 0.0)
                      + jnp.where(mask_same, _dot_nt(ops[2, :, cols], ops[3, :, cols]), 0.0))
            state_t = state_scr[hd]
            o_scr[sl, cols] = _dot(scores.astype(BF16), v) + _dot_nt(ops[4, :, cols], state_t.astype(BF16))
            v_t = v.astype(F32).T.astype(BF16)
            state_scr[hd] = state_t * dec[0:1, cols] + _dot(v_t, ops[5, :, cols])

    def finish(ci):
        sl = pl.ds(ci * C, C)
        o = o_scr[sl, :]
        ys = []
        for hd in range(heads):
            oh = o[:, hd * HGRN_DK:(hd + 1) * HGRN_DK]
            ys.append(oh * lax.rsqrt(jnp.mean(oh * oh, axis=-1, keepdims=True) + EPS))
        gate = g_ref[sl, :].astype(F32)
        o_ref[sl, :] = (jnp.concatenate(ys, axis=1) * gn * (gate * _sigmoid(gate))).astype(BF16)

    prepare(0, ops0, dec0)

    def pair(j, carry):
        prepare(2 * j + 1, ops1, dec1)
        contract(2 * j, ops0, dec0)
        prepare(2 * j + 2, ops0, dec0)
        contract(2 * j + 1, ops1, dec1)
        finish(jnp.maximum(2 * j - 1, 0))
        finish(2 * j)
        return carry

    lax.fori_loop(0, n_chunks // 2 - 1, pair, 0, unroll=5)
    prepare(n_chunks - 1, ops1, dec1)
    contract(n_chunks - 2, ops0, dec0)
    contract(n_chunks - 1, ops1, dec1)
    for ci in range(n_chunks - 3, n_chunks):
        finish(ci)


def _hgrn(qig, flog, lb, gn, B, S, heads=4):
    ng = N_HEADS_C // heads
    w = heads * HGRN_DK
    blk = lambda off: pl.BlockSpec((None, S, w), lambda b, h: (b, 0, off + h))
    return pl.pallas_call(
        functools.partial(_hgrn_body, heads=heads),
        grid=(B, ng),
        in_specs=[
            blk(0), blk(ng), blk(2 * ng),
            pl.BlockSpec((None, S, w), lambda b, h: (b, 0, h)),
            pl.BlockSpec((1, w), lambda b, h: (0, h)),
            pl.BlockSpec((1, HGRN_DK), lambda b, h: (0, 0)),
        ],
        out_specs=pl.BlockSpec((None, S, w), lambda b, h: (b, 0, h)),
        out_shape=jax.ShapeDtypeStruct((B, S, D_MODEL), BF16),
        scratch_shapes=[
            pltpu.VMEM((heads, HGRN_DK, HGRN_DK), F32), pltpu.VMEM((S, w), F32),
            pltpu.VMEM((6, HGRN_CHUNK, w), BF16), pltpu.VMEM((6, HGRN_CHUNK, w), BF16),
            pltpu.VMEM((SUBLANES, w), F32), pltpu.VMEM((SUBLANES, w), F32),
        ],
        compiler_params=_params(("parallel", "parallel")),
        name="hgrn2",
    )(qig, qig, qig, flog, lb, gn)


SEG_ALIGN = SUBLANES
SEG_BITS = tuple(range(9, 2, -1))


def _local_rows(tm):
    return TOP_K * tm + N_EXPERTS * SEG_ALIGN


def _outproj_router_body(x_ref, o_ref, w_ref, gain_ref, r_ref, x3_ref, xl_ref, pos_ref, gate_ref, seg_ref):
    tm = x_ref.shape[0]
    x3 = x_ref[...] + _dot(o_ref[...], w_ref[...])
    x3_ref[...] = x3
    h = _rmsnorm(x3, gain_ref[...])
    h_bf = h.astype(BF16)
    logits = _dot(h_bf, r_ref[...])
    lane = lax.broadcasted_iota(jnp.int32, logits.shape, 1)
    lane_f = lane.astype(F32)
    lg = jnp.where(lane < N_EXPERTS, logits, -jnp.inf)
    m1 = jnp.max(lg, axis=-1, keepdims=True)
    i1 = jnp.min(jnp.where(lg == m1, lane_f, float(LANES)), axis=-1, keepdims=True)
    lg2 = jnp.where(lane_f == i1, -jnp.inf, lg)
    m2 = jnp.max(lg2, axis=-1, keepdims=True)
    i2 = jnp.min(jnp.where(lg2 == m2, lane_f, float(LANES)), axis=-1, keepdims=True)
    e2 = jnp.exp(m2 - m1)
    den = 1.0 + e2
    gate_ref[...] = jnp.where(lane == 0, 1.0 / den, jnp.where(lane == 1, e2 / den, 0.0))

    oh1 = (lane_f == i1).astype(F32)
    oh2 = (lane_f == i2).astype(F32)
    c1 = _cumsum_groups(oh1)
    c2 = _cumsum_groups(oh2)
    n1 = c1[tm - 1:tm, :]
    count = n1 + c2[tm - 1:tm, :]
    padded = jnp.floor((count + (SEG_ALIGN - 1.0)) * (1.0 / SEG_ALIGN)) * SEG_ALIGN
    run = jnp.broadcast_to(padded, (SUBLANES, LANES))
    lane8 = lax.broadcasted_iota(jnp.int32, (SUBLANES, LANES), 1)
    for s in (1, 2, 4):
        run = run + jnp.where(lane8 >= s, pltpu.roll(run, s, 1), 0.0)
    start = run[0:1, :] - padded
    pos1 = jnp.sum(oh1 * (start + c1 - 1.0), axis=-1, keepdims=True)
    pos2 = jnp.sum(oh2 * (start + n1 + c2 - 1.0), axis=-1, keepdims=True)
    pos_ref[...] = jnp.where(lane == 0, pos1, jnp.where(lane == 1, pos2, 0.0)).astype(jnp.int32)
    pos1_row = jnp.broadcast_to(pos1, (tm, LANES)).T[0:1, :]
    pos2_row = jnp.broadcast_to(pos2, (tm, LANES)).T[0:1, :]
    slot = lax.broadcasted_iota(jnp.int32, (xl_ref.shape[0], tm), 0).astype(F32)
    perm = ((slot == pos1_row) | (slot == pos2_row)).astype(BF16)
    xl_ref[...] = _dot(perm, h_bf)
    r8 = lax.broadcasted_iota(jnp.int32, (SUBLANES, LANES), 0)
    seg_ref[...] = jnp.where(r8 == 0, count, jnp.where(r8 == 1, padded, jnp.where(r8 == 2, start, 0.0))).astype(jnp.int32)


def _outproj_router(x2d, o, w, gain, r, tm):
    T = x2d.shape[0]
    n_tt = T // tm
    lr = _local_rows(tm)
    return pl.pallas_call(
        _outproj_router_body,
        grid=(n_tt,),
        in_specs=[
            pl.BlockSpec((tm, D_MODEL), lambda i: (i, 0)),
            pl.BlockSpec((tm, D_MODEL), lambda i: (i, 0)),
            pl.BlockSpec((D_MODEL, D_MODEL), lambda i: (0, 0)),
            pl.BlockSpec((1, D_MODEL), lambda i: (0, 0)),
            pl.BlockSpec((D_MODEL, LANES), lambda i: (0, 0)),
        ],
        out_specs=[
            pl.BlockSpec((tm, D_MODEL), lambda i: (i, 0)),
            pl.BlockSpec((lr, D_MODEL), lambda i: (i, 0)),
            pl.BlockSpec((tm, LANES), lambda i: (i, 0)),
            pl.BlockSpec((tm, LANES), lambda i: (i, 0)),
            pl.BlockSpec((SUBLANES, LANES), lambda i: (i, 0)),
        ],
        out_shape=[
            jax.ShapeDtypeStruct((T, D_MODEL), F32),
            jax.ShapeDtypeStruct((n_tt * lr, D_MODEL), F32),
            jax.ShapeDtypeStruct((T, LANES), jnp.int32),
            jax.ShapeDtypeStruct((T, LANES), F32),
            jax.ShapeDtypeStruct((n_tt * SUBLANES, LANES), jnp.int32),
        ],
        compiler_params=_params(("parallel",)),
        name="outproj_router",
    )(x2d, o, w, gain, r)


def _run_copies(n, src, s0, dst, d0, sem):
    out = []
    for b in SEG_BITS:
        offs = (n >> (b + 1)) << (b + 1)
        cp = pltpu.make_async_copy(src.at[pl.ds(pl.multiple_of(s0 + offs, SEG_ALIGN), 1 << b), :],
                                   dst.at[pl.ds(pl.multiple_of(d0 + offs, SEG_ALIGN), 1 << b), :], sem)
        out.append((((n >> b) & 1) == 1, cp))
    return out


def _experts_body(te_ref, nu_ref, r0_ref, jlo_ref, jhi_ref, valid_ref, cs_ref, lp_ref, src_ref,
                  xl_hbm, wg_ref, wu_ref, wd_ref, out_ref, xbuf, xb_scr, a_scr, sem, *, n_tt):
    i = pl.program_id(0)
    f = pl.program_id(1)
    tm = out_ref.shape[0]

    def move(tile, slot, wait):
        base = te_ref[tile] * n_tt
        r0 = r0_ref[tile]

        def one_run(j, carry):
            c0 = cs_ref[base + j]
            lo = jnp.maximum(c0, r0)
            hi = jnp.minimum(c0 + lp_ref[base + j], r0 + tm)
            n = jnp.maximum(hi - lo, 0)
            for cond, cp in _run_copies(n, xl_hbm, src_ref[base + j] + (lo - c0), xbuf.at[slot], lo - r0, sem.at[slot]):
                @pl.when(cond)
                def _():
                    cp.wait() if wait else cp.start()
            return carry

        lax.fori_loop(jlo_ref[tile], jhi_ref[tile], one_run, 0)

    @pl.when(f == 0)
    def _():
        @pl.when(i == 0)
        def _():
            xbuf[...] = jnp.zeros_like(xbuf)
            move(0, 0, False)

        @pl.when(i < nu_ref[0])
        def _():
            move(i, i % 2, True)

        @pl.when(i + 1 < nu_ref[0])
        def _():
            move(i + 1, (i + 1) % 2, False)

        row = lax.broadcasted_iota(jnp.int32, (tm, D_MODEL), 0)
        xb_scr[...] = jnp.where(row < valid_ref[i], xbuf[i % 2], 0.0).astype(BF16)
        out_ref[...] = jnp.zeros_like(out_ref)

    @pl.when(i < nu_ref[0])
    def _():
        xb = xb_scr[...]
        for c in range(a_scr.shape[1] // 256):
            cols = slice(c * 256, (c + 1) * 256)
            g = _dot(xb, wg_ref[:, cols])
            u = _dot(xb, wu_ref[:, cols])
            a_scr[:, cols] = (g * _sigmoid(g) * u).astype(BF16)
        out_ref[...] += _dot(a_scr[...], wd_ref[...])


def _moe_experts(tabs, xl, wg, wu, wd, n_rows, n_tt, tm, tf):
    F = wg.shape[2]
    nf = F // tf

    def f_eff(i, f, nu):
        return jnp.where(i < nu[0], f, nf - 1)

    return pl.pallas_call(
        functools.partial(_experts_body, n_tt=n_tt),
        grid_spec=pltpu.PrefetchScalarGridSpec(
            num_scalar_prefetch=len(tabs),
            grid=(n_rows // tm, nf),
            in_specs=[
                pl.BlockSpec(memory_space=pl.ANY),
                pl.BlockSpec((None, D_MODEL, tf), lambda i, f, te, nu, *_: (te[i], 0, f_eff(i, f, nu))),
                pl.BlockSpec((None, D_MODEL, tf), lambda i, f, te, nu, *_: (te[i], 0, f_eff(i, f, nu))),
                pl.BlockSpec((None, tf, D_MODEL), lambda i, f, te, nu, *_: (te[i], f_eff(i, f, nu), 0)),
            ],
            out_specs=pl.BlockSpec((tm, D_MODEL), lambda i, f, *_: (i, 0)),
            scratch_shapes=[pltpu.VMEM((2, tm, D_MODEL), F32), pltpu.VMEM((tm, D_MODEL), BF16),
                            pltpu.VMEM((tm, tf), BF16), pltpu.SemaphoreType.DMA((2,))],
        ),
        out_shape=jax.ShapeDtypeStruct((n_rows, D_MODEL), F32),
        compiler_params=_params(("arbitrary", "arbitrary")),
        name="moe_experts",
    )(*tabs, xl, wg, wu, wd)


def _combine_body(row_ref, lp_ref, off_ref, x_ref, pos_ref, gate_ref, ys_hbm, gain_ref, out_ref, yl, sem):
    j = pl.program_id(0)
    tm = x_ref.shape[0]
    lr = yl.shape[1]

    def move(tile, slot, wait):
        for e in range(N_EXPERTS):
            t = tile * N_EXPERTS + e
            for cond, cp in _run_copies(lp_ref[t], ys_hbm, row_ref[t], yl.at[slot], off_ref[t], sem.at[slot]):
                @pl.when(cond)
                def _():
                    cp.wait() if wait else cp.start()

    @pl.when(j == 0)
    def _():
        yl[...] = jnp.zeros_like(yl)
        move(0, 0, False)

    move(j, j % 2, True)

    @pl.when(j + 1 < pl.num_programs(0))
    def _():
        move(j + 1, (j + 1) % 2, False)

    last = j * N_EXPERTS + N_EXPERTS - 1
    used = off_ref[last] + lp_ref[last]
    row = lax.broadcasted_iota(jnp.int32, (lr, D_MODEL), 0)
    y_sorted = jnp.where(row < used, yl[j % 2], 0.0).astype(BF16)
    pos = pos_ref[...]
    gates = gate_ref[...]
    slot = lax.broadcasted_iota(jnp.int32, (tm, lr), 1)
    weights = jnp.where(slot == pos[:, 0:1], gates[:, 0:1], jnp.where(slot == pos[:, 1:2], gates[:, 1:2], 0.0))
    y = x_ref[...] + _dot(weights.astype(BF16), y_sorted)
    out_ref[...] = _rmsnorm(y, gain_ref[...])


def _moe_combine(tabs, x3, pos, gates, ys, gain, tm):
    T = x3.shape[0]
    return pl.pallas_call(
        _combine_body,
        grid_spec=pltpu.PrefetchScalarGridSpec(
            num_scalar_prefetch=len(tabs),
            grid=(T // tm,),
            in_specs=[
                pl.BlockSpec((tm, D_MODEL), lambda i, *_: (i, 0)),
                pl.BlockSpec((tm, LANES), lambda i, *_: (i, 0)),
                pl.BlockSpec((tm, LANES), lambda i, *_: (i, 0)),
                pl.BlockSpec(memory_space=pl.ANY),
                pl.BlockSpec((1, D_MODEL), lambda i, *_: (0, 0)),
            ],
            out_specs=pl.BlockSpec((tm, D_MODEL), lambda i, *_: (i, 0)),
            scratch_shapes=[pltpu.VMEM((2, _local_rows(tm), D_MODEL), F32), pltpu.SemaphoreType.DMA((2,))],
        ),
        out_shape=jax.ShapeDtypeStruct((T, D_MODEL), F32),
        compiler_params=_params(("arbitrary",)),
        name="moe_combine",
    )(*tabs, x3, pos, gates, ys, gain)


def _routing_tables(seg, n_tt, tm, tm_e):
    seg = seg.reshape(n_tt, SUBLANES, LANES)
    lp = seg[:, 1, :N_EXPERTS]
    off = seg[:, 2, :N_EXPERTS]
    cs = jnp.cumsum(lp, axis=0) - lp
    total = jnp.sum(lp, axis=0)
    padded = ((total + tm_e - 1) // tm_e) * tm_e
    ends = jnp.cumsum(padded)
    starts = ends - padded
    n_rows = -(-(TOP_K * n_tt * tm + N_EXPERTS * (SEG_ALIGN - 1) * n_tt) // tm_e) * tm_e + N_EXPERTS * tm_e
    tile_start = jnp.arange(n_rows // tm_e, dtype=jnp.int32) * tm_e
    te = jnp.minimum(jnp.sum((tile_start[:, None] >= ends[None, :]).astype(jnp.int32), axis=1), N_EXPERTS - 1)
    n_used = (ends[-1] // tm_e).astype(jnp.int32).reshape(1)
    r0 = tile_start - starts[te]
    cs_t = cs[:, te]
    run_end_t = cs_t + lp[:, te]
    jlo = jnp.sum((run_end_t <= r0[None, :]).astype(jnp.int32), axis=0)
    jhi = jnp.sum((cs_t < (r0 + tm_e)[None, :]).astype(jnp.int32), axis=0)
    valid = jnp.clip(total[te] - r0, 0, tm_e)
    src = jnp.arange(n_tt, dtype=jnp.int32)[:, None] * _local_rows(tm) + off
    i32 = lambda a: a.astype(jnp.int32)
    expert_tabs = (i32(te), n_used, i32(r0), i32(jlo), i32(jhi), i32(valid),
                   i32(cs.T.reshape(-1)), i32(lp.T.reshape(-1)), i32(src.T.reshape(-1)))
    combine_tabs = (i32((starts[None, :] + cs).reshape(-1)), i32(lp.reshape(-1)), i32(off.reshape(-1)))
    return expert_tabs, combine_tabs, n_rows


def _rope_tables(positions):
    B, S = positions.shape
    inv_freq = jnp.power(jnp.float32(ROPE_THETA), -jnp.arange(ROPE_HALF, dtype=F32) / ROPE_HALF)
    ang = positions.astype(F32)[..., None] * inv_freq
    cos, sin = jnp.cos(ang), jnp.sin(ang)
    rest = HEAD_DIM - ROPE_DIM
    cos_h = jnp.concatenate([cos, cos, jnp.ones((B, S, rest), F32)], axis=-1)
    sin_h = jnp.concatenate([-sin, sin, jnp.zeros((B, S, rest), F32)], axis=-1)
    reps = LANES // HEAD_DIM
    return (jnp.tile(cos_h, (1, 1, reps)).reshape(B * S, LANES),
            jnp.tile(sin_h, (1, 1, reps)).reshape(B * S, LANES))


def kernel(x, positions, norm_mix, norm_ffn, w_in_ab, fgate_bias, w_out_ab, w_in_c, lower_bounds, gnorm_c, w_out_c,
           w_gate_ffn, w_up_ffn, w_down_ffn, router, w_gate_moe, w_up_moe, w_down_moe, norm_final):
    B, S, D = x.shape
    T = B * S
    assert D == D_MODEL and S % 1024 == 0
    tm = 512
    x2d = x.reshape(T, D)

    w_ab = jnp.pad(w_in_ab[0], ((0, 0), (0, QKV_WIDTH + LANES - w_in_ab.shape[2]))).astype(BF16)
    cos, sin = _rope_tables(positions)
    bias = jnp.pad(fgate_bias[0], (0, LANES - N_HEADS_B)).reshape(1, LANES)
    qkv, cum = _inproj_ab(x2d, norm_mix[0:1], w_ab, cos, sin, bias, tm, S)
    qkv = qkv.reshape(B, S, QKV_WIDTH)
    out_a = _dilated_attention(qkv, B, S).reshape(T, WIDTH_A)
    out_b = _fox_attention(qkv, cum.reshape(B, S, LANES), B, S).reshape(T, WIDTH_B)
    w_o = w_out_ab[0].astype(BF16)
    x2 = _outproj_ffn(x2d, out_a, out_b, w_o[:WIDTH_A], w_o[WIDTH_A:], norm_ffn[0:1],
                      w_gate_ffn[0].astype(BF16), w_up_ffn[0].astype(BF16), w_down_ffn[0].astype(BF16), tm)

    lb_all = jnp.cumsum(jax.nn.softmax(lower_bounds.astype(F32), axis=0), axis=0)
    lb = (lb_all - lb_all[0:1])[1].reshape(1, D)
    wq, wf, wi, wg = jnp.split(w_in_c[0], 4, axis=-1)
    w_c = jnp.concatenate([wq, wi, wg, wf], axis=-1).astype(BF16)
    qig, flog_c = _inproj_c(x2, norm_mix[1:2], w_c, tm)
    o_c = _hgrn(qig.reshape(B, S, 3 * D), flog_c.reshape(B, S, D), lb, gnorm_c[0:1], B, S).reshape(T, D)

    r_pad = jnp.pad(router[0], ((0, 0), (0, LANES - N_EXPERTS))).astype(BF16)
    x3, xl, pos, gates, seg = _outproj_router(x2, o_c, w_out_c[0].astype(BF16), norm_ffn[1:2], r_pad, tm)
    tm_e = 1024
    expert_tabs, combine_tabs, n_rows = _routing_tables(seg, T // tm, tm, tm_e)
    ys = _moe_experts(expert_tabs, xl, w_gate_moe[0].astype(BF16), w_up_moe[0].astype(BF16),
                      w_down_moe[0].astype(BF16), n_rows, T // tm, tm_e, w_gate_moe.shape[3] // 2)
    out = _moe_combine(combine_tabs, x3, pos, gates, ys, norm_final.reshape(1, D), tm)
    return out.reshape(B, S, D)
```

```python
import functools

import jax
import jax.numpy as jnp
import numpy as np
from jax import lax
from jax.experimental import pallas as pl
from jax.experimental.pallas import tpu as pltpu

F32 = jnp.float32
BF16 = jnp.bfloat16

D_MODEL = 1024
HEAD_DIM = 64
N_HEADS_A = 8
N_HEADS_B = 8
WIDTH_A = N_HEADS_A * HEAD_DIM
WIDTH_B = N_HEADS_B * HEAD_DIM
QKV_WIDTH = 3 * (WIDTH_A + WIDTH_B)
ROPE_THETA = 500000.0
ROPE_DIM = HEAD_DIM // 4
ROPE_HALF = ROPE_DIM // 2
ATT_BLOCK = 128
DILATIONS = (1, 4, 16)
N_HEADS_C = 8
HGRN_DK = 128
HGRN_CHUNK = 64
HGRN_LEAF = 16
N_EXPERTS = 8
TOP_K = 2
EPS = 1e-6
LOG2E = 1.4426950408889634

LANES = 128
SUBLANES = 8
VMEM_LIMIT = 56 * 1024 * 1024

NT_DIMS = (((1,), (1,)), ((), ()))


def _params(semantics, **kw):
    return pltpu.CompilerParams(dimension_semantics=semantics, vmem_limit_bytes=VMEM_LIMIT, **kw)


def _rmsnorm(x, gain):
    return x * lax.rsqrt(jnp.mean(x * x, axis=-1, keepdims=True) + EPS) * gain


def _sigmoid(x):
    return 1.0 / (1.0 + jnp.exp(-x))


def _split3(x):
    hi = x.astype(BF16)
    r1 = x - hi.astype(F32)
    mid = r1.astype(BF16)
    lo = (r1 - mid.astype(F32)).astype(BF16)
    return hi, mid, lo


def _dot(a, b):
    return jnp.dot(a, b, preferred_element_type=F32)


def _dot_nt(a, b):
    return lax.dot_general(a, b, NT_DIMS, preferred_element_type=F32)


def _cumsum_groups(x):
    n, w = x.shape
    rows = lax.broadcasted_iota(jnp.int32, (SUBLANES, w), 0)
    out, carry = [], None
    for g in range(n // SUBLANES):
        xg = x[SUBLANES * g:SUBLANES * (g + 1), :]
        for s in (1, 2, 4):
            xg = xg + jnp.where(rows >= s, pltpu.roll(xg, s, 0), 0.0)
        if carry is not None:
            xg = xg + carry
        carry = xg[SUBLANES - 1:SUBLANES, :]
        out.append(xg)
    return jnp.concatenate(out, axis=0)


def _inproj_ab_body(x_ref, gain_ref, w_ref, cos_ref, sin_ref, bias_ref, qkv_ref, cum_ref, carry_scr, *, tiles_per_seq):
    h = _rmsnorm(x_ref[...], gain_ref[...]).astype(BF16)

    @pl.when(pl.program_id(0) % tiles_per_seq == 0)
    def _():
        carry_scr[...] = jnp.zeros_like(carry_scr)

    x = _dot(h, w_ref[:, QKV_WIDTH:QKV_WIDTH + LANES]) + bias_ref[...]
    logf = -(jnp.maximum(-x, 0.0) + jnp.log1p(jnp.exp(-jnp.abs(x))))
    cum = _cumsum_groups(logf * LOG2E) + carry_scr[0:1, :]
    cum_ref[...] = cum
    carry_scr[0:1, :] = cum[cum.shape[0] - 1:, :]

    cos = cos_ref[...]
    sin = sin_ref[...]
    lane = lax.broadcasted_iota(jnp.int32, cos.shape, 1)
    low = (lane & (HEAD_DIM - 1)) < ROPE_HALF
    scale = HEAD_DIM ** -0.5 * LOG2E
    for c in range(QKV_WIDTH // 256):
        y = _dot(h, w_ref[:, c * 256:(c + 1) * 256])
        seg = c // 2
        for s in range(2):
            yy = y[:, s * LANES:(s + 1) * LANES]
            if seg in (0, 1):
                partner = jnp.where(low, pltpu.roll(yy, LANES - ROPE_HALF, 1), pltpu.roll(yy, ROPE_HALF, 1))
                yy = yy * cos + partner * sin
            if seg in (0, 3):
                yy = yy * scale
            qkv_ref[:, c * 256 + s * LANES:c * 256 + (s + 1) * LANES] = yy.astype(BF16)


def _inproj_ab(x2d, gain, w, cos, sin, bias, tm, seq_len):
    T = x2d.shape[0]
    wn = w.shape[1]
    return pl.pallas_call(
        functools.partial(_inproj_ab_body, tiles_per_seq=seq_len // tm),
        grid=(T // tm,),
        in_specs=[
            pl.BlockSpec((tm, D_MODEL), lambda i: (i, 0)),
            pl.BlockSpec((1, D_MODEL), lambda i: (0, 0)),
            pl.BlockSpec((D_MODEL, wn), lambda i: (0, 0)),
            pl.BlockSpec((tm, LANES), lambda i: (i, 0)),
            pl.BlockSpec((tm, LANES), lambda i: (i, 0)),
            pl.BlockSpec((1, LANES), lambda i: (0, 0)),
        ],
        out_specs=[
            pl.BlockSpec((tm, QKV_WIDTH), lambda i: (i, 0)),
            pl.BlockSpec((tm, LANES), lambda i: (i, 0)),
        ],
        out_shape=[
            jax.ShapeDtypeStruct((T, QKV_WIDTH), BF16),
            jax.ShapeDtypeStruct((T, LANES), F32),
        ],
        scratch_shapes=[pltpu.VMEM((SUBLANES, LANES), F32)],
        compiler_params=_params(("arbitrary",)),
        name="inproj_ab",
    )(x2d, gain, w, cos, sin, bias)


def _dilated_body(q_ref, k_ref, v_ref, o_ref, qf, kf, vf, ob, lb):
    S = q_ref.shape[0]
    nb = ATT_BLOCK
    qf[...] = q_ref[...].astype(F32)
    kf[...] = k_ref[...].astype(F32)
    vf[...] = v_ref[...].astype(F32)
    head0 = lax.broadcasted_iota(jnp.int32, (nb, LANES), 1) < HEAD_DIM
    qi2 = lax.broadcasted_iota(jnp.int32, (2 * nb, 2 * nb), 0) & (nb - 1)
    kj2 = lax.broadcasted_iota(jnp.int32, (2 * nb, 2 * nb), 1)
    valid2 = (kj2 >= qi2) & (kj2 <= qi2 + nb)
    qi1 = lax.broadcasted_iota(jnp.int32, (2 * nb, nb), 0) & (nb - 1)
    kj1 = lax.broadcasted_iota(jnp.int32, (2 * nb, nb), 1)
    valid1 = kj1 <= qi1

    def rows(start, size, r):
        return pl.ds(start, size) if r == 1 else pl.ds(start, size, stride=r)

    def block(br, r, q0, k0, nk):
        qs = qf[rows(q0, nb, r), :]
        kb = kf[rows(k0, nk, r), :].astype(BF16)
        vb = vf[rows(k0, nk, r), :].astype(BF16)
        q2 = jnp.concatenate([jnp.where(head0, qs, 0.0), jnp.where(head0, 0.0, qs)], axis=0).astype(BF16)
        s = jnp.where(valid2 if nk == 2 * nb else valid1, _dot_nt(q2, kb), -jnp.inf)
        m = jnp.max(s, axis=-1, keepdims=True)
        e = jnp.exp2(s - m)
        l = jnp.sum(e, axis=-1, keepdims=True)
        o = _dot(e.astype(BF16), vb) / l
        lse = jnp.broadcast_to(m + jnp.log2(l), (2 * nb, LANES))
        ob[br, rows(q0, nb, r), :] = jnp.where(head0, o[:nb], o[nb:])
        lb[br, rows(q0, nb, r), :] = jnp.where(head0, lse[:nb], lse[nb:])

    for br, r in enumerate(DILATIONS):
        n_blocks = S // (r * nb)
        if n_blocks == 1:
            def first_only(c, carry, br=br, r=r):
                block(br, r, c, c, nb)
                return carry
            lax.fori_loop(0, r, first_only, 0, unroll=True)
        else:
            for c in range(r):
                block(br, r, c, c, nb)

                def later(n, carry, br=br, r=r, c=c):
                    block(br, r, n * (nb * r) + c, (n - 1) * (nb * r) + c, 2 * nb)
                    return carry
                lax.fori_loop(1, n_blocks, later, 0, unroll=True)

    rows_per = 256
    for ch in range(S // rows_per):
        sl = pl.ds(ch * rows_per, rows_per)
        l0, l1, l2 = lb[0, sl, :], lb[1, sl, :], lb[2, sl, :]
        m = jnp.maximum(jnp.maximum(l0, l1), l2)
        w0, w1, w2 = jnp.exp2(l0 - m), jnp.exp2(l1 - m), jnp.exp2(l2 - m)
        o = (w0 * ob[0, sl, :] + w1 * ob[1, sl, :] + w2 * ob[2, sl, :]) / (w0 + w1 + w2)
        o_ref[sl, :] = o.astype(BF16)


def _dilated_attention(qkv, B, S):
    n_pairs = WIDTH_A // LANES
    blk = lambda off: pl.BlockSpec((None, S, LANES), lambda b, p: (b, 0, off + p))
    return pl.pallas_call(
        _dilated_body,
        grid=(B, n_pairs),
        in_specs=[blk(0), blk(n_pairs), blk(2 * n_pairs)],
        out_specs=pl.BlockSpec((None, S, LANES), lambda b, p: (b, 0, p)),
        out_shape=jax.ShapeDtypeStruct((B, S, WIDTH_A), BF16),
        scratch_shapes=[
            pltpu.VMEM((S, LANES), F32), pltpu.VMEM((S, LANES), F32), pltpu.VMEM((S, LANES), F32),
            pltpu.VMEM((len(DILATIONS), S, LANES), F32), pltpu.VMEM((len(DILATIONS), S, LANES), F32),
        ],
        compiler_params=_params(("parallel", "parallel")),
        name="dilated_attention",
    )(qkv, qkv, qkv)


def _fox_body(q_ref, k_ref, v_ref, c_ref, sel_ref, o_ref, qa_scr, ka_scr, *, tk):
    S = q_ref.shape[0]
    rows_per = 256
    lane = lax.broadcasted_iota(jnp.int32, (rows_per, LANES), 1)

    def build(i, carry):
        sl = pl.ds(i * rows_per, rows_per)
        q = q_ref[sl, :].astype(F32)
        k = k_ref[sl, :].astype(F32)
        extra = _dot(jnp.concatenate(_split3(c_ref[sl, :]), axis=1), sel_ref[...])
        for hh in range(2):
            own = (lane < HEAD_DIM) if hh == 0 else (lane >= HEAD_DIM)
            a0 = HEAD_DIM if hh == 0 else 0
            first = (lane >= a0) & (lane < a0 + 3)
            second = (lane >= a0 + 3) & (lane < a0 + 6)
            ex = extra[:, hh * LANES:(hh + 1) * LANES]
            qaug = jnp.where(own, q, jnp.where(first, ex, jnp.where(second, 1.0, 0.0)))
            kaug = jnp.where(own, k, jnp.where(second, ex, jnp.where(first, 1.0, 0.0)))
            qa_scr[hh, sl, :] = qaug.astype(BF16)
            ka_scr[hh, sl, :] = kaug.astype(BF16)
        return carry

    lax.fori_loop(0, S // rows_per, build, 0, unroll=True)

    n_blocks = S // tk
    row_t = lax.broadcasted_iota(jnp.int32, (tk, tk), 0)
    col_t = lax.broadcasted_iota(jnp.int32, (tk, tk), 1)
    causal = col_t <= row_t
    head0 = lax.broadcasted_iota(jnp.int32, (tk, LANES), 1) < HEAD_DIM
    state = [[None] * n_blocks for _ in range(2)]
    for j in range(n_blocks):
        ksl = pl.ds(j * tk, tk)
        vblk = v_ref[ksl, :]
        for hh in range(2):
            s_all = _dot_nt(qa_scr[hh, pl.ds(j * tk, S - j * tk), :], ka_scr[hh, ksl, :])
            es, scales = [], []
            for rb in range(j, n_blocks):
                s = s_all[(rb - j) * tk:(rb - j + 1) * tk, :]
                if rb == j:
                    s = jnp.where(causal, s, -jnp.inf)
                if j == 0:
                    m_new = jnp.max(s, axis=-1, keepdims=True)
                    alpha = None
                else:
                    m_old = state[hh][rb][0]
                    m_new = jnp.maximum(m_old, jnp.max(s, axis=-1, keepdims=True))
                    alpha = jnp.exp2(m_old - m_new)
                e = jnp.exp2(s - m_new)
                es.append(e.astype(BF16))
                scales.append((m_new, alpha, jnp.sum(e, axis=-1, keepdims=True)))
            pv_all = _dot(jnp.concatenate(es, axis=0) if len(es) > 1 else es[0], vblk)
            for rb in range(j, n_blocks):
                m_new, alpha, rowsum = scales[rb - j]
                pv = pv_all[(rb - j) * tk:(rb - j + 1) * tk, :]
                if alpha is None:
                    state[hh][rb] = (m_new, rowsum, pv)
                else:
                    _, l_old, acc_old = state[hh][rb]
                    state[hh][rb] = (m_new, alpha * l_old + rowsum, alpha * acc_old + pv)
        outs = [state[hh][j][2] / state[hh][j][1] for hh in range(2)]
        o_ref[pl.ds(j * tk, tk), :] = jnp.where(head0, outs[0], outs[1]).astype(BF16)


def _fox_selection(n_pairs):
    sel = np.zeros((n_pairs, 3 * LANES, 2 * LANES), np.float32)
    for p in range(n_pairs):
        for hh in range(2):
            a0 = hh * LANES + (HEAD_DIM if hh == 0 else 0)
            for piece in range(3):
                sel[p, piece * LANES + 2 * p + hh, a0 + piece] = 1.0
                sel[p, piece * LANES + 2 * p + hh, a0 + 3 + piece] = -1.0
    return jnp.asarray(sel, BF16)


def _fox_attention(qkv, cum, B, S, tk=256):
    n_pairs = WIDTH_B // LANES
    base = 3 * (WIDTH_A // LANES)
    blk = lambda off: pl.BlockSpec((None, S, LANES), lambda b, p: (b, 0, off + p))
    return pl.pallas_call(
        functools.partial(_fox_body, tk=tk),
        grid=(B, n_pairs),
        in_specs=[
            blk(base), blk(base + n_pairs), blk(base + 2 * n_pairs),
            pl.BlockSpec((None, S, LANES), lambda b, p: (b, 0, 0)),
            pl.BlockSpec((None, 3 * LANES, 2 * LANES), lambda b, p: (p, 0, 0)),
        ],
        out_specs=pl.BlockSpec((None, S, LANES), lambda b, p: (b, 0, p)),
        out_shape=jax.ShapeDtypeStruct((B, S, WIDTH_B), BF16),
        scratch_shapes=[pltpu.VMEM((2, S, LANES), BF16), pltpu.VMEM((2, S, LANES), BF16)],
        compiler_params=_params(("parallel", "parallel")),
        name="fox_attention",
    )(qkv, qkv, qkv, cum, _fox_selection(n_pairs))


def _outproj_ffn_body(x_ref, oa_ref, ob_ref, woa_ref, wob_ref, gain_ref, wg_ref, wu_ref, wd_ref, out_ref, a_scr):
    x1 = x_ref[...] + _dot(oa_ref[...], woa_ref[...]) + _dot(ob_ref[...], wob_ref[...])
    h = _rmsnorm(x1, gain_ref[...]).astype(BF16)
    for c in range(a_scr.shape[1] // 256):
        cols = slice(c * 256, (c + 1) * 256)
        g = _dot(h, wg_ref[:, cols])
        u = _dot(h, wu_ref[:, cols])
        a_scr[:, cols] = (g * _sigmoid(g) * u).astype(BF16)
    out_ref[...] = x1 + _dot(a_scr[...], wd_ref[...])


def _outproj_ffn(x2d, oa, ob, woa, wob, gain, wg, wu, wd, tm):
    T = x2d.shape[0]
    F = wg.shape[1]
    once = dict(pipeline_mode=pl.Buffered(1))
    return pl.pallas_call(
        _outproj_ffn_body,
        grid=(T // tm,),
        in_specs=[
            pl.BlockSpec((tm, D_MODEL), lambda i: (i, 0)),
            pl.BlockSpec((tm, WIDTH_A), lambda i: (i, 0)),
            pl.BlockSpec((tm, WIDTH_B), lambda i: (i, 0)),
            pl.BlockSpec((WIDTH_A, D_MODEL), lambda i: (0, 0), **once),
            pl.BlockSpec((WIDTH_B, D_MODEL), lambda i: (0, 0), **once),
            pl.BlockSpec((1, D_MODEL), lambda i: (0, 0)),
            pl.BlockSpec((D_MODEL, F), lambda i: (0, 0), **once),
            pl.BlockSpec((D_MODEL, F), lambda i: (0, 0), **once),
            pl.BlockSpec((F, D_MODEL), lambda i: (0, 0), **once),
        ],
        out_specs=pl.BlockSpec((tm, D_MODEL), lambda i: (i, 0)),
        out_shape=jax.ShapeDtypeStruct((T, D_MODEL), F32),
        scratch_shapes=[pltpu.VMEM((tm, F), BF16)],
        compiler_params=_params(("parallel",)),
        name="outproj_ffn",
    )(x2d, oa, ob, woa, wob, gain, wg, wu, wd)


def _inproj_c_body(x_ref, gain_ref, w_ref, qig_ref, f_ref):
    h = _rmsnorm(x_ref[...], gain_ref[...]).astype(BF16)
    n_qig = qig_ref.shape[1]
    for c in range(n_qig // 256):
        qig_ref[:, c * 256:(c + 1) * 256] = _dot(h, w_ref[:, c * 256:(c + 1) * 256]).astype(BF16)
    for c in range(f_ref.shape[1] // 256):
        f_ref[:, c * 256:(c + 1) * 256] = _dot(h, w_ref[:, n_qig + c * 256:n_qig + (c + 1) * 256])


def _inproj_c(x2d, gain, w, tm):
    T = x2d.shape[0]
    return pl.pallas_call(
        _inproj_c_body,
        grid=(T // tm,),
        in_specs=[
            pl.BlockSpec((tm, D_MODEL), lambda i: (i, 0)),
            pl.BlockSpec((1, D_MODEL), lambda i: (0, 0)),
            pl.BlockSpec((D_MODEL, 4 * D_MODEL), lambda i: (0, 0)),
        ],
        out_specs=[
            pl.BlockSpec((tm, 3 * D_MODEL), lambda i: (i, 0)),
            pl.BlockSpec((tm, D_MODEL), lambda i: (i, 0)),
        ],
        out_shape=[
            jax.ShapeDtypeStruct((T, 3 * D_MODEL), BF16),
            jax.ShapeDtypeStruct((T, D_MODEL), F32),
        ],
        compiler_params=_params(("parallel",)),
        name="inproj_c",
    )(x2d, gain, w)


def _hgrn_body(q_ref, i_ref, g_ref, f_ref, lb_ref, gn_ref, o_ref, state_scr, o_scr, ops0, ops1, dec0, dec1, *, heads):
    S = q_ref.shape[0]
    C = HGRN_CHUNK
    half = C // 2
    quarter = C // 4
    assert quarter == HGRN_LEAF
    n_chunks = S // C
    row = lax.broadcasted_iota(jnp.int32, (C, C), 0)
    col = lax.broadcasted_iota(jnp.int32, (C, C), 1)
    mask_cross = (row >= half) & (col < half)
    mask_same = ((row // half) == (col // half)) & (col <= row)
    r = lax.broadcasted_iota(jnp.int32, (C, heads * HGRN_DK), 0)
    scale = HGRN_DK ** -0.5
    gn = jnp.concatenate([gn_ref[...]] * heads, axis=1)
    state_scr[...] = jnp.zeros_like(state_scr)

    def prepare(ci, ops, dec):
        sl = pl.ds(ci * C, C)
        lbv = lb_ref[...]
        f = lbv + (1.0 - lbv) * _sigmoid(f_ref[sl, :])
        k = 1.0 - f
        q = q_ref[sl, :].astype(F32) * scale
        b = _cumsum_groups(jnp.log2(f))
        b_last = b[C - 1:C, :]
        e_cross = b - b[half - 1:half, :]
        e_same = b - jnp.where(r < half, b[quarter - 1:quarter, :], b[half + quarter - 1:half + quarter, :])
        ops[0] = (q * jnp.exp2(jnp.minimum(e_cross, 0.0))).astype(BF16)
        ops[1] = (k * jnp.exp2(jnp.minimum(-e_cross, 0.0))).astype(BF16)
        ops[2] = (q * jnp.exp2(e_same)).astype(BF16)
        ops[3] = (k * jnp.exp2(-e_same)).astype(BF16)
        ops[4] = (q * jnp.exp2(b)).astype(BF16)
        ops[5] = (k * jnp.exp2(b_last - b)).astype(BF16)
        dec[0:1, :] = jnp.exp2(b_last)

    def contract(ci, ops, dec):
        sl = pl.ds(ci * C, C)
        for hd in range(heads):
            cols = slice(hd * HGRN_DK, (hd + 1) * HGRN_DK)
            v = i_ref[sl, cols]
            scores = (jnp.where(mask_cross, _dot_nt(ops[0, :, cols], ops[1, :, cols]), 0.0)
                      + jnp.where(mask_same, _dot_nt(ops[2, :, cols], ops[3, :, cols]), 0.0))
            state_t = state_scr[hd]
            o_scr[sl, cols] = _dot(scores.astype(BF16), v) + _dot_nt(ops[4, :, cols], state_t.astype(BF16))
            v_t = v.astype(F32).T.astype(BF16)
            state_scr[hd] = state_t * dec[0:1, cols] + _dot(v_t, ops[5, :, cols])

    def finish(ci):
        sl = pl.ds(ci * C, C)
        o = o_scr[sl, :]
        ys = []
        for hd in range(heads):
            oh = o[:, hd * HGRN_DK:(hd + 1) * HGRN_DK]
            ys.append(oh * lax.rsqrt(jnp.mean(oh * oh, axis=-1, keepdims=True) + EPS))
        gate = g_ref[sl, :].astype(F32)
        o_ref[sl, :] = (jnp.concatenate(ys, axis=1) * gn * (gate * _sigmoid(gate))).astype(BF16)

    prepare(0, ops0, dec0)

    def pair(j, carry):
        prepare(2 * j + 1, ops1, dec1)
        contract(2 * j, ops0, dec0)
        prepare(2 * j + 2, ops0, dec0)
        contract(2 * j + 1, ops1, dec1)
        finish(jnp.maximum(2 * j - 1, 0))
        finish(2 * j)
        return carry

    lax.fori_loop(0, n_chunks // 2 - 1, pair, 0, unroll=5)
    prepare(n_chunks - 1, ops1, dec1)
    contract(n_chunks - 2, ops0, dec0)
    contract(n_chunks - 1, ops1, dec1)
    for ci in range(n_chunks - 3, n_chunks):
        finish(ci)


def _hgrn(qig, flog, lb, gn, B, S, heads=4):
    ng = N_HEADS_C // heads
    w = heads * HGRN_DK
    blk = lambda off: pl.BlockSpec((None, S, w), lambda b, h: (b, 0, off + h))
    return pl.pallas_call(
        functools.partial(_hgrn_body, heads=heads),
        grid=(B, ng),
        in_specs=[
            blk(0), blk(ng), blk(2 * ng),
            pl.BlockSpec((None, S, w), lambda b, h: (b, 0, h)),
            pl.BlockSpec((1, w), lambda b, h: (0, h)),
            pl.BlockSpec((1, HGRN_DK), lambda b, h: (0, 0)),
        ],
        out_specs=pl.BlockSpec((None, S, w), lambda b, h: (b, 0, h)),
        out_shape=jax.ShapeDtypeStruct((B, S, D_MODEL), BF16),
        scratch_shapes=[
            pltpu.VMEM((heads, HGRN_DK, HGRN_DK), F32), pltpu.VMEM((S, w), F32),
            pltpu.VMEM((6, HGRN_CHUNK, w), BF16), pltpu.VMEM((6, HGRN_CHUNK, w), BF16),
            pltpu.VMEM((SUBLANES, w), F32), pltpu.VMEM((SUBLANES, w), F32),
        ],
        compiler_params=_params(("parallel", "parallel")),
        name="hgrn2",
    )(qig, qig, qig, flog, lb, gn)


SEG_ALIGN = SUBLANES
SEG_BITS = tuple(range(9, 2, -1))


def _local_rows(tm):
    return TOP_K * tm + N_EXPERTS * SEG_ALIGN


def _outproj_router_body(x_ref, o_ref, w_ref, gain_ref, r_ref, x3_ref, xl_ref, pos_ref, gate_ref, seg_ref, *, tm):
    for t in range(x_ref.shape[0] // tm):
        _route_tile(x_ref, o_ref, w_ref, gain_ref, r_ref, x3_ref, xl_ref, pos_ref, gate_ref, seg_ref, t, tm)


def _route_tile(x_ref, o_ref, w_ref, gain_ref, r_ref, x3_ref, xl_ref, pos_ref, gate_ref, seg_ref, t, tm):
    rows = pl.ds(t * tm, tm)
    lr = _local_rows(tm)
    x3 = x_ref[rows, :] + _dot(o_ref[rows, :], w_ref[...])
    x3_ref[rows, :] = x3
    h = _rmsnorm(x3, gain_ref[...])
    h_bf = h.astype(BF16)
    logits = _dot(h_bf, r_ref[...])
    lane = lax.broadcasted_iota(jnp.int32, logits.shape, 1)
    lane_f = lane.astype(F32)
    lg = jnp.where(lane < N_EXPERTS, logits, -jnp.inf)
    m1 = jnp.max(lg, axis=-1, keepdims=True)
    i1 = jnp.min(jnp.where(lg == m1, lane_f, float(LANES)), axis=-1, keepdims=True)
    lg2 = jnp.where(lane_f == i1, -jnp.inf, lg)
    m2 = jnp.max(lg2, axis=-1, keepdims=True)
    i2 = jnp.min(jnp.where(lg2 == m2, lane_f, float(LANES)), axis=-1, keepdims=True)
    e2 = jnp.exp(m2 - m1)
    den = 1.0 + e2
    gate_ref[rows, :] = jnp.where(lane == 0, 1.0 / den, jnp.where(lane == 1, e2 / den, 0.0))

    oh1 = (lane_f == i1).astype(F32)
    oh2 = (lane_f == i2).astype(F32)
    c1 = _cumsum_groups(oh1)
    c2 = _cumsum_groups(oh2)
    n1 = c1[tm - 1:tm, :]
    count = n1 + c2[tm - 1:tm, :]
    padded = jnp.floor((count + (SEG_ALIGN - 1.0)) * (1.0 / SEG_ALIGN)) * SEG_ALIGN
    run = jnp.broadcast_to(padded, (SUBLANES, LANES))
    lane8 = lax.broadcasted_iota(jnp.int32, (SUBLANES, LANES), 1)
    for s in (1, 2, 4):
        run = run + jnp.where(lane8 >= s, pltpu.roll(run, s, 1), 0.0)
    start = run[0:1, :] - padded
    pos1 = jnp.sum(oh1 * (start + c1 - 1.0), axis=-1, keepdims=True)
    pos2 = jnp.sum(oh2 * (start + n1 + c2 - 1.0), axis=-1, keepdims=True)
    pos_ref[rows, :] = jnp.where(lane == 0, pos1, jnp.where(lane == 1, pos2, 0.0)).astype(jnp.int32)
    pos1_row = jnp.broadcast_to(pos1, (tm, LANES)).T[0:1, :]
    pos2_row = jnp.broadcast_to(pos2, (tm, LANES)).T[0:1, :]
    slot = lax.broadcasted_iota(jnp.int32, (lr, tm), 0).astype(F32)
    perm = ((slot == pos1_row) | (slot == pos2_row)).astype(BF16)
    xl_ref[pl.ds(t * lr, lr), :] = _dot(perm, h_bf)
    r8 = lax.broadcasted_iota(jnp.int32, (SUBLANES, LANES), 0)
    seg_ref[pl.ds(t * SUBLANES, SUBLANES), :] = jnp.where(r8 == 0, count, jnp.where(r8 == 1, padded, jnp.where(r8 == 2, start, 0.0))).astype(jnp.int32)


def _outproj_router(x2d, o, w, gain, r, tm, group=2):
    T = x2d.shape[0]
    n_tt = T // tm
    lr = _local_rows(tm)
    tg = group * tm
    return pl.pallas_call(
        functools.partial(_outproj_router_body, tm=tm),
        grid=(n_tt // group,),
        in_specs=[
            pl.BlockSpec((tg, D_MODEL), lambda i: (i, 0)),
            pl.BlockSpec((tg, D_MODEL), lambda i: (i, 0)),
            pl.BlockSpec((D_MODEL, D_MODEL), lambda i: (0, 0)),
            pl.BlockSpec((1, D_MODEL), lambda i: (0, 0)),
            pl.BlockSpec((D_MODEL, LANES), lambda i: (0, 0)),
        ],
        out_specs=[
            pl.BlockSpec((tg, D_MODEL), lambda i: (i, 0)),
            pl.BlockSpec((group * lr, D_MODEL), lambda i: (i, 0)),
            pl.BlockSpec((tg, LANES), lambda i: (i, 0)),
            pl.BlockSpec((tg, LANES), lambda i: (i, 0)),
            pl.BlockSpec((group * SUBLANES, LANES), lambda i: (i, 0)),
        ],
        out_shape=[
            jax.ShapeDtypeStruct((T, D_MODEL), F32),
            jax.ShapeDtypeStruct((n_tt * lr, D_MODEL), F32),
            jax.ShapeDtypeStruct((T, LANES), jnp.int32),
            jax.ShapeDtypeStruct((T, LANES), F32),
            jax.ShapeDtypeStruct((n_tt * SUBLANES, LANES), jnp.int32),
        ],
        compiler_params=_params(("parallel",)),
        name="outproj_router",
    )(x2d, o, w, gain, r)


def _run_copies(n, src, s0, dst, d0, sem):
    out = []
    for b in SEG_BITS:
        offs = (n >> (b + 1)) << (b + 1)
        cp = pltpu.make_async_copy(src.at[pl.ds(pl.multiple_of(s0 + offs, SEG_ALIGN), 1 << b), :],
                                   dst.at[pl.ds(pl.multiple_of(d0 + offs, SEG_ALIGN), 1 << b), :], sem)
        out.append((((n >> b) & 1) == 1, cp))
    return out


def _experts_body(te_ref, nu_ref, r0_ref, jlo_ref, jhi_ref, valid_ref, cs_ref, lp_ref, src_ref,
                  xl_hbm, wg_ref, wu_ref, wd_ref, out_ref, xbuf, xb_scr, a_scr, sem, *, n_tt):
    i = pl.program_id(0)
    f = pl.program_id(1)
    nf = pl.num_programs(1)
    tm = out_ref.shape[0]
    n_used = nu_ref[0]

    def move(tile, slot, wait):
        base = te_ref[tile] * n_tt
        r0 = r0_ref[tile]

        def one_run(j, carry):
            c0 = cs_ref[base + j]
            lo = jnp.maximum(c0, r0)
            hi = jnp.minimum(c0 + lp_ref[base + j], r0 + tm)
            n = jnp.maximum(hi - lo, 0)
            for cond, cp in _run_copies(n, xl_hbm, src_ref[base + j] + (lo - c0), xbuf.at[slot], lo - r0, sem.at[slot]):
                @pl.when(cond)
                def _():
                    cp.wait() if wait else cp.start()
            return carry

        lax.fori_loop(jlo_ref[tile], jhi_ref[tile], one_run, 0)

    def to_operand(tile, slot, r_start, n_rows):
        rows = pl.ds(r_start, n_rows)
        row = r_start + lax.broadcasted_iota(jnp.int32, (n_rows, D_MODEL), 0)
        xb_scr[slot, rows, :] = jnp.where(row < valid_ref[tile], xbuf[slot, rows, :], 0.0).astype(BF16)

    @pl.when(f == 0)
    def _():
        @pl.when(i == 0)
        def _():
            xbuf[...] = jnp.zeros_like(xbuf)
            move(0, 0, False)
            move(0, 0, True)
            to_operand(0, 0, 0, tm)

            @pl.when(1 < n_used)
            def _():
                move(1, 1, False)

        @pl.when(i + 1 < n_used)
        def _():
            move(i + 1, (i + 1) % 2, True)

        @pl.when(i + 2 < n_used)
        def _():
            move(i + 2, i % 2, False)

        out_ref[...] = jnp.zeros_like(out_ref)

    @pl.when(i < n_used)
    def _():
        part = tm // nf
        nxt = jnp.minimum(i + 1, pl.num_programs(0) - 1)
        to_operand(nxt, (i + 1) % 2, pl.multiple_of(f * part, part), part)
        xb = xb_scr[i % 2]
        for c in range(a_scr.shape[1] // 256):
            cols = slice(c * 256, (c + 1) * 256)
            g = _dot(xb, wg_ref[:, cols])
            u = _dot(xb, wu_ref[:, cols])
            a_scr[:, cols] = (g * _sigmoid(g) * u).astype(BF16)
        out_ref[...] += _dot(a_scr[...], wd_ref[...])


def _moe_experts(tabs, xl, wg, wu, wd, n_rows, n_tt, tm, tf):
    F = wg.shape[2]
    nf = F // tf

    def f_eff(i, f, nu):
        return jnp.where(i < nu[0], f, nf - 1)

    return pl.pallas_call(
        functools.partial(_experts_body, n_tt=n_tt),
        grid_spec=pltpu.PrefetchScalarGridSpec(
            num_scalar_prefetch=len(tabs),
            grid=(n_rows // tm, nf),
            in_specs=[
                pl.BlockSpec(memory_space=pl.ANY),
                pl.BlockSpec((None, D_MODEL, tf), lambda i, f, te, nu, *_: (te[i], 0, f_eff(i, f, nu))),
                pl.BlockSpec((None, D_MODEL, tf), lambda i, f, te, nu, *_: (te[i], 0, f_eff(i, f, nu))),
                pl.BlockSpec((None, tf, D_MODEL), lambda i, f, te, nu, *_: (te[i], f_eff(i, f, nu), 0)),
            ],
            out_specs=pl.BlockSpec((tm, D_MODEL), lambda i, f, *_: (i, 0)),
            scratch_shapes=[pltpu.VMEM((2, tm, D_MODEL), F32), pltpu.VMEM((2, tm, D_MODEL), BF16),
                            pltpu.VMEM((tm, tf), BF16), pltpu.SemaphoreType.DMA((2,))],
        ),
        out_shape=jax.ShapeDtypeStruct((n_rows, D_MODEL), F32),
        compiler_params=_params(("arbitrary", "arbitrary")),
        name="moe_experts",
    )(*tabs, xl, wg, wu, wd)


def _combine_body(row_ref, lp_ref, off_ref, x_ref, pos_ref, gate_ref, ys_hbm, gain_ref, out_ref, yl, sem):
    j = pl.program_id(0)
    tm = x_ref.shape[0]
    lr = yl.shape[1]

    def move(tile, slot, wait):
        for e in range(N_EXPERTS):
            t = tile * N_EXPERTS + e
            for cond, cp in _run_copies(lp_ref[t], ys_hbm, row_ref[t], yl.at[slot], off_ref[t], sem.at[slot]):
                @pl.when(cond)
                def _():
                    cp.wait() if wait else cp.start()

    @pl.when(j == 0)
    def _():
        yl[...] = jnp.zeros_like(yl)
        move(0, 0, False)

    move(j, j % 2, True)

    @pl.when(j + 1 < pl.num_programs(0))
    def _():
        move(j + 1, (j + 1) % 2, False)

    last = j * N_EXPERTS + N_EXPERTS - 1
    used = off_ref[last] + lp_ref[last]
    row = lax.broadcasted_iota(jnp.int32, (lr, D_MODEL), 0)
    y_sorted = jnp.where(row < used, yl[j % 2], 0.0).astype(BF16)
    pos = pos_ref[...]
    gates = gate_ref[...]
    slot = lax.broadcasted_iota(jnp.int32, (tm, lr), 1)
    weights = jnp.where(slot == pos[:, 0:1], gates[:, 0:1], jnp.where(slot == pos[:, 1:2], gates[:, 1:2], 0.0))
    y = x_ref[...] + _dot(weights.astype(BF16), y_sorted)
    out_ref[...] = _rmsnorm(y, gain_ref[...])


def _moe_combine(tabs, x3, pos, gates, ys, gain, tm):
    T = x3.shape[0]
    return pl.pallas_call(
        _combine_body,
        grid_spec=pltpu.PrefetchScalarGridSpec(
            num_scalar_prefetch=len(tabs),
            grid=(T // tm,),
            in_specs=[
                pl.BlockSpec((tm, D_MODEL), lambda i, *_: (i, 0)),
                pl.BlockSpec((tm, LANES), lambda i, *_: (i, 0)),
                pl.BlockSpec((tm, LANES), lambda i, *_: (i, 0)),
                pl.BlockSpec(memory_space=pl.ANY),
                pl.BlockSpec((1, D_MODEL), lambda i, *_: (0, 0)),
            ],
            out_specs=pl.BlockSpec((tm, D_MODEL), lambda i, *_: (i, 0)),
            scratch_shapes=[pltpu.VMEM((2, _local_rows(tm), D_MODEL), F32), pltpu.SemaphoreType.DMA((2,))],
        ),
        out_shape=jax.ShapeDtypeStruct((T, D_MODEL), F32),
        compiler_params=_params(("arbitrary",)),
        name="moe_combine",
    )(*tabs, x3, pos, gates, ys, gain)


def _routing_tables(seg, n_tt, tm, tm_e):
    seg = seg.reshape(n_tt, SUBLANES, LANES)
    lp = seg[:, 1, :N_EXPERTS]
    off = seg[:, 2, :N_EXPERTS]
    cs = jnp.cumsum(lp, axis=0) - lp
    total = jnp.sum(lp, axis=0)
    padded = ((total + tm_e - 1) // tm_e) * tm_e
    ends = jnp.cumsum(padded)
    starts = ends - padded
    n_rows = -(-(TOP_K * n_tt * tm + N_EXPERTS * (SEG_ALIGN - 1) * n_tt) // tm_e) * tm_e + N_EXPERTS * tm_e
    tile_start = jnp.arange(n_rows // tm_e, dtype=jnp.int32) * tm_e
    te = jnp.minimum(jnp.sum((tile_start[:, None] >= ends[None, :]).astype(jnp.int32), axis=1), N_EXPERTS - 1)
    n_used = (ends[-1] // tm_e).astype(jnp.int32).reshape(1)
    r0 = tile_start - starts[te]
    cs_t = cs[:, te]
    run_end_t = cs_t + lp[:, te]
    jlo = jnp.sum((run_end_t <= r0[None, :]).astype(jnp.int32), axis=0)
    jhi = jnp.sum((cs_t < (r0 + tm_e)[None, :]).astype(jnp.int32), axis=0)
    valid = jnp.clip(total[te] - r0, 0, tm_e)
    src = jnp.arange(n_tt, dtype=jnp.int32)[:, None] * _local_rows(tm) + off
    i32 = lambda a: a.astype(jnp.int32)
    expert_tabs = (i32(te), n_used, i32(r0), i32(jlo), i32(jhi), i32(valid),
                   i32(cs.T.reshape(-1)), i32(lp.T.reshape(-1)), i32(src.T.reshape(-1)))
    combine_tabs = (i32((starts[None, :] + cs).reshape(-1)), i32(lp.reshape(-1)), i32(off.reshape(-1)))
    return expert_tabs, combine_tabs, n_rows


def _rope_tables(positions):
    B, S = positions.shape
    inv_freq = jnp.power(jnp.float32(ROPE_THETA), -jnp.arange(ROPE_HALF, dtype=F32) / ROPE_HALF)
    ang = positions.astype(F32)[..., None] * inv_freq
    cos, sin = jnp.cos(ang), jnp.sin(ang)
    rest = HEAD_DIM - ROPE_DIM
    cos_h = jnp.concatenate([cos, cos, jnp.ones((B, S, rest), F32)], axis=-1)
    sin_h = jnp.concatenate([-sin, sin, jnp.zeros((B, S, rest), F32)], axis=-1)
    reps = LANES // HEAD_DIM
    return (jnp.tile(cos_h, (1, 1, reps)).reshape(B * S, LANES),
            jnp.tile(sin_h, (1, 1, reps)).reshape(B * S, LANES))


def kernel(x, positions, norm_mix, norm_ffn, w_in_ab, fgate_bias, w_out_ab, w_in_c, lower_bounds, gnorm_c, w_out_c,
           w_gate_ffn, w_up_ffn, w_down_ffn, router, w_gate_moe, w_up_moe, w_down_moe, norm_final):
    B, S, D = x.shape
    T = B * S
    assert D == D_MODEL and S % 1024 == 0
    tm = 512
    x2d = x.reshape(T, D)

    w_ab = jnp.pad(w_in_ab[0], ((0, 0), (0, QKV_WIDTH + LANES - w_in_ab.shape[2]))).astype(BF16)
    cos, sin = _rope_tables(positions)
    bias = jnp.pad(fgate_bias[0], (0, LANES - N_HEADS_B)).reshape(1, LANES)
    qkv, cum = _inproj_ab(x2d, norm_mix[0:1], w_ab, cos, sin, bias, tm, S)
    qkv = qkv.reshape(B, S, QKV_WIDTH)
    out_a = _dilated_attention(qkv, B, S).reshape(T, WIDTH_A)
    out_b = _fox_attention(qkv, cum.reshape(B, S, LANES), B, S).reshape(T, WIDTH_B)
    w_o = w_out_ab[0].astype(BF16)
    x2 = _outproj_ffn(x2d, out_a, out_b, w_o[:WIDTH_A], w_o[WIDTH_A:], norm_ffn[0:1],
                      w_gate_ffn[0].astype(BF16), w_up_ffn[0].astype(BF16), w_down_ffn[0].astype(BF16), tm)

    lb_all = jnp.cumsum(jax.nn.softmax(lower_bounds.astype(F32), axis=0), axis=0)
    lb = (lb_all - lb_all[0:1])[1].reshape(1, D)
    wq, wf, wi, wg = jnp.split(w_in_c[0], 4, axis=-1)
    w_c = jnp.concatenate([wq, wi, wg, wf], axis=-1).astype(BF16)
    qig, flog_c = _inproj_c(x2, norm_mix[1:2], w_c, tm)
    o_c = _hgrn(qig.reshape(B, S, 3 * D), flog_c.reshape(B, S, D), lb, gnorm_c[0:1], B, S).reshape(T, D)

    r_pad = jnp.pad(router[0], ((0, 0), (0, LANES - N_EXPERTS))).astype(BF16)
    x3, xl, pos, gates, seg = _outproj_router(x2, o_c, w_out_c[0].astype(BF16), norm_ffn[1:2], r_pad, tm)
    tm_e = 1024
    expert_tabs, combine_tabs, n_rows = _routing_tables(seg, T // tm, tm, tm_e)
    ys = _moe_experts(expert_tabs, xl, w_gate_moe[0].astype(BF16), w_up_moe[0].astype(BF16),
                      w_down_moe[0].astype(BF16), n_rows, T // tm, tm_e, w_gate_moe.shape[3] // 2)
    out = _moe_combine(combine_tabs, x3, pos, gates, ys, norm_final.reshape(1, D), tm)
    return out.reshape(B, S, D)
```

```python
import functools

import jax
import jax.numpy as jnp
import numpy as np
from jax import lax
from jax.experimental import pallas as pl
from jax.experimental.pallas import tpu as pltpu

F32 = jnp.float32
BF16 = jnp.bfloat16

D_MODEL = 1024
HEAD_DIM = 64
N_HEADS_A = 8
N_HEADS_B = 8
WIDTH_A = N_HEADS_A * HEAD_DIM
WIDTH_B = N_HEADS_B * HEAD_DIM
QKV_WIDTH = 3 * (WIDTH_A + WIDTH_B)
ROPE_THETA = 500000.0
ROPE_DIM = HEAD_DIM // 4
ROPE_HALF = ROPE_DIM // 2
ATT_BLOCK = 128
DILATIONS = (1, 4, 16)
N_HEADS_C = 8
HGRN_DK = 128
HGRN_CHUNK = 64
HGRN_LEAF = 16
N_EXPERTS = 8
TOP_K = 2
EPS = 1e-6
LOG2E = 1.4426950408889634

LANES = 128
SUBLANES = 8
VMEM_LIMIT = 56 * 1024 * 1024

NT_DIMS = (((1,), (1,)), ((), ()))


def _params(semantics, **kw):
    return pltpu.CompilerParams(dimension_semantics=semantics, vmem_limit_bytes=VMEM_LIMIT, **kw)


def _rmsnorm(x, gain):
    return x * lax.rsqrt(jnp.mean(x * x, axis=-1, keepdims=True) + EPS) * gain


def _sigmoid(x):
    return 1.0 / (1.0 + jnp.exp(-x))


def _split3(x):
    hi = x.astype(BF16)
    r1 = x - hi.astype(F32)
    mid = r1.astype(BF16)
    lo = (r1 - mid.astype(F32)).astype(BF16)
    return hi, mid, lo


def _dot(a, b):
    return jnp.dot(a, b, preferred_element_type=F32)


def _dot_nt(a, b):
    return lax.dot_general(a, b, NT_DIMS, preferred_element_type=F32)


def _cumsum_groups(x):
    n, w = x.shape
    rows = lax.broadcasted_iota(jnp.int32, (SUBLANES, w), 0)
    out, carry = [], None
    for g in range(n // SUBLANES):
        xg = x[SUBLANES * g:SUBLANES * (g + 1), :]
        for s in (1, 2, 4):
            xg = xg + jnp.where(rows >= s, pltpu.roll(xg, s, 0), 0.0)
        if carry is not None:
            xg = xg + carry
        carry = xg[SUBLANES - 1:SUBLANES, :]
        out.append(xg)
    return jnp.concatenate(out, axis=0)


def _inproj_ab_body(x_ref, gain_ref, w_ref, rope_ref, bias_ref, qkv_ref, cum_ref, carry_scr, *, tiles_per_seq):
    h = _rmsnorm(x_ref[...], gain_ref[...]).astype(BF16)

    @pl.when(pl.program_id(0) % tiles_per_seq == 0)
    def _():
        carry_scr[...] = jnp.zeros_like(carry_scr)

    x = _dot(h, w_ref[:, QKV_WIDTH:QKV_WIDTH + LANES]) + bias_ref[...]
    logf = -(jnp.maximum(-x, 0.0) + jnp.log1p(jnp.exp(-jnp.abs(x))))
    cum = _cumsum_groups(logf * LOG2E) + carry_scr[0:1, :]
    cum_ref[...] = cum
    carry_scr[0:1, :] = cum[cum.shape[0] - 1:, :]

    packed = rope_ref[...]
    lane = lax.broadcasted_iota(jnp.int32, packed.shape, 1)
    j = lane & (HEAD_DIM - 1)
    low = j < ROPE_HALF
    rot = j < ROPE_DIM
    sin_lo = pltpu.roll(packed, LANES - ROPE_DIM, 1)
    cos = jnp.where(low, packed, jnp.where(rot, pltpu.roll(packed, ROPE_HALF, 1), 1.0))
    sin = jnp.where(low, -sin_lo, jnp.where(rot, pltpu.roll(sin_lo, ROPE_HALF, 1), 0.0))
    scale = HEAD_DIM ** -0.5 * LOG2E
    for c in range(QKV_WIDTH // 256):
        y = _dot(h, w_ref[:, c * 256:(c + 1) * 256])
        seg = c // 2
        for s in range(2):
            yy = y[:, s * LANES:(s + 1) * LANES]
            if seg in (0, 1):
                partner = jnp.where(low, pltpu.roll(yy, LANES - ROPE_HALF, 1), pltpu.roll(yy, ROPE_HALF, 1))
                yy = yy * cos + partner * sin
            if seg in (0, 3):
                yy = yy * scale
            qkv_ref[:, c * 256 + s * LANES:c * 256 + (s + 1) * LANES] = yy.astype(BF16)


def _inproj_ab(x2d, gain, w, rope, bias, tm, seq_len):
    T = x2d.shape[0]
    wn = w.shape[1]
    return pl.pallas_call(
        functools.partial(_inproj_ab_body, tiles_per_seq=seq_len // tm),
        grid=(T // tm,),
        in_specs=[
            pl.BlockSpec((tm, D_MODEL), lambda i: (i, 0)),
            pl.BlockSpec((1, D_MODEL), lambda i: (0, 0)),
            pl.BlockSpec((D_MODEL, wn), lambda i: (0, 0)),
            pl.BlockSpec((tm, LANES), lambda i: (i, 0)),
            pl.BlockSpec((1, LANES), lambda i: (0, 0)),
        ],
        out_specs=[
            pl.BlockSpec((tm, QKV_WIDTH), lambda i: (i, 0)),
            pl.BlockSpec((tm, LANES), lambda i: (i, 0)),
        ],
        out_shape=[
            jax.ShapeDtypeStruct((T, QKV_WIDTH), BF16),
            jax.ShapeDtypeStruct((T, LANES), F32),
        ],
        scratch_shapes=[pltpu.VMEM((SUBLANES, LANES), F32)],
        compiler_params=_params(("arbitrary",)),
        name="inproj_ab",
    )(x2d, gain, w, rope, bias)


def _dilated_body(q_ref, k_ref, v_ref, o_ref, qf, kf, vf, ob, lb):
    S = q_ref.shape[0]
    nb = ATT_BLOCK
    qf[...] = q_ref[...].astype(F32)
    kf[...] = k_ref[...].astype(F32)
    vf[...] = v_ref[...].astype(F32)
    head0 = lax.broadcasted_iota(jnp.int32, (nb, LANES), 1) < HEAD_DIM
    qi2 = lax.broadcasted_iota(jnp.int32, (2 * nb, 2 * nb), 0) & (nb - 1)
    kj2 = lax.broadcasted_iota(jnp.int32, (2 * nb, 2 * nb), 1)
    valid2 = (kj2 >= qi2) & (kj2 <= qi2 + nb)
    qi1 = lax.broadcasted_iota(jnp.int32, (2 * nb, nb), 0) & (nb - 1)
    kj1 = lax.broadcasted_iota(jnp.int32, (2 * nb, nb), 1)
    valid1 = kj1 <= qi1

    def rows(start, size, r):
        return pl.ds(start, size) if r == 1 else pl.ds(start, size, stride=r)

    def block(br, r, q0, k0, nk):
        qs = qf[rows(q0, nb, r), :]
        kb = kf[rows(k0, nk, r), :].astype(BF16)
        vb = vf[rows(k0, nk, r), :].astype(BF16)
        q2 = jnp.concatenate([jnp.where(head0, qs, 0.0), jnp.where(head0, 0.0, qs)], axis=0).astype(BF16)
        s = jnp.where(valid2 if nk == 2 * nb else valid1, _dot_nt(q2, kb), -jnp.inf)
        m = jnp.max(s, axis=-1, keepdims=True)
        e = jnp.exp2(s - m)
        l = jnp.sum(e, axis=-1, keepdims=True)
        o = _dot(e.astype(BF16), vb) / l
        lse = jnp.broadcast_to(m + jnp.log2(l), (2 * nb, LANES))
        ob[br, rows(q0, nb, r), :] = jnp.where(head0, o[:nb], o[nb:])
        lb[br, rows(q0, nb, r), :] = jnp.where(head0, lse[:nb], lse[nb:])

    for br, r in enumerate(DILATIONS):
        n_blocks = S // (r * nb)
        if n_blocks == 1:
            def first_only(c, carry, br=br, r=r):
                block(br, r, c, c, nb)
                return carry
            lax.fori_loop(0, r, first_only, 0, unroll=True)
        else:
            for c in range(r):
                block(br, r, c, c, nb)

                def later(n, carry, br=br, r=r, c=c):
                    block(br, r, n * (nb * r) + c, (n - 1) * (nb * r) + c, 2 * nb)
                    return carry
                lax.fori_loop(1, n_blocks, later, 0, unroll=True)

    rows_per = 256
    for ch in range(S // rows_per):
        sl = pl.ds(ch * rows_per, rows_per)
        l0, l1, l2 = lb[0, sl, :], lb[1, sl, :], lb[2, sl, :]
        m = jnp.maximum(jnp.maximum(l0, l1), l2)
        w0, w1, w2 = jnp.exp2(l0 - m), jnp.exp2(l1 - m), jnp.exp2(l2 - m)
        o = (w0 * ob[0, sl, :] + w1 * ob[1, sl, :] + w2 * ob[2, sl, :]) / (w0 + w1 + w2)
        o_ref[sl, :] = o.astype(BF16)


def _dilated_attention(qkv, B, S):
    n_pairs = WIDTH_A // LANES
    blk = lambda off: pl.BlockSpec((None, S, LANES), lambda b, p: (b, 0, off + p))
    return pl.pallas_call(
        _dilated_body,
        grid=(B, n_pairs),
        in_specs=[blk(0), blk(n_pairs), blk(2 * n_pairs)],
        out_specs=pl.BlockSpec((None, S, LANES), lambda b, p: (b, 0, p)),
        out_shape=jax.ShapeDtypeStruct((B, S, WIDTH_A), BF16),
        scratch_shapes=[
            pltpu.VMEM((S, LANES), F32), pltpu.VMEM((S, LANES), F32), pltpu.VMEM((S, LANES), F32),
            pltpu.VMEM((len(DILATIONS), S, LANES), F32), pltpu.VMEM((len(DILATIONS), S, LANES), F32),
        ],
        compiler_params=_params(("parallel", "parallel")),
        name="dilated_attention",
    )(qkv, qkv, qkv)


def _fox_body(q_ref, k_ref, v_ref, c_ref, sel_ref, o_ref, qa_scr, ka_scr, *, tk):
    S = q_ref.shape[0]
    rows_per = 256
    lane = lax.broadcasted_iota(jnp.int32, (rows_per, LANES), 1)

    def build(i, carry):
        sl = pl.ds(i * rows_per, rows_per)
        q = q_ref[sl, :].astype(F32)
        k = k_ref[sl, :].astype(F32)
        extra = _dot(jnp.concatenate(_split3(c_ref[sl, :]), axis=1), sel_ref[...])
        for hh in range(2):
            own = (lane < HEAD_DIM) if hh == 0 else (lane >= HEAD_DIM)
            a0 = HEAD_DIM if hh == 0 else 0
            first = (lane >= a0) & (lane < a0 + 3)
            second = (lane >= a0 + 3) & (lane < a0 + 6)
            ex = extra[:, hh * LANES:(hh + 1) * LANES]
            qaug = jnp.where(own, q, jnp.where(first, ex, jnp.where(second, 1.0, 0.0)))
            kaug = jnp.where(own, k, jnp.where(second, ex, jnp.where(first, 1.0, 0.0)))
            qa_scr[hh, sl, :] = qaug.astype(BF16)
            ka_scr[hh, sl, :] = kaug.astype(BF16)
        return carry

    lax.fori_loop(0, S // rows_per, build, 0, unroll=True)

    n_blocks = S // tk
    row_t = lax.broadcasted_iota(jnp.int32, (tk, tk), 0)
    col_t = lax.broadcasted_iota(jnp.int32, (tk, tk), 1)
    causal = col_t <= row_t
    head0 = lax.broadcasted_iota(jnp.int32, (tk, LANES), 1) < HEAD_DIM
    state = [[None] * n_blocks for _ in range(2)]
    for j in range(n_blocks):
        ksl = pl.ds(j * tk, tk)
        vblk = v_ref[ksl, :]
        for hh in range(2):
            s_all = _dot_nt(qa_scr[hh, pl.ds(j * tk, S - j * tk), :], ka_scr[hh, ksl, :])
            es, scales = [], []
            for rb in range(j, n_blocks):
                s = s_all[(rb - j) * tk:(rb - j + 1) * tk, :]
                if rb == j:
                    s = jnp.where(causal, s, -jnp.inf)
                if j == 0:
                    m_new = jnp.max(s, axis=-1, keepdims=True)
                    alpha = None
                else:
                    m_old = state[hh][rb][0]
                    m_new = jnp.maximum(m_old, jnp.max(s, axis=-1, keepdims=True))
                    alpha = jnp.exp2(m_old - m_new)
                e = jnp.exp2(s - m_new)
                es.append(e.astype(BF16))
                scales.append((m_new, alpha, jnp.sum(e, axis=-1, keepdims=True)))
            pv_all = _dot(jnp.concatenate(es, axis=0) if len(es) > 1 else es[0], vblk)
            for rb in range(j, n_blocks):
                m_new, alpha, rowsum = scales[rb - j]
                pv = pv_all[(rb - j) * tk:(rb - j + 1) * tk, :]
                if alpha is None:
                    state[hh][rb] = (m_new, rowsum, pv)
                else:
                    _, l_old, acc_old = state[hh][rb]
                    state[hh][rb] = (m_new, alpha * l_old + rowsum, alpha * acc_old + pv)
        outs = [state[hh][j][2] / state[hh][j][1] for hh in range(2)]
        o_ref[pl.ds(j * tk, tk), :] = jnp.where(head0, outs[0], outs[1]).astype(BF16)


def _fox_selection(n_pairs):
    sel = np.zeros((n_pairs, 3 * LANES, 2 * LANES), np.float32)
    for p in range(n_pairs):
        for hh in range(2):
            a0 = hh * LANES + (HEAD_DIM if hh == 0 else 0)
            for piece in range(3):
                sel[p, piece * LANES + 2 * p + hh, a0 + piece] = 1.0
                sel[p, piece * LANES + 2 * p + hh, a0 + 3 + piece] = -1.0
    return jnp.asarray(sel, BF16)


def _fox_attention(qkv, cum, B, S, tk=256):
    n_pairs = WIDTH_B // LANES
    base = 3 * (WIDTH_A // LANES)
    blk = lambda off: pl.BlockSpec((None, S, LANES), lambda b, p: (b, 0, off + p))
    return pl.pallas_call(
        functools.partial(_fox_body, tk=tk),
        grid=(B, n_pairs),
        in_specs=[
            blk(base), blk(base + n_pairs), blk(base + 2 * n_pairs),
            pl.BlockSpec((None, S, LANES), lambda b, p: (b, 0, 0)),
            pl.BlockSpec((None, 3 * LANES, 2 * LANES), lambda b, p: (p, 0, 0)),
        ],
        out_specs=pl.BlockSpec((None, S, LANES), lambda b, p: (b, 0, p)),
        out_shape=jax.ShapeDtypeStruct((B, S, WIDTH_B), BF16),
        scratch_shapes=[pltpu.VMEM((2, S, LANES), BF16), pltpu.VMEM((2, S, LANES), BF16)],
        compiler_params=_params(("parallel", "parallel")),
        name="fox_attention",
    )(qkv, qkv, qkv, cum, _fox_selection(n_pairs))


def _outproj_ffn_body(x_ref, oa_ref, ob_ref, woa_ref, wob_ref, gain_ref, wg_ref, wu_ref, wd_ref, out_ref, a_scr):
    x1 = x_ref[...] + _dot(oa_ref[...], woa_ref[...]) + _dot(ob_ref[...], wob_ref[...])
    h = _rmsnorm(x1, gain_ref[...]).astype(BF16)
    for c in range(a_scr.shape[1] // 256):
        cols = slice(c * 256, (c + 1) * 256)
        g = _dot(h, wg_ref[:, cols])
        u = _dot(h, wu_ref[:, cols])
        a_scr[:, cols] = (g * _sigmoid(g) * u).astype(BF16)
    out_ref[...] = x1 + _dot(a_scr[...], wd_ref[...])


def _outproj_ffn(x2d, oa, ob, woa, wob, gain, wg, wu, wd, tm):
    T = x2d.shape[0]
    F = wg.shape[1]
    once = dict(pipeline_mode=pl.Buffered(1))
    return pl.pallas_call(
        _outproj_ffn_body,
        grid=(T // tm,),
        in_specs=[
            pl.BlockSpec((tm, D_MODEL), lambda i: (i, 0)),
            pl.BlockSpec((tm, WIDTH_A), lambda i: (i, 0)),
            pl.BlockSpec((tm, WIDTH_B), lambda i: (i, 0)),
            pl.BlockSpec((WIDTH_A, D_MODEL), lambda i: (0, 0), **once),
            pl.BlockSpec((WIDTH_B, D_MODEL), lambda i: (0, 0), **once),
            pl.BlockSpec((1, D_MODEL), lambda i: (0, 0)),
            pl.BlockSpec((D_MODEL, F), lambda i: (0, 0), **once),
            pl.BlockSpec((D_MODEL, F), lambda i: (0, 0), **once),
            pl.BlockSpec((F, D_MODEL), lambda i: (0, 0), **once),
        ],
        out_specs=pl.BlockSpec((tm, D_MODEL), lambda i: (i, 0)),
        out_shape=jax.ShapeDtypeStruct((T, D_MODEL), F32),
        scratch_shapes=[pltpu.VMEM((tm, F), BF16)],
        compiler_params=_params(("parallel",)),
        name="outproj_ffn",
    )(x2d, oa, ob, woa, wob, gain, wg, wu, wd)


def _inproj_c_body(x_ref, gain_ref, w_ref, qig_ref, f_ref):
    h = _rmsnorm(x_ref[...], gain_ref[...]).astype(BF16)
    n_qig = qig_ref.shape[1]
    for c in range(n_qig // 256):
        qig_ref[:, c * 256:(c + 1) * 256] = _dot(h, w_ref[:, c * 256:(c + 1) * 256]).astype(BF16)
    for c in range(f_ref.shape[1] // 256):
        f_ref[:, c * 256:(c + 1) * 256] = _dot(h, w_ref[:, n_qig + c * 256:n_qig + (c + 1) * 256])


def _inproj_c(x2d, gain, w, tm):
    T = x2d.shape[0]
    return pl.pallas_call(
        _inproj_c_body,
        grid=(T // tm,),
        in_specs=[
            pl.BlockSpec((tm, D_MODEL), lambda i: (i, 0)),
            pl.BlockSpec((1, D_MODEL), lambda i: (0, 0)),
            pl.BlockSpec((D_MODEL, 4 * D_MODEL), lambda i: (0, 0)),
        ],
        out_specs=[
            pl.BlockSpec((tm, 3 * D_MODEL), lambda i: (i, 0)),
            pl.BlockSpec((tm, D_MODEL), lambda i: (i, 0)),
        ],
        out_shape=[
            jax.ShapeDtypeStruct((T, 3 * D_MODEL), BF16),
            jax.ShapeDtypeStruct((T, D_MODEL), F32),
        ],
        compiler_params=_params(("parallel",)),
        name="inproj_c",
    )(x2d, gain, w)


def _hgrn_body(q_ref, i_ref, g_ref, f_ref, lb_ref, gn_ref, o_ref, state_scr, o_scr, ops0, ops1, dec0, dec1, *, heads):
    S = q_ref.shape[0]
    C = HGRN_CHUNK
    half = C // 2
    quarter = C // 4
    assert quarter == HGRN_LEAF
    n_chunks = S // C
    row = lax.broadcasted_iota(jnp.int32, (C, C), 0)
    col = lax.broadcasted_iota(jnp.int32, (C, C), 1)
    mask_cross = (row >= half) & (col < half)
    mask_same = ((row // half) == (col // half)) & (col <= row)
    r = lax.broadcasted_iota(jnp.int32, (C, heads * HGRN_DK), 0)
    scale = HGRN_DK ** -0.5
    gn = jnp.concatenate([gn_ref[...]] * heads, axis=1)
    state_scr[...] = jnp.zeros_like(state_scr)

    def prepare(ci, ops, dec):
        sl = pl.ds(ci * C, C)
        lbv = lb_ref[...]
        f = lbv + (1.0 - lbv) * _sigmoid(f_ref[sl, :])
        k = 1.0 - f
        q = q_ref[sl, :].astype(F32) * scale
        b = _cumsum_groups(jnp.log2(f))
        b_last = b[C - 1:C, :]
        e_cross = b - b[half - 1:half, :]
        e_same = b - jnp.where(r < half, b[quarter - 1:quarter, :], b[half + quarter - 1:half + quarter, :])
        ops[0] = (q * jnp.exp2(jnp.minimum(e_cross, 0.0))).astype(BF16)
        ops[1] = (k * jnp.exp2(jnp.minimum(-e_cross, 0.0))).astype(BF16)
        ops[2] = (q * jnp.exp2(e_same)).astype(BF16)
        ops[3] = (k * jnp.exp2(-e_same)).astype(BF16)
        ops[4] = (q * jnp.exp2(b)).astype(BF16)
        ops[5] = (k * jnp.exp2(b_last - b)).astype(BF16)
        dec[0:1, :] = jnp.exp2(b_last)

    def contract(ci, ops, dec):
        sl = pl.ds(ci * C, C)
        for hd in range(heads):
            cols = slice(hd * HGRN_DK, (hd + 1) * HGRN_DK)
            v = i_ref[sl, cols]
            scores = (jnp.where(mask_cross, _dot_nt(ops[0, :, cols], ops[1, :, cols]), 0.0)
                      + jnp.where(mask_same, _dot_nt(ops[2, :, cols], ops[3, :, cols]), 0.0))
            state_t = state_scr[hd]
            o_scr[sl, cols] = _dot(scores.astype(BF16), v) + _dot_nt(ops[4, :, cols], state_t.astype(BF16))
            v_t = v.astype(F32).T.astype(BF16)
            state_scr[hd] = state_t * dec[0:1, cols] + _dot(v_t, ops[5, :, cols])

    def finish(ci):
        sl = pl.ds(ci * C, C)
        o = o_scr[sl, :]
        ys = []
        for hd in range(heads):
            oh = o[:, hd * HGRN_DK:(hd + 1) * HGRN_DK]
            ys.append(oh * lax.rsqrt(jnp.mean(oh * oh, axis=-1, keepdims=True) + EPS))
        gate = g_ref[sl, :].astype(F32)
        o_ref[sl, :] = (jnp.concatenate(ys, axis=1) * gn * (gate * _sigmoid(gate))).astype(BF16)

    prepare(0, ops0, dec0)

    def pair(j, carry):
        prepare(2 * j + 1, ops1, dec1)
        contract(2 * j, ops0, dec0)
        prepare(2 * j + 2, ops0, dec0)
        contract(2 * j + 1, ops1, dec1)
        finish(jnp.maximum(2 * j - 1, 0))
        finish(2 * j)
        return carry

    lax.fori_loop(0, n_chunks // 2 - 1, pair, 0, unroll=5)
    prepare(n_chunks - 1, ops1, dec1)
    contract(n_chunks - 2, ops0, dec0)
    contract(n_chunks - 1, ops1, dec1)
    for ci in range(n_chunks - 3, n_chunks):
        finish(ci)


def _hgrn(qig, flog, lb, gn, B, S, heads=4):
    ng = N_HEADS_C // heads
    w = heads * HGRN_DK
    blk = lambda off: pl.BlockSpec((None, S, w), lambda b, h: (b, 0, off + h))
    return pl.pallas_call(
        functools.partial(_hgrn_body, heads=heads),
        grid=(B, ng),
        in_specs=[
            blk(0), blk(ng), blk(2 * ng),
            pl.BlockSpec((None, S, w), lambda b, h: (b, 0, h)),
            pl.BlockSpec((1, w), lambda b, h: (0, h)),
            pl.BlockSpec((1, HGRN_DK), lambda b, h: (0, 0)),
        ],
        out_specs=pl.BlockSpec((None, S, w), lambda b, h: (b, 0, h)),
        out_shape=jax.ShapeDtypeStruct((B, S, D_MODEL), BF16),
        scratch_shapes=[
            pltpu.VMEM((heads, HGRN_DK, HGRN_DK), F32), pltpu.VMEM((S, w), F32),
            pltpu.VMEM((6, HGRN_CHUNK, w), BF16), pltpu.VMEM((6, HGRN_CHUNK, w), BF16),
            pltpu.VMEM((SUBLANES, w), F32), pltpu.VMEM((SUBLANES, w), F32),
        ],
        compiler_params=_params(("parallel", "parallel")),
        name="hgrn2",
    )(qig, qig, qig, flog, lb, gn)


SEG_ALIGN = SUBLANES
SEG_BITS = tuple(range(9, 2, -1))


def _local_rows(tm):
    return TOP_K * tm + N_EXPERTS * SEG_ALIGN


def _outproj_router_body(x_ref, o_ref, w_ref, gain_ref, r_ref, x3_ref, xl_ref, pos_ref, gate_ref, seg_ref, *, tm):
    for t in range(x_ref.shape[0] // tm):
        _route_tile(x_ref, o_ref, w_ref, gain_ref, r_ref, x3_ref, xl_ref, pos_ref, gate_ref, seg_ref, t, tm)


def _route_tile(x_ref, o_ref, w_ref, gain_ref, r_ref, x3_ref, xl_ref, pos_ref, gate_ref, seg_ref, t, tm):
    rows = pl.ds(t * tm, tm)
    lr = _local_rows(tm)
    x3 = x_ref[rows, :] + _dot(o_ref[rows, :], w_ref[...])
    x3_ref[rows, :] = x3
    h = _rmsnorm(x3, gain_ref[...])
    h_bf = h.astype(BF16)
    logits = _dot(h_bf, r_ref[...])
    lane = lax.broadcasted_iota(jnp.int32, logits.shape, 1)
    lane_f = lane.astype(F32)
    lg = jnp.where(lane < N_EXPERTS, logits, -jnp.inf)
    m1 = jnp.max(lg, axis=-1, keepdims=True)
    i1 = jnp.min(jnp.where(lg == m1, lane_f, float(LANES)), axis=-1, keepdims=True)
    lg2 = jnp.where(lane_f == i1, -jnp.inf, lg)
    m2 = jnp.max(lg2, axis=-1, keepdims=True)
    i2 = jnp.min(jnp.where(lg2 == m2, lane_f, float(LANES)), axis=-1, keepdims=True)
    e2 = jnp.exp(m2 - m1)
    den = 1.0 + e2
    gate_ref[rows, :] = jnp.where(lane == 0, 1.0 / den, jnp.where(lane == 1, e2 / den, 0.0))

    oh1 = (lane_f == i1).astype(F32)
    oh2 = (lane_f == i2).astype(F32)
    c1 = _cumsum_groups(oh1)
    c2 = _cumsum_groups(oh2)
    n1 = c1[tm - 1:tm, :]
    count = n1 + c2[tm - 1:tm, :]
    padded = jnp.floor((count + (SEG_ALIGN - 1.0)) * (1.0 / SEG_ALIGN)) * SEG_ALIGN
    run = jnp.broadcast_to(padded, (SUBLANES, LANES))
    lane8 = lax.broadcasted_iota(jnp.int32, (SUBLANES, LANES), 1)
    for s in (1, 2, 4):
        run = run + jnp.where(lane8 >= s, pltpu.roll(run, s, 1), 0.0)
    start = run[0:1, :] - padded
    pos1 = jnp.sum(oh1 * (start + c1 - 1.0), axis=-1, keepdims=True)
    pos2 = jnp.sum(oh2 * (start + n1 + c2 - 1.0), axis=-1, keepdims=True)
    pos_ref[rows, :] = jnp.where(lane == 0, pos1, jnp.where(lane == 1, pos2, 0.0)).astype(jnp.int32)
    pos1_row = jnp.broadcast_to(pos1, (tm, LANES)).T[0:1, :]
    pos2_row = jnp.broadcast_to(pos2, (tm, LANES)).T[0:1, :]
    slot = lax.broadcasted_iota(jnp.int32, (lr, tm), 0).astype(F32)
    perm = ((slot == pos1_row) | (slot == pos2_row)).astype(BF16)
    xl_ref[pl.ds(t * lr, lr), :] = _dot(perm, h_bf)
    r8 = lax.broadcasted_iota(jnp.int32, (SUBLANES, LANES), 0)
    seg_ref[pl.ds(t * SUBLANES, SUBLANES), :] = jnp.where(r8 == 0, count, jnp.where(r8 == 1, padded, jnp.where(r8 == 2, start, 0.0))).astype(jnp.int32)


def _outproj_router(x2d, o, w, gain, r, tm, group=2):
    T = x2d.shape[0]
    n_tt = T // tm
    lr = _local_rows(tm)
    tg = group * tm
    return pl.pallas_call(
        functools.partial(_outproj_router_body, tm=tm),
        grid=(n_tt // group,),
        in_specs=[
            pl.BlockSpec((tg, D_MODEL), lambda i: (i, 0)),
            pl.BlockSpec((tg, D_MODEL), lambda i: (i, 0)),
            pl.BlockSpec((D_MODEL, D_MODEL), lambda i: (0, 0)),
            pl.BlockSpec((1, D_MODEL), lambda i: (0, 0)),
            pl.BlockSpec((D_MODEL, LANES), lambda i: (0, 0)),
        ],
        out_specs=[
            pl.BlockSpec((tg, D_MODEL), lambda i: (i, 0)),
            pl.BlockSpec((group * lr, D_MODEL), lambda i: (i, 0)),
            pl.BlockSpec((tg, LANES), lambda i: (i, 0)),
            pl.BlockSpec((tg, LANES), lambda i: (i, 0)),
            pl.BlockSpec((group * SUBLANES, LANES), lambda i: (i, 0)),
        ],
        out_shape=[
            jax.ShapeDtypeStruct((T, D_MODEL), F32),
            jax.ShapeDtypeStruct((n_tt * lr, D_MODEL), F32),
            jax.ShapeDtypeStruct((T, LANES), jnp.int32),
            jax.ShapeDtypeStruct((T, LANES), F32),
            jax.ShapeDtypeStruct((n_tt * SUBLANES, LANES), jnp.int32),
        ],
        compiler_params=_params(("parallel",)),
        name="outproj_router",
    )(x2d, o, w, gain, r)


def _run_copies(n, src, s0, dst, d0, sem):
    out = []
    for b in SEG_BITS:
        offs = (n >> (b + 1)) << (b + 1)
        cp = pltpu.make_async_copy(src.at[pl.ds(pl.multiple_of(s0 + offs, SEG_ALIGN), 1 << b), :],
                                   dst.at[pl.ds(pl.multiple_of(d0 + offs, SEG_ALIGN), 1 << b), :], sem)
        out.append((((n >> b) & 1) == 1, cp))
    return out


def _experts_body(te_ref, nu_ref, r0_ref, jlo_ref, jhi_ref, valid_ref, cs_ref, lp_ref, src_ref,
                  xl_hbm, wg_ref, wu_ref, wd_ref, out_ref, xbuf, xb_scr, a_scr, sem, *, n_tt):
    i = pl.program_id(0)
    f = pl.program_id(1)
    tm = out_ref.shape[0]

    def move(tile, slot, wait):
        base = te_ref[tile] * n_tt
        r0 = r0_ref[tile]

        def one_run(j, carry):
            c0 = cs_ref[base + j]
            lo = jnp.maximum(c0, r0)
            hi = jnp.minimum(c0 + lp_ref[base + j], r0 + tm)
            n = jnp.maximum(hi - lo, 0)
            for cond, cp in _run_copies(n, xl_hbm, src_ref[base + j] + (lo - c0), xbuf.at[slot], lo - r0, sem.at[slot]):
                @pl.when(cond)
                def _():
                    cp.wait() if wait else cp.start()
            return carry

        lax.fori_loop(jlo_ref[tile], jhi_ref[tile], one_run, 0)

    @pl.when(f == 0)
    def _():
        @pl.when(i == 0)
        def _():
            xbuf[...] = jnp.zeros_like(xbuf)
            move(0, 0, False)

        @pl.when(i < nu_ref[0])
        def _():
            move(i, i % 2, True)

        @pl.when(i + 1 < nu_ref[0])
        def _():
            move(i + 1, (i + 1) % 2, False)

        row = lax.broadcasted_iota(jnp.int32, (tm, D_MODEL), 0)
        xb_scr[...] = jnp.where(row < valid_ref[i], xbuf[i % 2], 0.0).astype(BF16)
        out_ref[...] = jnp.zeros_like(out_ref)

    @pl.when(i < nu_ref[0])
    def _():
        xb = xb_scr[...]
        for c in range(a_scr.shape[1] // 256):
            cols = slice(c * 256, (c + 1) * 256)
            g = _dot(xb, wg_ref[:, cols])
            u = _dot(xb, wu_ref[:, cols])
            a_scr[:, cols] = (g * _sigmoid(g) * u).astype(BF16)
        out_ref[...] += _dot(a_scr[...], wd_ref[...])


def _moe_experts(tabs, xl, wg, wu, wd, n_rows, n_tt, tm, tf):
    F = wg.shape[2]
    nf = F // tf

    def f_eff(i, f, nu):
        return jnp.where(i < nu[0], f, nf - 1)

    return pl.pallas_call(
        functools.partial(_experts_body, n_tt=n_tt),
        grid_spec=pltpu.PrefetchScalarGridSpec(
            num_scalar_prefetch=len(tabs),
            grid=(n_rows // tm, nf),
            in_specs=[
                pl.BlockSpec(memory_space=pl.ANY),
                pl.BlockSpec((None, D_MODEL, tf), lambda i, f, te, nu, *_: (te[i], 0, f_eff(i, f, nu))),
                pl.BlockSpec((None, D_MODEL, tf), lambda i, f, te, nu, *_: (te[i], 0, f_eff(i, f, nu))),
                pl.BlockSpec((None, tf, D_MODEL), lambda i, f, te, nu, *_: (te[i], f_eff(i, f, nu), 0)),
            ],
            out_specs=pl.BlockSpec((tm, D_MODEL), lambda i, f, *_: (i, 0)),
            scratch_shapes=[pltpu.VMEM((2, tm, D_MODEL), F32), pltpu.VMEM((tm, D_MODEL), BF16),
                            pltpu.VMEM((tm, tf), BF16), pltpu.SemaphoreType.DMA((2,))],
        ),
        out_shape=jax.ShapeDtypeStruct((n_rows, D_MODEL), F32),
        compiler_params=_params(("arbitrary", "arbitrary")),
        name="moe_experts",
    )(*tabs, xl, wg, wu, wd)


def _combine_body(row_ref, lp_ref, off_ref, x_ref, pos_ref, gate_ref, ys_hbm, gain_ref, out_ref, yl, sem):
    j = pl.program_id(0)
    tm = x_ref.shape[0]
    lr = yl.shape[1]

    def move(tile, slot, wait):
        for e in range(N_EXPERTS):
            t = tile * N_EXPERTS + e
            for cond, cp in _run_copies(lp_ref[t], ys_hbm, row_ref[t], yl.at[slot], off_ref[t], sem.at[slot]):
                @pl.when(cond)
                def _():
                    cp.wait() if wait else cp.start()

    @pl.when(j == 0)
    def _():
        yl[...] = jnp.zeros_like(yl)
        move(0, 0, False)

    move(j, j % 2, True)

    @pl.when(j + 1 < pl.num_programs(0))
    def _():
        move(j + 1, (j + 1) % 2, False)

    last = j * N_EXPERTS + N_EXPERTS - 1
    used = off_ref[last] + lp_ref[last]
    row = lax.broadcasted_iota(jnp.int32, (lr, D_MODEL), 0)
    y_sorted = jnp.where(row < used, yl[j % 2], 0.0).astype(BF16)
    pos = pos_ref[...]
    gates = gate_ref[...]
    slot = lax.broadcasted_iota(jnp.int32, (tm, lr), 1)
    weights = jnp.where(slot == pos[:, 0:1], gates[:, 0:1], jnp.where(slot == pos[:, 1:2], gates[:, 1:2], 0.0))
    y = x_ref[...] + _dot(weights.astype(BF16), y_sorted)
    out_ref[...] = _rmsnorm(y, gain_ref[...])


def _moe_combine(tabs, x3, pos, gates, ys, gain, tm):
    T = x3.shape[0]
    return pl.pallas_call(
        _combine_body,
        grid_spec=pltpu.PrefetchScalarGridSpec(
            num_scalar_prefetch=len(tabs),
            grid=(T // tm,),
            in_specs=[
                pl.BlockSpec((tm, D_MODEL), lambda i, *_: (i, 0)),
                pl.BlockSpec((tm, LANES), lambda i, *_: (i, 0)),
                pl.BlockSpec((tm, LANES), lambda i, *_: (i, 0)),
                pl.BlockSpec(memory_space=pl.ANY),
                pl.BlockSpec((1, D_MODEL), lambda i, *_: (0, 0)),
            ],
            out_specs=pl.BlockSpec((tm, D_MODEL), lambda i, *_: (i, 0)),
            scratch_shapes=[pltpu.VMEM((2, _local_rows(tm), D_MODEL), F32), pltpu.SemaphoreType.DMA((2,))],
        ),
        out_shape=jax.ShapeDtypeStruct((T, D_MODEL), F32),
        compiler_params=_params(("arbitrary",)),
        name="moe_combine",
    )(*tabs, x3, pos, gates, ys, gain)


def _routing_tables(seg, n_tt, tm, tm_e):
    seg = seg.reshape(n_tt, SUBLANES, LANES)
    lp = seg[:, 1, :N_EXPERTS]
    off = seg[:, 2, :N_EXPERTS]
    cs = jnp.cumsum(lp, axis=0) - lp
    total = jnp.sum(lp, axis=0)
    padded = ((total + tm_e - 1) // tm_e) * tm_e
    ends = jnp.cumsum(padded)
    starts = ends - padded
    n_rows = -(-(TOP_K * n_tt * tm + N_EXPERTS * (SEG_ALIGN - 1) * n_tt) // tm_e) * tm_e + N_EXPERTS * tm_e
    tile_start = jnp.arange(n_rows // tm_e, dtype=jnp.int32) * tm_e
    te = jnp.minimum(jnp.sum((tile_start[:, None] >= ends[None, :]).astype(jnp.int32), axis=1), N_EXPERTS - 1)
    n_used = (ends[-1] // tm_e).astype(jnp.int32).reshape(1)
    r0 = tile_start - starts[te]
    cs_t = cs[:, te]
    run_end_t = cs_t + lp[:, te]
    jlo = jnp.sum((run_end_t <= r0[None, :]).astype(jnp.int32), axis=0)
    jhi = jnp.sum((cs_t < (r0 + tm_e)[None, :]).astype(jnp.int32), axis=0)
    valid = jnp.clip(total[te] - r0, 0, tm_e)
    src = jnp.arange(n_tt, dtype=jnp.int32)[:, None] * _local_rows(tm) + off
    i32 = lambda a: a.astype(jnp.int32)
    expert_tabs = (i32(te), n_used, i32(r0), i32(jlo), i32(jhi), i32(valid),
                   i32(cs.T.reshape(-1)), i32(lp.T.reshape(-1)), i32(src.T.reshape(-1)))
    combine_tabs = (i32((starts[None, :] + cs).reshape(-1)), i32(lp.reshape(-1)), i32(off.reshape(-1)))
    return expert_tabs, combine_tabs, n_rows


def _rope_table(positions):
    B, S = positions.shape
    inv_freq = jnp.power(jnp.float32(ROPE_THETA), -jnp.arange(ROPE_HALF, dtype=F32) / ROPE_HALF)
    ang = positions.astype(F32)[..., None] * inv_freq
    cos, sin = jnp.cos(ang), jnp.sin(ang)
    head = jnp.concatenate([cos, jnp.zeros_like(cos), sin, jnp.zeros((B, S, HEAD_DIM - 3 * ROPE_HALF), F32)], axis=-1)
    return jnp.concatenate([head] * (LANES // HEAD_DIM), axis=-1).reshape(B * S, LANES)


def kernel(x, positions, norm_mix, norm_ffn, w_in_ab, fgate_bias, w_out_ab, w_in_c, lower_bounds, gnorm_c, w_out_c,
           w_gate_ffn, w_up_ffn, w_down_ffn, router, w_gate_moe, w_up_moe, w_down_moe, norm_final):
    B, S, D = x.shape
    T = B * S
    assert D == D_MODEL and S % 1024 == 0
    tm = 512
    x2d = x.reshape(T, D)

    w_ab = jnp.pad(w_in_ab[0], ((0, 0), (0, QKV_WIDTH + LANES - w_in_ab.shape[2]))).astype(BF16)
    bias = jnp.pad(fgate_bias[0], (0, LANES - N_HEADS_B)).reshape(1, LANES)
    qkv, cum = _inproj_ab(x2d, norm_mix[0:1], w_ab, _rope_table(positions), bias, tm, S)
    qkv = qkv.reshape(B, S, QKV_WIDTH)
    out_a = _dilated_attention(qkv, B, S).reshape(T, WIDTH_A)
    out_b = _fox_attention(qkv, cum.reshape(B, S, LANES), B, S).reshape(T, WIDTH_B)
    w_o = w_out_ab[0].astype(BF16)
    x2 = _outproj_ffn(x2d, out_a, out_b, w_o[:WIDTH_A], w_o[WIDTH_A:], norm_ffn[0:1],
                      w_gate_ffn[0].astype(BF16), w_up_ffn[0].astype(BF16), w_down_ffn[0].astype(BF16), tm)

    lb_all = jnp.cumsum(jax.nn.softmax(lower_bounds.astype(F32), axis=0), axis=0)
    lb = (lb_all - lb_all[0:1])[1].reshape(1, D)
    wq, wf, wi, wg = jnp.split(w_in_c[0], 4, axis=-1)
    w_c = jnp.concatenate([wq, wi, wg, wf], axis=-1).astype(BF16)
    qig, flog_c = _inproj_c(x2, norm_mix[1:2], w_c, tm)
    o_c = _hgrn(qig.reshape(B, S, 3 * D), flog_c.reshape(B, S, D), lb, gnorm_c[0:1], B, S).reshape(T, D)

    r_pad = jnp.pad(router[0], ((0, 0), (0, LANES - N_EXPERTS))).astype(BF16)
    x3, xl, pos, gates, seg = _outproj_router(x2, o_c, w_out_c[0].astype(BF16), norm_ffn[1:2], r_pad, tm)
    tm_e = 1024
    expert_tabs, combine_tabs, n_rows = _routing_tables(seg, T // tm, tm, tm_e)
    ys = _moe_experts(expert_tabs, xl, w_gate_moe[0].astype(BF16), w_up_moe[0].astype(BF16),
                      w_down_moe[0].astype(BF16), n_rows, T // tm, tm_e, w_gate_moe.shape[3] // 2)
    out = _moe_combine(combine_tabs, x3, pos, gates, ys, norm_final.reshape(1, D), tm)
    return out.reshape(B, S, D)
```

```python
import functools

import jax
import jax.numpy as jnp
import numpy as np
from jax import lax
from jax.experimental import pallas as pl
from jax.experimental.pallas import tpu as pltpu

F32 = jnp.float32
BF16 = jnp.bfloat16

D_MODEL = 1024
HEAD_DIM = 64
N_HEADS_A = 8
N_HEADS_B = 8
WIDTH_A = N_HEADS_A * HEAD_DIM
WIDTH_B = N_HEADS_B * HEAD_DIM
QKV_WIDTH = 3 * (WIDTH_A + WIDTH_B)
ROPE_THETA = 500000.0
ROPE_DIM = HEAD_DIM // 4
ROPE_HALF = ROPE_DIM // 2
ATT_BLOCK = 128
DILATIONS = (1, 4, 16)
N_HEADS_C = 8
HGRN_DK = 128
HGRN_CHUNK = 64
HGRN_LEAF = 16
N_EXPERTS = 8
TOP_K = 2
EPS = 1e-6
LOG2E = 1.4426950408889634

LANES = 128
SUBLANES = 8
VMEM_LIMIT = 56 * 1024 * 1024

NT_DIMS = (((1,), (1,)), ((), ()))


def _params(semantics, **kw):
    return pltpu.CompilerParams(dimension_semantics=semantics, vmem_limit_bytes=VMEM_LIMIT, **kw)


def _rmsnorm(x, gain):
    return x * lax.rsqrt(jnp.mean(x * x, axis=-1, keepdims=True) + EPS) * gain


def _sigmoid(x):
    return 1.0 / (1.0 + jnp.exp(-x))


def _split3(x):
    hi = x.astype(BF16)
    r1 = x - hi.astype(F32)
    mid = r1.astype(BF16)
    lo = (r1 - mid.astype(F32)).astype(BF16)
    return hi, mid, lo


def _dot(a, b):
    return jnp.dot(a, b, preferred_element_type=F32)


def _dot_nt(a, b):
    return lax.dot_general(a, b, NT_DIMS, preferred_element_type=F32)


def _cumsum_groups(x):
    n, w = x.shape
    rows = lax.broadcasted_iota(jnp.int32, (SUBLANES, w), 0)
    out, carry = [], None
    for g in range(n // SUBLANES):
        xg = x[SUBLANES * g:SUBLANES * (g + 1), :]
        for s in (1, 2, 4):
            xg = xg + jnp.where(rows >= s, pltpu.roll(xg, s, 0), 0.0)
        if carry is not None:
            xg = xg + carry
        carry = xg[SUBLANES - 1:SUBLANES, :]
        out.append(xg)
    return jnp.concatenate(out, axis=0)


def _inproj_ab_body(x_ref, gain_ref, w_ref, cos_ref, sin_ref, bias_ref, qkv_ref, cum_ref, carry_scr, *, tiles_per_seq):
    h = _rmsnorm(x_ref[...], gain_ref[...]).astype(BF16)

    @pl.when(pl.program_id(0) % tiles_per_seq == 0)
    def _():
        carry_scr[...] = jnp.zeros_like(carry_scr)

    x = _dot(h, w_ref[:, QKV_WIDTH:QKV_WIDTH + LANES]) + bias_ref[...]
    logf = -(jnp.maximum(-x, 0.0) + jnp.log1p(jnp.exp(-jnp.abs(x))))
    cum = _cumsum_groups(logf * LOG2E) + carry_scr[0:1, :]
    cum_ref[...] = cum
    carry_scr[0:1, :] = cum[cum.shape[0] - 1:, :]

    cos = cos_ref[...]
    sin = sin_ref[...]
    lane = lax.broadcasted_iota(jnp.int32, cos.shape, 1)
    low = (lane & (HEAD_DIM - 1)) < ROPE_HALF
    scale = HEAD_DIM ** -0.5 * LOG2E
    for c in range(QKV_WIDTH // 256):
        y = _dot(h, w_ref[:, c * 256:(c + 1) * 256])
        seg = c // 2
        for s in range(2):
            yy = y[:, s * LANES:(s + 1) * LANES]
            if seg in (0, 1):
                partner = jnp.where(low, pltpu.roll(yy, LANES - ROPE_HALF, 1), pltpu.roll(yy, ROPE_HALF, 1))
                yy = yy * cos + partner * sin
            if seg in (0, 3):
                yy = yy * scale
            qkv_ref[:, c * 256 + s * LANES:c * 256 + (s + 1) * LANES] = yy.astype(BF16)


def _inproj_ab(x2d, gain, w, cos, sin, bias, tm, seq_len):
    T = x2d.shape[0]
    wn = w.shape[1]
    return pl.pallas_call(
        functools.partial(_inproj_ab_body, tiles_per_seq=seq_len // tm),
        grid=(T // tm,),
        in_specs=[
            pl.BlockSpec((tm, D_MODEL), lambda i: (i, 0)),
            pl.BlockSpec((1, D_MODEL), lambda i: (0, 0)),
            pl.BlockSpec((D_MODEL, wn), lambda i: (0, 0)),
            pl.BlockSpec((tm, LANES), lambda i: (i, 0)),
            pl.BlockSpec((tm, LANES), lambda i: (i, 0)),
            pl.BlockSpec((1, LANES), lambda i: (0, 0)),
        ],
        out_specs=[
            pl.BlockSpec((tm, QKV_WIDTH), lambda i: (i, 0)),
            pl.BlockSpec((tm, LANES), lambda i: (i, 0)),
        ],
        out_shape=[
            jax.ShapeDtypeStruct((T, QKV_WIDTH), BF16),
            jax.ShapeDtypeStruct((T, LANES), F32),
        ],
        scratch_shapes=[pltpu.VMEM((SUBLANES, LANES), F32)],
        compiler_params=_params(("arbitrary",)),
        name="inproj_ab",
    )(x2d, gain, w, cos, sin, bias)


def _dilated_body(q_ref, k_ref, v_ref, o_ref, qf, kf, vf, ob, lb):
    S = q_ref.shape[0]
    nb = ATT_BLOCK
    qf[...] = q_ref[...].astype(F32)
    kf[...] = k_ref[...].astype(F32)
    vf[...] = v_ref[...].astype(F32)
    head0 = lax.broadcasted_iota(jnp.int32, (nb, LANES), 1) < HEAD_DIM
    qi2 = lax.broadcasted_iota(jnp.int32, (2 * nb, 2 * nb), 0) & (nb - 1)
    kj2 = lax.broadcasted_iota(jnp.int32, (2 * nb, 2 * nb), 1)
    valid2 = (kj2 >= qi2) & (kj2 <= qi2 + nb)
    qi1 = lax.broadcasted_iota(jnp.int32, (2 * nb, nb), 0) & (nb - 1)
    kj1 = lax.broadcasted_iota(jnp.int32, (2 * nb, nb), 1)
    valid1 = kj1 <= qi1

    def rows(start, size, r):
        return pl.ds(start, size) if r == 1 else pl.ds(start, size, stride=r)

    def block(br, r, q0, k0, nk):
        qs = qf[rows(q0, nb, r), :]
        kb = kf[rows(k0, nk, r), :].astype(BF16)
        vb = vf[rows(k0, nk, r), :].astype(BF16)
        q2 = jnp.concatenate([jnp.where(head0, qs, 0.0), jnp.where(head0, 0.0, qs)], axis=0).astype(BF16)
        s = jnp.where(valid2 if nk == 2 * nb else valid1, _dot_nt(q2, kb), -jnp.inf)
        m = jnp.max(s, axis=-1, keepdims=True)
        e = jnp.exp2(s - m)
        l = jnp.sum(e, axis=-1, keepdims=True)
        o = _dot(e.astype(BF16), vb) / l
        lse = jnp.broadcast_to(m + jnp.log2(l), (2 * nb, LANES))
        ob[br, rows(q0, nb, r), :] = jnp.where(head0, o[:nb], o[nb:])
        lb[br, rows(q0, nb, r), :] = jnp.where(head0, lse[:nb], lse[nb:])

    for br, r in enumerate(DILATIONS):
        n_blocks = S // (r * nb)
        if n_blocks == 1:
            def first_only(c, carry, br=br, r=r):
                block(br, r, c, c, nb)
                return carry
            lax.fori_loop(0, r, first_only, 0, unroll=True)
        else:
            for c in range(r):
                block(br, r, c, c, nb)

                def later(n, carry, br=br, r=r, c=c):
                    block(br, r, n * (nb * r) + c, (n - 1) * (nb * r) + c, 2 * nb)
                    return carry
                lax.fori_loop(1, n_blocks, later, 0, unroll=True)

    rows_per = 256
    for ch in range(S // rows_per):
        sl = pl.ds(ch * rows_per, rows_per)
        l0, l1, l2 = lb[0, sl, :], lb[1, sl, :], lb[2, sl, :]
        m = jnp.maximum(jnp.maximum(l0, l1), l2)
        w0, w1, w2 = jnp.exp2(l0 - m), jnp.exp2(l1 - m), jnp.exp2(l2 - m)
        o = (w0 * ob[0, sl, :] + w1 * ob[1, sl, :] + w2 * ob[2, sl, :]) / (w0 + w1 + w2)
        o_ref[sl, :] = o.astype(BF16)


def _dilated_attention(qkv, B, S):
    n_pairs = WIDTH_A // LANES
    blk = lambda off: pl.BlockSpec((None, S, LANES), lambda b, p: (b, 0, off + p))
    return pl.pallas_call(
        _dilated_body,
        grid=(B, n_pairs),
        in_specs=[blk(0), blk(n_pairs), blk(2 * n_pairs)],
        out_specs=pl.BlockSpec((None, S, LANES), lambda b, p: (b, 0, p)),
        out_shape=jax.ShapeDtypeStruct((B, S, WIDTH_A), BF16),
        scratch_shapes=[
            pltpu.VMEM((S, LANES), F32), pltpu.VMEM((S, LANES), F32), pltpu.VMEM((S, LANES), F32),
            pltpu.VMEM((len(DILATIONS), S, LANES), F32), pltpu.VMEM((len(DILATIONS), S, LANES), F32),
        ],
        compiler_params=_params(("parallel", "parallel")),
        name="dilated_attention",
    )(qkv, qkv, qkv)


def _fox_body(q_ref, k_ref, v_ref, c_ref, sel_ref, o_ref, qa_scr, ka_scr, *, tk):
    S = q_ref.shape[0]
    rows_per = 256
    lane = lax.broadcasted_iota(jnp.int32, (rows_per, LANES), 1)

    def build(i, carry):
        sl = pl.ds(i * rows_per, rows_per)
        q = q_ref[sl, :].astype(F32)
        k = k_ref[sl, :].astype(F32)
        extra = _dot(jnp.concatenate(_split3(c_ref[sl, :]), axis=1), sel_ref[...])
        for hh in range(2):
            own = (lane < HEAD_DIM) if hh == 0 else (lane >= HEAD_DIM)
            a0 = HEAD_DIM if hh == 0 else 0
            first = (lane >= a0) & (lane < a0 + 3)
            second = (lane >= a0 + 3) & (lane < a0 + 6)
            ex = extra[:, hh * LANES:(hh + 1) * LANES]
            qaug = jnp.where(own, q, jnp.where(first, ex, jnp.where(second, 1.0, 0.0)))
            kaug = jnp.where(own, k, jnp.where(second, ex, jnp.where(first, 1.0, 0.0)))
            qa_scr[hh, sl, :] = qaug.astype(BF16)
            ka_scr[hh, sl, :] = kaug.astype(BF16)
        return carry

    lax.fori_loop(0, S // rows_per, build, 0, unroll=True)

    n_blocks = S // tk
    row_t = lax.broadcasted_iota(jnp.int32, (tk, tk), 0)
    col_t = lax.broadcasted_iota(jnp.int32, (tk, tk), 1)
    causal = col_t <= row_t
    head0 = lax.broadcasted_iota(jnp.int32, (tk, LANES), 1) < HEAD_DIM
    state = [[None] * n_blocks for _ in range(2)]
    for j in range(n_blocks):
        ksl = pl.ds(j * tk, tk)
        vblk = v_ref[ksl, :]
        for hh in range(2):
            s_all = _dot_nt(qa_scr[hh, pl.ds(j * tk, S - j * tk), :], ka_scr[hh, ksl, :])
            es, scales = [], []
            for rb in range(j, n_blocks):
                s = s_all[(rb - j) * tk:(rb - j + 1) * tk, :]
                if rb == j:
                    s = jnp.where(causal, s, -jnp.inf)
                if j == 0:
                    m_new = jnp.max(s, axis=-1, keepdims=True)
                    alpha = None
                else:
                    m_old = state[hh][rb][0]
                    m_new = jnp.maximum(m_old, jnp.max(s, axis=-1, keepdims=True))
                    alpha = jnp.exp2(m_old - m_new)
                e = jnp.exp2(s - m_new)
                es.append(e.astype(BF16))
                scales.append((m_new, alpha, jnp.sum(e, axis=-1, keepdims=True)))
            pv_all = _dot(jnp.concatenate(es, axis=0) if len(es) > 1 else es[0], vblk)
            for rb in range(j, n_blocks):
                m_new, alpha, rowsum = scales[rb - j]
                pv = pv_all[(rb - j) * tk:(rb - j + 1) * tk, :]
                if alpha is None:
                    state[hh][rb] = (m_new, rowsum, pv)
                else:
                    _, l_old, acc_old = state[hh][rb]
                    state[hh][rb] = (m_new, alpha * l_old + rowsum, alpha * acc_old + pv)
        outs = [state[hh][j][2] / state[hh][j][1] for hh in range(2)]
        o_ref[pl.ds(j * tk, tk), :] = jnp.where(head0, outs[0], outs[1]).astype(BF16)


def _fox_selection(n_pairs):
    sel = np.zeros((n_pairs, 3 * LANES, 2 * LANES), np.float32)
    for p in range(n_pairs):
        for hh in range(2):
            a0 = hh * LANES + (HEAD_DIM if hh == 0 else 0)
            for piece in range(3):
                sel[p, piece * LANES + 2 * p + hh, a0 + piece] = 1.0
                sel[p, piece * LANES + 2 * p + hh, a0 + 3 + piece] = -1.0
    return jnp.asarray(sel, BF16)


def _fox_attention(qkv, cum, B, S, tk=256):
    n_pairs = WIDTH_B // LANES
    base = 3 * (WIDTH_A // LANES)
    blk = lambda off: pl.BlockSpec((None, S, LANES), lambda b, p: (b, 0, off + p))
    return pl.pallas_call(
        functools.partial(_fox_body, tk=tk),
        grid=(B, n_pairs),
        in_specs=[
            blk(base), blk(base + n_pairs), blk(base + 2 * n_pairs),
            pl.BlockSpec((None, S, LANES), lambda b, p: (b, 0, 0)),
            pl.BlockSpec((None, 3 * LANES, 2 * LANES), lambda b, p: (p, 0, 0)),
        ],
        out_specs=pl.BlockSpec((None, S, LANES), lambda b, p: (b, 0, p)),
        out_shape=jax.ShapeDtypeStruct((B, S, WIDTH_B), BF16),
        scratch_shapes=[pltpu.VMEM((2, S, LANES), BF16), pltpu.VMEM((2, S, LANES), BF16)],
        compiler_params=_params(("parallel", "parallel")),
        name="fox_attention",
    )(qkv, qkv, qkv, cum, _fox_selection(n_pairs))


def _outproj_ffn_body(x_ref, oa_ref, ob_ref, woa_ref, wob_ref, gain_ref, wg_ref, wu_ref, wd_ref, out_ref, a_scr):
    x1 = x_ref[...] + _dot(oa_ref[...], woa_ref[...]) + _dot(ob_ref[...], wob_ref[...])
    h = _rmsnorm(x1, gain_ref[...]).astype(BF16)
    for c in range(a_scr.shape[1] // 256):
        cols = slice(c * 256, (c + 1) * 256)
        g = _dot(h, wg_ref[:, cols])
        u = _dot(h, wu_ref[:, cols])
        a_scr[:, cols] = (g * _sigmoid(g) * u).astype(BF16)
    out_ref[...] = x1 + _dot(a_scr[...], wd_ref[...])


def _outproj_ffn(x2d, oa, ob, woa, wob, gain, wg, wu, wd, tm):
    T = x2d.shape[0]
    F = wg.shape[1]
    once = dict(pipeline_mode=pl.Buffered(1))
    return pl.pallas_call(
        _outproj_ffn_body,
        grid=(T // tm,),
        in_specs=[
            pl.BlockSpec((tm, D_MODEL), lambda i: (i, 0)),
            pl.BlockSpec((tm, WIDTH_A), lambda i: (i, 0)),
            pl.BlockSpec((tm, WIDTH_B), lambda i: (i, 0)),
            pl.BlockSpec((WIDTH_A, D_MODEL), lambda i: (0, 0), **once),
            pl.BlockSpec((WIDTH_B, D_MODEL), lambda i: (0, 0), **once),
            pl.BlockSpec((1, D_MODEL), lambda i: (0, 0)),
            pl.BlockSpec((D_MODEL, F), lambda i: (0, 0), **once),
            pl.BlockSpec((D_MODEL, F), lambda i: (0, 0), **once),
            pl.BlockSpec((F, D_MODEL), lambda i: (0, 0), **once),
        ],
        out_specs=pl.BlockSpec((tm, D_MODEL), lambda i: (i, 0)),
        out_shape=jax.ShapeDtypeStruct((T, D_MODEL), F32),
        scratch_shapes=[pltpu.VMEM((tm, F), BF16)],
        compiler_params=_params(("parallel",)),
        name="outproj_ffn",
    )(x2d, oa, ob, woa, wob, gain, wg, wu, wd)


def _inproj_c_body(x_ref, gain_ref, w_ref, qig_ref, f_ref):
    h = _rmsnorm(x_ref[...], gain_ref[...]).astype(BF16)
    n_qig = qig_ref.shape[1]
    for c in range(n_qig // 256):
        qig_ref[:, c * 256:(c + 1) * 256] = _dot(h, w_ref[:, c * 256:(c + 1) * 256]).astype(BF16)
    for c in range(f_ref.shape[1] // 256):
        f_ref[:, c * 256:(c + 1) * 256] = _dot(h, w_ref[:, n_qig + c * 256:n_qig + (c + 1) * 256])


def _inproj_c(x2d, gain, w, tm):
    T = x2d.shape[0]
    return pl.pallas_call(
        _inproj_c_body,
        grid=(T // tm,),
        in_specs=[
            pl.BlockSpec((tm, D_MODEL), lambda i: (i, 0)),
            pl.BlockSpec((1, D_MODEL), lambda i: (0, 0)),
            pl.BlockSpec((D_MODEL, 4 * D_MODEL), lambda i: (0, 0)),
        ],
        out_specs=[
            pl.BlockSpec((tm, 3 * D_MODEL), lambda i: (i, 0)),
            pl.BlockSpec((tm, D_MODEL), lambda i: (i, 0)),
        ],
        out_shape=[
            jax.ShapeDtypeStruct((T, 3 * D_MODEL), BF16),
            jax.ShapeDtypeStruct((T, D_MODEL), F32),
        ],
        compiler_params=_params(("parallel",)),
        name="inproj_c",
    )(x2d, gain, w)


def _hgrn_body(q_ref, i_ref, g_ref, f_ref, lb_ref, gn_ref, o_ref, state_scr, o_scr, ops0, ops1, dec0, dec1, *, heads):
    S = q_ref.shape[0]
    C = HGRN_CHUNK
    half = C // 2
    quarter = C // 4
    assert quarter == HGRN_LEAF
    n_chunks = S // C
    row = lax.broadcasted_iota(jnp.int32, (C, C), 0)
    col = lax.broadcasted_iota(jnp.int32, (C, C), 1)
    mask_cross = (row >= half) & (col < half)
    mask_same = ((row // half) == (col // half)) & (col <= row)
    r = lax.broadcasted_iota(jnp.int32, (C, heads * HGRN_DK), 0)
    scale = HGRN_DK ** -0.5
    gn = jnp.concatenate([gn_ref[...]] * heads, axis=1)
    state_scr[...] = jnp.zeros_like(state_scr)

    def prepare(ci, ops, dec):
        sl = pl.ds(ci * C, C)
        lbv = lb_ref[...]
        f = lbv + (1.0 - lbv) * _sigmoid(f_ref[sl, :])
        k = 1.0 - f
        q = q_ref[sl, :].astype(F32) * scale
        b = _cumsum_groups(jnp.log2(f))
        b_last = b[C - 1:C, :]
        e_cross = b - b[half - 1:half, :]
        e_same = b - jnp.where(r < half, b[quarter - 1:quarter, :], b[half + quarter - 1:half + quarter, :])
        ops[0] = (q * jnp.exp2(jnp.minimum(e_cross, 0.0))).astype(BF16)
        ops[1] = (k * jnp.exp2(jnp.minimum(-e_cross, 0.0))).astype(BF16)
        ops[2] = (q * jnp.exp2(e_same)).astype(BF16)
        ops[3] = (k * jnp.exp2(-e_same)).astype(BF16)
        ops[4] = (q * jnp.exp2(b)).astype(BF16)
        ops[5] = (k * jnp.exp2(b_last - b)).astype(BF16)
        dec[0:1, :] = jnp.exp2(b_last)

    def contract(ci, ops, dec):
        sl = pl.ds(ci * C, C)
        for hd in range(heads):
            cols = slice(hd * HGRN_DK, (hd + 1) * HGRN_DK)
            v = i_ref[sl, cols]
            scores = (jnp.where(mask_cross, _dot_nt(ops[0, :, cols], ops[1, :, cols]), 0.0)
                      + jnp.where(mask_same, _dot_nt(ops[2, :, cols], ops[3, :, cols]), 0.0))
            state_t = state_scr[hd]
            o_scr[sl, cols] = _dot(scores.astype(BF16), v) + _dot_nt(ops[4, :, cols], state_t.astype(BF16))
            v_t = v.astype(F32).T.astype(BF16)
            state_scr[hd] = state_t * dec[0:1, cols] + _dot(v_t, ops[5, :, cols])

    def finish(ci):
        sl = pl.ds(ci * C, C)
        o = o_scr[sl, :]
        ys = []
        for hd in range(heads):
            oh = o[:, hd * HGRN_DK:(hd + 1) * HGRN_DK]
            ys.append(oh * lax.rsqrt(jnp.mean(oh * oh, axis=-1, keepdims=True) + EPS))
        gate = g_ref[sl, :].astype(F32)
        o_ref[sl, :] = (jnp.concatenate(ys, axis=1) * gn * (gate * _sigmoid(gate))).astype(BF16)

    prepare(0, ops0, dec0)

    def pair(j, carry):
        prepare(2 * j + 1, ops1, dec1)
        contract(2 * j, ops0, dec0)
        prepare(2 * j + 2, ops0, dec0)
        contract(2 * j + 1, ops1, dec1)
        finish(jnp.maximum(2 * j - 1, 0))
        finish(2 * j)
        return carry

    lax.fori_loop(0, n_chunks // 2 - 1, pair, 0, unroll=5)
    prepare(n_chunks - 1, ops1, dec1)
    contract(n_chunks - 2, ops0, dec0)
    contract(n_chunks - 1, ops1, dec1)
    for ci in range(n_chunks - 3, n_chunks):
        finish(ci)


def _hgrn(qig, flog, lb, gn, B, S, heads=4):
    ng = N_HEADS_C // heads
    w = heads * HGRN_DK
    blk = lambda off: pl.BlockSpec((None, S, w), lambda b, h: (b, 0, off + h))
    return pl.pallas_call(
        functools.partial(_hgrn_body, heads=heads),
        grid=(B, ng),
        in_specs=[
            blk(0), blk(ng), blk(2 * ng),
            pl.BlockSpec((None, S, w), lambda b, h: (b, 0, h)),
            pl.BlockSpec((1, w), lambda b, h: (0, h)),
            pl.BlockSpec((1, HGRN_DK), lambda b, h: (0, 0)),
        ],
        out_specs=pl.BlockSpec((None, S, w), lambda b, h: (b, 0, h)),
        out_shape=jax.ShapeDtypeStruct((B, S, D_MODEL), BF16),
        scratch_shapes=[
            pltpu.VMEM((heads, HGRN_DK, HGRN_DK), F32), pltpu.VMEM((S, w), F32),
            pltpu.VMEM((6, HGRN_CHUNK, w), BF16), pltpu.VMEM((6, HGRN_CHUNK, w), BF16),
            pltpu.VMEM((SUBLANES, w), F32), pltpu.VMEM((SUBLANES, w), F32),
        ],
        compiler_params=_params(("parallel", "parallel")),
        name="hgrn2",
    )(qig, qig, qig, flog, lb, gn)


SEG_ALIGN = SUBLANES
SEG_BITS = tuple(range(9, 2, -1))


def _local_rows(tm):
    return TOP_K * tm + N_EXPERTS * SEG_ALIGN


def _outproj_router_body(x_ref, o_ref, w_ref, gain_ref, r_ref, x3_ref, xl_ref, pos_ref, gate_ref, seg_ref, *, tm):
    for t in range(x_ref.shape[0] // tm):
        _route_tile(x_ref, o_ref, w_ref, gain_ref, r_ref, x3_ref, xl_ref, pos_ref, gate_ref, seg_ref, t, tm)


def _route_tile(x_ref, o_ref, w_ref, gain_ref, r_ref, x3_ref, xl_ref, pos_ref, gate_ref, seg_ref, t, tm):
    rows = pl.ds(t * tm, tm)
    lr = _local_rows(tm)
    x3 = x_ref[rows, :] + _dot(o_ref[rows, :], w_ref[...])
    x3_ref[rows, :] = x3
    h = _rmsnorm(x3, gain_ref[...])
    h_bf = h.astype(BF16)
    logits = _dot(h_bf, r_ref[...])
    lane = lax.broadcasted_iota(jnp.int32, logits.shape, 1)
    lane_f = lane.astype(F32)
    lg = jnp.where(lane < N_EXPERTS, logits, -jnp.inf)
    m1 = jnp.max(lg, axis=-1, keepdims=True)
    i1 = jnp.min(jnp.where(lg == m1, lane_f, float(LANES)), axis=-1, keepdims=True)
    lg2 = jnp.where(lane_f == i1, -jnp.inf, lg)
    m2 = jnp.max(lg2, axis=-1, keepdims=True)
    i2 = jnp.min(jnp.where(lg2 == m2, lane_f, float(LANES)), axis=-1, keepdims=True)
    e2 = jnp.exp(m2 - m1)
    den = 1.0 + e2
    gate_ref[rows, :] = jnp.where(lane == 0, 1.0 / den, jnp.where(lane == 1, e2 / den, 0.0))

    oh1 = (lane_f == i1).astype(F32)
    oh2 = (lane_f == i2).astype(F32)
    c1 = _cumsum_groups(oh1)
    c2 = _cumsum_groups(oh2)
    n1 = c1[tm - 1:tm, :]
    count = n1 + c2[tm - 1:tm, :]
    padded = jnp.floor((count + (SEG_ALIGN - 1.0)) * (1.0 / SEG_ALIGN)) * SEG_ALIGN
    run = jnp.broadcast_to(padded, (SUBLANES, LANES))
    lane8 = lax.broadcasted_iota(jnp.int32, (SUBLANES, LANES), 1)
    for s in (1, 2, 4):
        run = run + jnp.where(lane8 >= s, pltpu.roll(run, s, 1), 0.0)
    start = run[0:1, :] - padded
    pos1 = jnp.sum(oh1 * (start + c1 - 1.0), axis=-1, keepdims=True)
    pos2 = jnp.sum(oh2 * (start + n1 + c2 - 1.0), axis=-1, keepdims=True)
    pos_ref[rows, :] = jnp.where(lane == 0, pos1, jnp.where(lane == 1, pos2, 0.0)).astype(jnp.int32)
    pos1_row = jnp.broadcast_to(pos1, (tm, LANES)).T[0:1, :]
    pos2_row = jnp.broadcast_to(pos2, (tm, LANES)).T[0:1, :]
    slot = lax.broadcasted_iota(jnp.int32, (lr, tm), 0).astype(F32)
    perm = ((slot == pos1_row) | (slot == pos2_row)).astype(BF16)
    xl_ref[pl.ds(t * lr, lr), :] = _dot(perm, h_bf)
    r8 = lax.broadcasted_iota(jnp.int32, (SUBLANES, LANES), 0)
    seg_ref[pl.ds(t * SUBLANES, SUBLANES), :] = jnp.where(r8 == 0, count, jnp.where(r8 == 1, padded, jnp.where(r8 == 2, start, 0.0))).astype(jnp.int32)


def _outproj_router(x2d, o, w, gain, r, tm, group=2):
    T = x2d.shape[0]
    n_tt = T // tm
    lr = _local_rows(tm)
    tg = group * tm
    return pl.pallas_call(
        functools.partial(_outproj_router_body, tm=tm),
        grid=(n_tt // group,),
        in_specs=[
            pl.BlockSpec((tg, D_MODEL), lambda i: (i, 0)),
            pl.BlockSpec((tg, D_MODEL), lambda i: (i, 0)),
            pl.BlockSpec((D_MODEL, D_MODEL), lambda i: (0, 0)),
            pl.BlockSpec((1, D_MODEL), lambda i: (0, 0)),
            pl.BlockSpec((D_MODEL, LANES), lambda i: (0, 0)),
        ],
        out_specs=[
            pl.BlockSpec((tg, D_MODEL), lambda i: (i, 0)),
            pl.BlockSpec((group * lr, D_MODEL), lambda i: (i, 0)),
            pl.BlockSpec((tg, LANES), lambda i: (i, 0)),
            pl.BlockSpec((tg, LANES), lambda i: (i, 0)),
            pl.BlockSpec((group * SUBLANES, LANES), lambda i: (i, 0)),
        ],
        out_shape=[
            jax.ShapeDtypeStruct((T, D_MODEL), F32),
            jax.ShapeDtypeStruct((n_tt * lr, D_MODEL), F32),
            jax.ShapeDtypeStruct((T, LANES), jnp.int32),
            jax.ShapeDtypeStruct((T, LANES), F32),
            jax.ShapeDtypeStruct((n_tt * SUBLANES, LANES), jnp.int32),
        ],
        compiler_params=_params(("parallel",)),
        name="outproj_router",
    )(x2d, o, w, gain, r)


def _run_copies(n, src, s0, dst, d0, sem):
    out = []
    for b in SEG_BITS:
        offs = (n >> (b + 1)) << (b + 1)
        cp = pltpu.make_async_copy(src.at[pl.ds(pl.multiple_of(s0 + offs, SEG_ALIGN), 1 << b), :],
                                   dst.at[pl.ds(pl.multiple_of(d0 + offs, SEG_ALIGN), 1 << b), :], sem)
        out.append((((n >> b) & 1) == 1, cp))
    return out


def _experts_body(te_ref, nu_ref, r0_ref, jlo_ref, jhi_ref, valid_ref, cs_ref, lp_ref, src_ref,
                  xl_hbm, wg_ref, wu_ref, wd_ref, out_ref, xbuf, xb_scr, a_scr, sem, *, n_tt):
    i = pl.program_id(0)
    f = pl.program_id(1)
    tm = out_ref.shape[0]

    def move(tile, slot, wait):
        base = te_ref[tile] * n_tt
        r0 = r0_ref[tile]

        def one_run(j, carry):
            c0 = cs_ref[base + j]
            lo = jnp.maximum(c0, r0)
            hi = jnp.minimum(c0 + lp_ref[base + j], r0 + tm)
            n = jnp.maximum(hi - lo, 0)
            for cond, cp in _run_copies(n, xl_hbm, src_ref[base + j] + (lo - c0), xbuf.at[slot], lo - r0, sem.at[slot]):
                @pl.when(cond)
                def _():
                    cp.wait() if wait else cp.start()
            return carry

        lax.fori_loop(jlo_ref[tile], jhi_ref[tile], one_run, 0)

    @pl.when(f == 0)
    def _():
        @pl.when(i == 0)
        def _():
            xbuf[...] = jnp.zeros_like(xbuf)
            move(0, 0, False)

        @pl.when(i < nu_ref[0])
        def _():
            move(i, i % 2, True)

        @pl.when(i + 1 < nu_ref[0])
        def _():
            move(i + 1, (i + 1) % 2, False)

        row = lax.broadcasted_iota(jnp.int32, (tm, D_MODEL), 0)
        xb_scr[...] = jnp.where(row < valid_ref[i], xbuf[i % 2], 0.0).astype(BF16)
        out_ref[...] = jnp.zeros_like(out_ref)

    @pl.when(i < nu_ref[0])
    def _():
        xb = xb_scr[...]
        for c in range(a_scr.shape[1] // 256):
            cols = slice(c * 256, (c + 1) * 256)
            g = _dot(xb, wg_ref[:, cols])
            u = _dot(xb, wu_ref[:, cols])
            a_scr[:, cols] = (g * _sigmoid(g) * u).astype(BF16)
        out_ref[...] += _dot(a_scr[...], wd_ref[...])


def _moe_experts(tabs, xl, wg, wu, wd, n_rows, n_tt, tm, tf):
    F = wg.shape[2]
    nf = F // tf

    def f_eff(i, f, nu):
        return jnp.where(i < nu[0], f, nf - 1)

    return pl.pallas_call(
        functools.partial(_experts_body, n_tt=n_tt),
        grid_spec=pltpu.PrefetchScalarGridSpec(
            num_scalar_prefetch=len(tabs),
            grid=(n_rows // tm, nf),
            in_specs=[
                pl.BlockSpec(memory_space=pl.ANY),
                pl.BlockSpec((None, D_MODEL, tf), lambda i, f, te, nu, *_: (te[i], 0, f_eff(i, f, nu))),
                pl.BlockSpec((None, D_MODEL, tf), lambda i, f, te, nu, *_: (te[i], 0, f_eff(i, f, nu))),
                pl.BlockSpec((None, tf, D_MODEL), lambda i, f, te, nu, *_: (te[i], f_eff(i, f, nu), 0)),
            ],
            out_specs=pl.BlockSpec((tm, D_MODEL), lambda i, f, *_: (i, 0)),
            scratch_shapes=[pltpu.VMEM((2, tm, D_MODEL), F32), pltpu.VMEM((tm, D_MODEL), BF16),
                            pltpu.VMEM((tm, tf), BF16), pltpu.SemaphoreType.DMA((2,))],
        ),
        out_shape=jax.ShapeDtypeStruct((n_rows, D_MODEL), F32),
        compiler_params=_params(("arbitrary", "arbitrary")),
        name="moe_experts",
    )(*tabs, xl, wg, wu, wd)


def _combine_body(row_ref, lp_ref, off_ref, x_ref, pos_ref, gate_ref, ys_hbm, gain_ref, out_ref, yl, sem):
    j = pl.program_id(0)
    tm = x_ref.shape[0]
    lr = yl.shape[1]

    def move(tile, slot, wait):
        for e in range(N_EXPERTS):
            t = tile * N_EXPERTS + e
            for cond, cp in _run_copies(lp_ref[t], ys_hbm, row_ref[t], yl.at[slot], off_ref[t], sem.at[slot]):
                @pl.when(cond)
                def _():
                    cp.wait() if wait else cp.start()

    @pl.when(j == 0)
    def _():
        yl[...] = jnp.zeros_like(yl)
        move(0, 0, False)

    move(j, j % 2, True)

    @pl.when(j + 1 < pl.num_programs(0))
    def _():
        move(j + 1, (j + 1) % 2, False)

    last = j * N_EXPERTS + N_EXPERTS - 1
    used = off_ref[last] + lp_ref[last]
    row = lax.broadcasted_iota(jnp.int32, (lr, D_MODEL), 0)
    y_sorted = jnp.where(row < used, yl[j % 2], 0.0).astype(BF16)
    pos = pos_ref[...]
    gates = gate_ref[...]
    slot = lax.broadcasted_iota(jnp.int32, (tm, lr), 1)
    weights = jnp.where(slot == pos[:, 0:1], gates[:, 0:1], jnp.where(slot == pos[:, 1:2], gates[:, 1:2], 0.0))
    y = x_ref[...] + _dot(weights.astype(BF16), y_sorted)
    out_ref[...] = _rmsnorm(y, gain_ref[...])


def _moe_combine(tabs, x3, pos, gates, ys, gain, tm):
    T = x3.shape[0]
    return pl.pallas_call(
        _combine_body,
        grid_spec=pltpu.PrefetchScalarGridSpec(
            num_scalar_prefetch=len(tabs),
            grid=(T // tm,),
            in_specs=[
                pl.BlockSpec((tm, D_MODEL), lambda i, *_: (i, 0)),
                pl.BlockSpec((tm, LANES), lambda i, *_: (i, 0)),
                pl.BlockSpec((tm, LANES), lambda i, *_: (i, 0)),
                pl.BlockSpec(memory_space=pl.ANY),
                pl.BlockSpec((1, D_MODEL), lambda i, *_: (0, 0)),
            ],
            out_specs=pl.BlockSpec((tm, D_MODEL), lambda i, *_: (i, 0)),
            scratch_shapes=[pltpu.VMEM((2, _local_rows(tm), D_MODEL), F32), pltpu.SemaphoreType.DMA((2,))],
        ),
        out_shape=jax.ShapeDtypeStruct((T, D_MODEL), F32),
        compiler_params=_params(("arbitrary",)),
        name="moe_combine",
    )(*tabs, x3, pos, gates, ys, gain)


def _routing_tables(seg, n_tt, tm, tm_e):
    seg = seg.reshape(n_tt, SUBLANES, LANES)
    lp = seg[:, 1, :N_EXPERTS]
    off = seg[:, 2, :N_EXPERTS]
    cs = jnp.cumsum(lp, axis=0) - lp
    total = jnp.sum(lp, axis=0)
    padded = ((total + tm_e - 1) // tm_e) * tm_e
    ends = jnp.cumsum(padded)
    starts = ends - padded
    n_rows = -(-(TOP_K * n_tt * tm + N_EXPERTS * (SEG_ALIGN - 1) * n_tt) // tm_e) * tm_e + N_EXPERTS * tm_e
    tile_start = jnp.arange(n_rows // tm_e, dtype=jnp.int32) * tm_e
    te = jnp.minimum(jnp.sum((tile_start[:, None] >= ends[None, :]).astype(jnp.int32), axis=1), N_EXPERTS - 1)
    n_used = (ends[-1] // tm_e).astype(jnp.int32).reshape(1)
    r0 = tile_start - starts[te]
    cs_t = cs[:, te]
    run_end_t = cs_t + lp[:, te]
    jlo = jnp.sum((run_end_t <= r0[None, :]).astype(jnp.int32), axis=0)
    jhi = jnp.sum((cs_t < (r0 + tm_e)[None, :]).astype(jnp.int32), axis=0)
    valid = jnp.clip(total[te] - r0, 0, tm_e)
    src = jnp.arange(n_tt, dtype=jnp.int32)[:, None] * _local_rows(tm) + off
    i32 = lambda a: a.astype(jnp.int32)
    expert_tabs = (i32(te), n_used, i32(r0), i32(jlo), i32(jhi), i32(valid),
                   i32(cs.T.reshape(-1)), i32(lp.T.reshape(-1)), i32(src.T.reshape(-1)))
    combine_tabs = (i32((starts[None, :] + cs).reshape(-1)), i32(lp.reshape(-1)), i32(off.reshape(-1)))
    return expert_tabs, combine_tabs, n_rows


def _rope_tables(positions):
    B, S = positions.shape
    inv_freq = jnp.power(jnp.float32(ROPE_THETA), -jnp.arange(ROPE_HALF, dtype=F32) / ROPE_HALF)
    ang = positions.astype(F32)[..., None] * inv_freq
    cos, sin = jnp.cos(ang), jnp.sin(ang)
    rest = HEAD_DIM - ROPE_DIM
    cos_h = jnp.concatenate([cos, cos, jnp.ones((B, S, rest), F32)], axis=-1)
    sin_h = jnp.concatenate([-sin, sin, jnp.zeros((B, S, rest), F32)], axis=-1)
    reps = LANES // HEAD_DIM
    return (jnp.tile(cos_h, (1, 1, reps)).reshape(B * S, LANES),
            jnp.tile(sin_h, (1, 1, reps)).reshape(B * S, LANES))


def kernel(x, positions, norm_mix, norm_ffn, w_in_ab, fgate_bias, w_out_ab, w_in_c, lower_bounds, gnorm_c, w_out_c,
           w_gate_ffn, w_up_ffn, w_down_ffn, router, w_gate_moe, w_up_moe, w_down_moe, norm_final):
    B, S, D = x.shape
    T = B * S
    assert D == D_MODEL and S % 1024 == 0
    tm = 512
    x2d = x.reshape(T, D)

    w_ab = jnp.pad(w_in_ab[0], ((0, 0), (0, QKV_WIDTH + LANES - w_in_ab.shape[2]))).astype(BF16)
    cos, sin = _rope_tables(positions)
    bias = jnp.pad(fgate_bias[0], (0, LANES - N_HEADS_B)).reshape(1, LANES)
    qkv, cum = _inproj_ab(x2d, norm_mix[0:1], w_ab, cos, sin, bias, tm, S)
    qkv = qkv.reshape(B, S, QKV_WIDTH)
    out_a = _dilated_attention(qkv, B, S).reshape(T, WIDTH_A)
    out_b = _fox_attention(qkv, cum.reshape(B, S, LANES), B, S).reshape(T, WIDTH_B)
    w_o = w_out_ab[0].astype(BF16)
    x2 = _outproj_ffn(x2d, out_a, out_b, w_o[:WIDTH_A], w_o[WIDTH_A:], norm_ffn[0:1],
                      w_gate_ffn[0].astype(BF16), w_up_ffn[0].astype(BF16), w_down_ffn[0].astype(BF16), tm)

    lb_all = jnp.cumsum(jax.nn.softmax(lower_bounds.astype(F32), axis=0), axis=0)
    lb = (lb_all - lb_all[0:1])[1].reshape(1, D)
    wq, wf, wi, wg = jnp.split(w_in_c[0], 4, axis=-1)
    w_c = jnp.concatenate([wq, wi, wg, wf], axis=-1).astype(BF16)
    qig, flog_c = _inproj_c(x2, norm_mix[1:2], w_c, tm)
    o_c = _hgrn(qig.reshape(B, S, 3 * D), flog_c.reshape(B, S, D), lb, gnorm_c[0:1], B, S).reshape(T, D)

    r_pad = jnp.pad(router[0], ((0, 0), (0, LANES - N_EXPERTS))).astype(BF16)
    x3, xl, pos, gates, seg = _outproj_router(x2, o_c, w_out_c[0].astype(BF16), norm_ffn[1:2], r_pad, tm)
    tm_e = 1024
    expert_tabs, combine_tabs, n_rows = _routing_tables(seg, T // tm, tm, tm_e)
    ys = _moe_experts(expert_tabs, xl, w_gate_moe[0].astype(BF16), w_up_moe[0].astype(BF16),
                      w_down_moe[0].astype(BF16), n_rows, T // tm, tm_e, w_gate_moe.shape[3] // 2)
    out = _moe_combine(combine_tabs, x3, pos, gates, ys, norm_final.reshape(1, D), tm)
    return out.reshape(B, S, D)
```

```python
import functools

import jax
import jax.numpy as jnp
import numpy as np
from jax import lax
from jax.experimental import pallas as pl
from jax.experimental.pallas import tpu as pltpu

F32 = jnp.float32
BF16 = jnp.bfloat16

D_MODEL = 1024
HEAD_DIM = 64
N_HEADS_A = 8
N_HEADS_B = 8
WIDTH_A = N_HEADS_A * HEAD_DIM
WIDTH_B = N_HEADS_B * HEAD_DIM
QKV_WIDTH = 3 * (WIDTH_A + WIDTH_B)
ROPE_THETA = 500000.0
ROPE_DIM = HEAD_DIM // 4
ROPE_HALF = ROPE_DIM // 2
ATT_BLOCK = 128
DILATIONS = (1, 4, 16)
N_HEADS_C = 8
HGRN_DK = 128
HGRN_CHUNK = 64
HGRN_LEAF = 16
N_EXPERTS = 8
TOP_K = 2
EPS = 1e-6
LOG2E = 1.4426950408889634

LANES = 128
SUBLANES = 8
VMEM_LIMIT = 56 * 1024 * 1024
PROJ_ROWS = 1024
ROUTE_ROWS = 512
EXPERT_ROWS = 1024

NT_DIMS = (((1,), (1,)), ((), ()))


def _params(semantics, **kw):
    return pltpu.CompilerParams(dimension_semantics=semantics, vmem_limit_bytes=VMEM_LIMIT, **kw)


def _rmsnorm(x, gain):
    return x * lax.rsqrt(jnp.mean(x * x, axis=-1, keepdims=True) + EPS) * gain


def _sigmoid(x):
    return 1.0 / (1.0 + jnp.exp(-x))


def _split3(x):
    hi = x.astype(BF16)
    r1 = x - hi.astype(F32)
    mid = r1.astype(BF16)
    lo = (r1 - mid.astype(F32)).astype(BF16)
    return hi, mid, lo


def _dot(a, b):
    return jnp.dot(a, b, preferred_element_type=F32)


def _dot_nt(a, b):
    return lax.dot_general(a, b, NT_DIMS, preferred_element_type=F32)


def _cumsum_groups(x):
    n, w = x.shape
    rows = lax.broadcasted_iota(jnp.int32, (SUBLANES, w), 0)
    out, carry = [], None
    for g in range(n // SUBLANES):
        xg = x[SUBLANES * g:SUBLANES * (g + 1), :]
        for s in (1, 2, 4):
            xg = xg + jnp.where(rows >= s, pltpu.roll(xg, s, 0), 0.0)
        if carry is not None:
            xg = xg + carry
        carry = xg[SUBLANES - 1:SUBLANES, :]
        out.append(xg)
    return jnp.concatenate(out, axis=0)


def _inproj_ab_body(x_ref, gain_ref, w_ref, cos_ref, sin_ref, bias_ref, qkv_ref, cum_ref, carry_scr, *, tiles_per_seq):
    h = _rmsnorm(x_ref[...], gain_ref[...]).astype(BF16)

    @pl.when(pl.program_id(0) % tiles_per_seq == 0)
    def _():
        carry_scr[...] = jnp.zeros_like(carry_scr)

    x = _dot(h, w_ref[:, QKV_WIDTH:QKV_WIDTH + LANES]) + bias_ref[...]
    logf = -(jnp.maximum(-x, 0.0) + jnp.log1p(jnp.exp(-jnp.abs(x))))
    cum = _cumsum_groups(logf * LOG2E) + carry_scr[0:1, :]
    cum_ref[...] = cum
    carry_scr[0:1, :] = cum[cum.shape[0] - 1:, :]

    cos = cos_ref[...]
    sin = sin_ref[...]
    lane = lax.broadcasted_iota(jnp.int32, cos.shape, 1)
    low = (lane & (HEAD_DIM - 1)) < ROPE_HALF
    scale = HEAD_DIM ** -0.5 * LOG2E
    for c in range(QKV_WIDTH // 256):
        y = _dot(h, w_ref[:, c * 256:(c + 1) * 256])
        seg = c // 2
        for s in range(2):
            yy = y[:, s * LANES:(s + 1) * LANES]
            if seg in (0, 1):
                partner = jnp.where(low, pltpu.roll(yy, LANES - ROPE_HALF, 1), pltpu.roll(yy, ROPE_HALF, 1))
                yy = yy * cos + partner * sin
            if seg in (0, 3):
                yy = yy * scale
            qkv_ref[:, c * 256 + s * LANES:c * 256 + (s + 1) * LANES] = yy.astype(BF16)


def _inproj_ab(x2d, gain, w, cos, sin, bias, tm, seq_len):
    T = x2d.shape[0]
    wn = w.shape[1]
    return pl.pallas_call(
        functools.partial(_inproj_ab_body, tiles_per_seq=seq_len // tm),
        grid=(T // tm,),
        in_specs=[
            pl.BlockSpec((tm, D_MODEL), lambda i: (i, 0)),
            pl.BlockSpec((1, D_MODEL), lambda i: (0, 0)),
            pl.BlockSpec((D_MODEL, wn), lambda i: (0, 0)),
            pl.BlockSpec((tm, LANES), lambda i: (i, 0)),
            pl.BlockSpec((tm, LANES), lambda i: (i, 0)),
            pl.BlockSpec((1, LANES), lambda i: (0, 0)),
        ],
        out_specs=[
            pl.BlockSpec((tm, QKV_WIDTH), lambda i: (i, 0)),
            pl.BlockSpec((tm, LANES), lambda i: (i, 0)),
        ],
        out_shape=[
            jax.ShapeDtypeStruct((T, QKV_WIDTH), BF16),
            jax.ShapeDtypeStruct((T, LANES), F32),
        ],
        scratch_shapes=[pltpu.VMEM((SUBLANES, LANES), F32)],
        compiler_params=_params(("arbitrary",)),
        name="inproj_ab",
    )(x2d, gain, w, cos, sin, bias)


def _dilated_body(q_ref, k_ref, v_ref, o_ref, qf, kf, vf, ob, lb):
    S = q_ref.shape[0]
    nb = ATT_BLOCK
    qf[...] = q_ref[...].astype(F32)
    kf[...] = k_ref[...].astype(F32)
    vf[...] = v_ref[...].astype(F32)
    head0 = lax.broadcasted_iota(jnp.int32, (nb, LANES), 1) < HEAD_DIM
    qi2 = lax.broadcasted_iota(jnp.int32, (2 * nb, 2 * nb), 0) & (nb - 1)
    kj2 = lax.broadcasted_iota(jnp.int32, (2 * nb, 2 * nb), 1)
    valid2 = (kj2 >= qi2) & (kj2 <= qi2 + nb)
    qi1 = lax.broadcasted_iota(jnp.int32, (2 * nb, nb), 0) & (nb - 1)
    kj1 = lax.broadcasted_iota(jnp.int32, (2 * nb, nb), 1)
    valid1 = kj1 <= qi1

    def rows(start, size, r):
        return pl.ds(start, size) if r == 1 else pl.ds(start, size, stride=r)

    def block(br, r, q0, k0, nk):
        qs = qf[rows(q0, nb, r), :]
        kb = kf[rows(k0, nk, r), :].astype(BF16)
        vb = vf[rows(k0, nk, r), :].astype(BF16)
        q2 = jnp.concatenate([jnp.where(head0, qs, 0.0), jnp.where(head0, 0.0, qs)], axis=0).astype(BF16)
        s = jnp.where(valid2 if nk == 2 * nb else valid1, _dot_nt(q2, kb), -jnp.inf)
        m = jnp.max(s, axis=-1, keepdims=True)
        e = jnp.exp2(s - m)
        l = jnp.sum(e, axis=-1, keepdims=True)
        o = _dot(e.astype(BF16), vb) / l
        lse = jnp.broadcast_to(m + jnp.log2(l), (2 * nb, LANES))
        ob[br, rows(q0, nb, r), :] = jnp.where(head0, o[:nb], o[nb:])
        lb[br, rows(q0, nb, r), :] = jnp.where(head0, lse[:nb], lse[nb:])

    for br, r in enumerate(DILATIONS):
        n_blocks = S // (r * nb)
        if n_blocks == 1:
            def first_only(c, carry, br=br, r=r):
                block(br, r, c, c, nb)
                return carry
            lax.fori_loop(0, r, first_only, 0, unroll=True)
        else:
            for c in range(r):
                block(br, r, c, c, nb)

                def later(n, carry, br=br, r=r, c=c):
                    block(br, r, n * (nb * r) + c, (n - 1) * (nb * r) + c, 2 * nb)
                    return carry
                lax.fori_loop(1, n_blocks, later, 0, unroll=True)

    rows_per = 256
    for ch in range(S // rows_per):
        sl = pl.ds(ch * rows_per, rows_per)
        l0, l1, l2 = lb[0, sl, :], lb[1, sl, :], lb[2, sl, :]
        m = jnp.maximum(jnp.maximum(l0, l1), l2)
        w0, w1, w2 = jnp.exp2(l0 - m), jnp.exp2(l1 - m), jnp.exp2(l2 - m)
        o = (w0 * ob[0, sl, :] + w1 * ob[1, sl, :] + w2 * ob[2, sl, :]) / (w0 + w1 + w2)
        o_ref[sl, :] = o.astype(BF16)


def _dilated_attention(qkv, B, S):
    n_pairs = WIDTH_A // LANES
    blk = lambda off: pl.BlockSpec((None, S, LANES), lambda b, p: (b, 0, off + p))
    return pl.pallas_call(
        _dilated_body,
        grid=(B, n_pairs),
        in_specs=[blk(0), blk(n_pairs), blk(2 * n_pairs)],
        out_specs=pl.BlockSpec((None, S, LANES), lambda b, p: (b, 0, p)),
        out_shape=jax.ShapeDtypeStruct((B, S, WIDTH_A), BF16),
        scratch_shapes=[
            pltpu.VMEM((S, LANES), F32), pltpu.VMEM((S, LANES), F32), pltpu.VMEM((S, LANES), F32),
            pltpu.VMEM((len(DILATIONS), S, LANES), F32), pltpu.VMEM((len(DILATIONS), S, LANES), F32),
        ],
        compiler_params=_params(("parallel", "parallel")),
        name="dilated_attention",
    )(qkv, qkv, qkv)


def _fox_body(q_ref, k_ref, v_ref, c_ref, sel_ref, o_ref, qa_scr, ka_scr, *, tk):
    S = q_ref.shape[0]
    rows_per = 256
    lane = lax.broadcasted_iota(jnp.int32, (rows_per, LANES), 1)

    def build(i, carry):
        sl = pl.ds(i * rows_per, rows_per)
        q = q_ref[sl, :].astype(F32)
        k = k_ref[sl, :].astype(F32)
        extra = _dot(jnp.concatenate(_split3(c_ref[sl, :]), axis=1), sel_ref[...])
        for hh in range(2):
            own = (lane < HEAD_DIM) if hh == 0 else (lane >= HEAD_DIM)
            a0 = HEAD_DIM if hh == 0 else 0
            first = (lane >= a0) & (lane < a0 + 3)
            second = (lane >= a0 + 3) & (lane < a0 + 6)
            ex = extra[:, hh * LANES:(hh + 1) * LANES]
            qaug = jnp.where(own, q, jnp.where(first, ex, jnp.where(second, 1.0, 0.0)))
            kaug = jnp.where(own, k, jnp.where(second, ex, jnp.where(first, 1.0, 0.0)))
            qa_scr[hh, sl, :] = qaug.astype(BF16)
            ka_scr[hh, sl, :] = kaug.astype(BF16)
        return carry

    lax.fori_loop(0, S // rows_per, build, 0, unroll=True)

    n_blocks = S // tk
    row_t = lax.broadcasted_iota(jnp.int32, (tk, tk), 0)
    col_t = lax.broadcasted_iota(jnp.int32, (tk, tk), 1)
    causal = col_t <= row_t
    head0 = lax.broadcasted_iota(jnp.int32, (tk, LANES), 1) < HEAD_DIM
    state = [[None] * n_blocks for _ in range(2)]
    for j in range(n_blocks):
        ksl = pl.ds(j * tk, tk)
        vblk = v_ref[ksl, :]
        for hh in range(2):
            s_all = _dot_nt(qa_scr[hh, pl.ds(j * tk, S - j * tk), :], ka_scr[hh, ksl, :])
            es, scales = [], []
            for rb in range(j, n_blocks):
                s = s_all[(rb - j) * tk:(rb - j + 1) * tk, :]
                if rb == j:
                    s = jnp.where(causal, s, -jnp.inf)
                if j == 0:
                    m_new = jnp.max(s, axis=-1, keepdims=True)
                    alpha = None
                else:
                    m_old = state[hh][rb][0]
                    m_new = jnp.maximum(m_old, jnp.max(s, axis=-1, keepdims=True))
                    alpha = jnp.exp2(m_old - m_new)
                e = jnp.exp2(s - m_new)
                es.append(e.astype(BF16))
                scales.append((m_new, alpha, jnp.sum(e, axis=-1, keepdims=True)))
            pv_all = _dot(jnp.concatenate(es, axis=0) if len(es) > 1 else es[0], vblk)
            for rb in range(j, n_blocks):
                m_new, alpha, rowsum = scales[rb - j]
                pv = pv_all[(rb - j) * tk:(rb - j + 1) * tk, :]
                if alpha is None:
                    state[hh][rb] = (m_new, rowsum, pv)
                else:
                    _, l_old, acc_old = state[hh][rb]
                    state[hh][rb] = (m_new, alpha * l_old + rowsum, alpha * acc_old + pv)
        outs = [state[hh][j][2] / state[hh][j][1] for hh in range(2)]
        o_ref[pl.ds(j * tk, tk), :] = jnp.where(head0, outs[0], outs[1]).astype(BF16)


def _fox_selection(n_pairs):
    sel = np.zeros((n_pairs, 3 * LANES, 2 * LANES), np.float32)
    for p in range(n_pairs):
        for hh in range(2):
            a0 = hh * LANES + (HEAD_DIM if hh == 0 else 0)
            for piece in range(3):
                sel[p, piece * LANES + 2 * p + hh, a0 + piece] = 1.0
                sel[p, piece * LANES + 2 * p + hh, a0 + 3 + piece] = -1.0
    return jnp.asarray(sel, BF16)


def _fox_attention(qkv, cum, B, S, tk=256):
    n_pairs = WIDTH_B // LANES
    base = 3 * (WIDTH_A // LANES)
    blk = lambda off: pl.BlockSpec((None, S, LANES), lambda b, p: (b, 0, off + p))
    return pl.pallas_call(
        functools.partial(_fox_body, tk=tk),
        grid=(B, n_pairs),
        in_specs=[
            blk(base), blk(base + n_pairs), blk(base + 2 * n_pairs),
            pl.BlockSpec((None, S, LANES), lambda b, p: (b, 0, 0)),
            pl.BlockSpec((None, 3 * LANES, 2 * LANES), lambda b, p: (p, 0, 0)),
        ],
        out_specs=pl.BlockSpec((None, S, LANES), lambda b, p: (b, 0, p)),
        out_shape=jax.ShapeDtypeStruct((B, S, WIDTH_B), BF16),
        scratch_shapes=[pltpu.VMEM((2, S, LANES), BF16), pltpu.VMEM((2, S, LANES), BF16)],
        compiler_params=_params(("parallel", "parallel")),
        name="fox_attention",
    )(qkv, qkv, qkv, cum, _fox_selection(n_pairs))


def _outproj_ffn_body(x_ref, oa_ref, ob_ref, woa_ref, wob_ref, gain_ref, wg_ref, wu_ref, wd_ref, out_ref, a_scr):
    x1 = x_ref[...] + _dot(oa_ref[...], woa_ref[...]) + _dot(ob_ref[...], wob_ref[...])
    h = _rmsnorm(x1, gain_ref[...]).astype(BF16)
    for c in range(a_scr.shape[1] // 256):
        cols = slice(c * 256, (c + 1) * 256)
        g = _dot(h, wg_ref[:, cols])
        u = _dot(h, wu_ref[:, cols])
        a_scr[:, cols] = (g * _sigmoid(g) * u).astype(BF16)
    out_ref[...] = x1 + _dot(a_scr[...], wd_ref[...])


def _outproj_ffn(x2d, oa, ob, woa, wob, gain, wg, wu, wd, tm):
    T = x2d.shape[0]
    F = wg.shape[1]
    once = dict(pipeline_mode=pl.Buffered(1))
    return pl.pallas_call(
        _outproj_ffn_body,
        grid=(T // tm,),
        in_specs=[
            pl.BlockSpec((tm, D_MODEL), lambda i: (i, 0)),
            pl.BlockSpec((tm, WIDTH_A), lambda i: (i, 0)),
            pl.BlockSpec((tm, WIDTH_B), lambda i: (i, 0)),
            pl.BlockSpec((WIDTH_A, D_MODEL), lambda i: (0, 0), **once),
            pl.BlockSpec((WIDTH_B, D_MODEL), lambda i: (0, 0), **once),
            pl.BlockSpec((1, D_MODEL), lambda i: (0, 0)),
            pl.BlockSpec((D_MODEL, F), lambda i: (0, 0), **once),
            pl.BlockSpec((D_MODEL, F), lambda i: (0, 0), **once),
            pl.BlockSpec((F, D_MODEL), lambda i: (0, 0), **once),
        ],
        out_specs=pl.BlockSpec((tm, D_MODEL), lambda i: (i, 0)),
        out_shape=jax.ShapeDtypeStruct((T, D_MODEL), F32),
        scratch_shapes=[pltpu.VMEM((tm, F), BF16)],
        compiler_params=_params(("parallel",)),
        name="outproj_ffn",
    )(x2d, oa, ob, woa, wob, gain, wg, wu, wd)


def _inproj_c_body(x_ref, gain_ref, w_ref, qig_ref, f_ref):
    h = _rmsnorm(x_ref[...], gain_ref[...]).astype(BF16)
    n_qig = qig_ref.shape[1]
    for c in range(n_qig // 256):
        qig_ref[:, c * 256:(c + 1) * 256] = _dot(h, w_ref[:, c * 256:(c + 1) * 256]).astype(BF16)
    for c in range(f_ref.shape[1] // 256):
        f_ref[:, c * 256:(c + 1) * 256] = _dot(h, w_ref[:, n_qig + c * 256:n_qig + (c + 1) * 256])


def _inproj_c(x2d, gain, w, tm):
    T = x2d.shape[0]
    return pl.pallas_call(
        _inproj_c_body,
        grid=(T // tm,),
        in_specs=[
            pl.BlockSpec((tm, D_MODEL), lambda i: (i, 0)),
            pl.BlockSpec((1, D_MODEL), lambda i: (0, 0)),
            pl.BlockSpec((D_MODEL, 4 * D_MODEL), lambda i: (0, 0)),
        ],
        out_specs=[
            pl.BlockSpec((tm, 3 * D_MODEL), lambda i: (i, 0)),
            pl.BlockSpec((tm, D_MODEL), lambda i: (i, 0)),
        ],
        out_shape=[
            jax.ShapeDtypeStruct((T, 3 * D_MODEL), BF16),
            jax.ShapeDtypeStruct((T, D_MODEL), F32),
        ],
        compiler_params=_params(("parallel",)),
        name="inproj_c",
    )(x2d, gain, w)


def _hgrn_body(q_ref, i_ref, g_ref, f_ref, lb_ref, gn_ref, o_ref, state_scr, o_scr, ops0, ops1, dec0, dec1, *, heads):
    S = q_ref.shape[0]
    C = HGRN_CHUNK
    half = C // 2
    quarter = C // 4
    assert quarter == HGRN_LEAF
    n_chunks = S // C
    row = lax.broadcasted_iota(jnp.int32, (C, C), 0)
    col = lax.broadcasted_iota(jnp.int32, (C, C), 1)
    mask_cross = (row >= half) & (col < half)
    mask_same = ((row // half) == (col // half)) & (col <= row)
    r = lax.broadcasted_iota(jnp.int32, (C, heads * HGRN_DK), 0)
    scale = HGRN_DK ** -0.5
    gn = jnp.concatenate([gn_ref[...]] * heads, axis=1)
    state_scr[...] = jnp.zeros_like(state_scr)

    def prepare(ci, ops, dec):
        sl = pl.ds(ci * C, C)
        lbv = lb_ref[...]
        f = lbv + (1.0 - lbv) * _sigmoid(f_ref[sl, :])
        k = 1.0 - f
        q = q_ref[sl, :].astype(F32) * scale
        b = _cumsum_groups(jnp.log2(f))
        b_last = b[C - 1:C, :]
        e_cross = b - b[half - 1:half, :]
        e_same = b - jnp.where(r < half, b[quarter - 1:quarter, :], b[half + quarter - 1:half + quarter, :])
        ops[0] = (q * jnp.exp2(jnp.minimum(e_cross, 0.0))).astype(BF16)
        ops[1] = (k * jnp.exp2(jnp.minimum(-e_cross, 0.0))).astype(BF16)
        ops[2] = (q * jnp.exp2(e_same)).astype(BF16)
        ops[3] = (k * jnp.exp2(-e_same)).astype(BF16)
        ops[4] = (q * jnp.exp2(b)).astype(BF16)
        ops[5] = (k * jnp.exp2(b_last - b)).astype(BF16)
        dec[0:1, :] = jnp.exp2(b_last)

    def contract(ci, ops, dec):
        sl = pl.ds(ci * C, C)
        for hd in range(heads):
            cols = slice(hd * HGRN_DK, (hd + 1) * HGRN_DK)
            v = i_ref[sl, cols]
            scores = (jnp.where(mask_cross, _dot_nt(ops[0, :, cols], ops[1, :, cols]), 0.0)
                      + jnp.where(mask_same, _dot_nt(ops[2, :, cols], ops[3, :, cols]), 0.0))
            state_t = state_scr[hd]
            o_scr[sl, cols] = _dot(scores.astype(BF16), v) + _dot_nt(ops[4, :, cols], state_t.astype(BF16))
            v_t = v.astype(F32).T.astype(BF16)
            state_scr[hd] = state_t * dec[0:1, cols] + _dot(v_t, ops[5, :, cols])

    def finish(ci):
        sl = pl.ds(ci * C, C)
        o = o_scr[sl, :]
        ys = []
        for hd in range(heads):
            oh = o[:, hd * HGRN_DK:(hd + 1) * HGRN_DK]
            ys.append(oh * lax.rsqrt(jnp.mean(oh * oh, axis=-1, keepdims=True) + EPS))
        gate = g_ref[sl, :].astype(F32)
        o_ref[sl, :] = (jnp.concatenate(ys, axis=1) * gn * (gate * _sigmoid(gate))).astype(BF16)

    prepare(0, ops0, dec0)

    def pair(j, carry):
        prepare(2 * j + 1, ops1, dec1)
        contract(2 * j, ops0, dec0)
        prepare(2 * j + 2, ops0, dec0)
        contract(2 * j + 1, ops1, dec1)
        finish(jnp.maximum(2 * j - 1, 0))
        finish(2 * j)
        return carry

    lax.fori_loop(0, n_chunks // 2 - 1, pair, 0, unroll=5)
    prepare(n_chunks - 1, ops1, dec1)
    contract(n_chunks - 2, ops0, dec0)
    contract(n_chunks - 1, ops1, dec1)
    for ci in range(n_chunks - 3, n_chunks):
        finish(ci)


def _hgrn(qig, flog, lb, gn, B, S, heads=4):
    ng = N_HEADS_C // heads
    w = heads * HGRN_DK
    blk = lambda off: pl.BlockSpec((None, S, w), lambda b, h: (b, 0, off + h))
    return pl.pallas_call(
        functools.partial(_hgrn_body, heads=heads),
        grid=(B, ng),
        in_specs=[
            blk(0), blk(ng), blk(2 * ng),
            pl.BlockSpec((None, S, w), lambda b, h: (b, 0, h)),
            pl.BlockSpec((1, w), lambda b, h: (0, h)),
            pl.BlockSpec((1, HGRN_DK), lambda b, h: (0, 0)),
        ],
        out_specs=pl.BlockSpec((None, S, w), lambda b, h: (b, 0, h)),
        out_shape=jax.ShapeDtypeStruct((B, S, D_MODEL), BF16),
        scratch_shapes=[
            pltpu.VMEM((heads, HGRN_DK, HGRN_DK), F32), pltpu.VMEM((S, w), F32),
            pltpu.VMEM((6, HGRN_CHUNK, w), BF16), pltpu.VMEM((6, HGRN_CHUNK, w), BF16),
            pltpu.VMEM((SUBLANES, w), F32), pltpu.VMEM((SUBLANES, w), F32),
        ],
        compiler_params=_params(("parallel", "parallel")),
        name="hgrn2",
    )(qig, qig, qig, flog, lb, gn)


SEG_ALIGN = SUBLANES
SEG_BITS = tuple(range(9, 2, -1))


def _local_rows(tm):
    return TOP_K * tm + N_EXPERTS * SEG_ALIGN


def _outproj_router_body(x_ref, o_ref, w_ref, gain_ref, r_ref, x3_ref, xl_ref, pos_ref, gate_ref, seg_ref, *, tm):
    for t in range(x_ref.shape[0] // tm):
        _route_tile(x_ref, o_ref, w_ref, gain_ref, r_ref, x3_ref, xl_ref, pos_ref, gate_ref, seg_ref, t, tm)


def _route_tile(x_ref, o_ref, w_ref, gain_ref, r_ref, x3_ref, xl_ref, pos_ref, gate_ref, seg_ref, t, tm):
    rows = pl.ds(t * tm, tm)
    lr = _local_rows(tm)
    x3 = x_ref[rows, :] + _dot(o_ref[rows, :], w_ref[...])
    x3_ref[rows, :] = x3
    h = _rmsnorm(x3, gain_ref[...])
    h_bf = h.astype(BF16)
    logits = _dot(h_bf, r_ref[...])
    lane = lax.broadcasted_iota(jnp.int32, logits.shape, 1)
    lane_f = lane.astype(F32)
    lg = jnp.where(lane < N_EXPERTS, logits, -jnp.inf)
    m1 = jnp.max(lg, axis=-1, keepdims=True)
    i1 = jnp.min(jnp.where(lg == m1, lane_f, float(LANES)), axis=-1, keepdims=True)
    lg2 = jnp.where(lane_f == i1, -jnp.inf, lg)
    m2 = jnp.max(lg2, axis=-1, keepdims=True)
    i2 = jnp.min(jnp.where(lg2 == m2, lane_f, float(LANES)), axis=-1, keepdims=True)
    e2 = jnp.exp(m2 - m1)
    den = 1.0 + e2
    gate_ref[rows, :] = jnp.where(lane == 0, 1.0 / den, jnp.where(lane == 1, e2 / den, 0.0))

    oh1 = (lane_f == i1).astype(F32)
    oh2 = (lane_f == i2).astype(F32)
    c1 = _cumsum_groups(oh1)
    c2 = _cumsum_groups(oh2)
    n1 = c1[tm - 1:tm, :]
    count = n1 + c2[tm - 1:tm, :]
    padded = jnp.floor((count + (SEG_ALIGN - 1.0)) * (1.0 / SEG_ALIGN)) * SEG_ALIGN
    run = jnp.broadcast_to(padded, (SUBLANES, LANES))
    lane8 = lax.broadcasted_iota(jnp.int32, (SUBLANES, LANES), 1)
    for s in (1, 2, 4):
        run = run + jnp.where(lane8 >= s, pltpu.roll(run, s, 1), 0.0)
    start = run[0:1, :] - padded
    pos1 = jnp.sum(oh1 * (start + c1 - 1.0), axis=-1, keepdims=True)
    pos2 = jnp.sum(oh2 * (start + n1 + c2 - 1.0), axis=-1, keepdims=True)
    pos_ref[rows, :] = jnp.where(lane == 0, pos1, jnp.where(lane == 1, pos2, 0.0)).astype(jnp.int32)
    pos1_row = jnp.broadcast_to(pos1, (tm, LANES)).T[0:1, :]
    pos2_row = jnp.broadcast_to(pos2, (tm, LANES)).T[0:1, :]
    slot = lax.broadcasted_iota(jnp.int32, (lr, tm), 0).astype(F32)
    perm = ((slot == pos1_row) | (slot == pos2_row)).astype(BF16)
    xl_ref[pl.ds(t * lr, lr), :] = _dot(perm, h_bf)
    r8 = lax.broadcasted_iota(jnp.int32, (SUBLANES, LANES), 0)
    seg_ref[pl.ds(t * SUBLANES, SUBLANES), :] = jnp.where(r8 == 0, count, jnp.where(r8 == 1, padded, jnp.where(r8 == 2, start, 0.0))).astype(jnp.int32)


def _outproj_router(x2d, o, w, gain, r, tm, group=2):
    T = x2d.shape[0]
    n_tt = T // tm
    lr = _local_rows(tm)
    tg = group * tm
    return pl.pallas_call(
        functools.partial(_outproj_router_body, tm=tm),
        grid=(n_tt // group,),
        in_specs=[
            pl.BlockSpec((tg, D_MODEL), lambda i: (i, 0)),
            pl.BlockSpec((tg, D_MODEL), lambda i: (i, 0)),
            pl.BlockSpec((D_MODEL, D_MODEL), lambda i: (0, 0)),
            pl.BlockSpec((1, D_MODEL), lambda i: (0, 0)),
            pl.BlockSpec((D_MODEL, LANES), lambda i: (0, 0)),
        ],
        out_specs=[
            pl.BlockSpec((tg, D_MODEL), lambda i: (i, 0)),
            pl.BlockSpec((group * lr, D_MODEL), lambda i: (i, 0)),
            pl.BlockSpec((tg, LANES), lambda i: (i, 0)),
            pl.BlockSpec((tg, LANES), lambda i: (i, 0)),
            pl.BlockSpec((group * SUBLANES, LANES), lambda i: (i, 0)),
        ],
        out_shape=[
            jax.ShapeDtypeStruct((T, D_MODEL), F32),
            jax.ShapeDtypeStruct((n_tt * lr, D_MODEL), F32),
            jax.ShapeDtypeStruct((T, LANES), jnp.int32),
            jax.ShapeDtypeStruct((T, LANES), F32),
            jax.ShapeDtypeStruct((n_tt * SUBLANES, LANES), jnp.int32),
        ],
        compiler_params=_params(("parallel",)),
        name="outproj_router",
    )(x2d, o, w, gain, r)


def _run_copies(n, src, s0, dst, d0, sem):
    out = []
    for b in SEG_BITS:
        offs = (n >> (b + 1)) << (b + 1)
        cp = pltpu.make_async_copy(src.at[pl.ds(pl.multiple_of(s0 + offs, SEG_ALIGN), 1 << b), :],
                                   dst.at[pl.ds(pl.multiple_of(d0 + offs, SEG_ALIGN), 1 << b), :], sem)
        out.append((((n >> b) & 1) == 1, cp))
    return out


def _experts_body(te_ref, nu_ref, r0_ref, jlo_ref, jhi_ref, valid_ref, cs_ref, lp_ref, src_ref,
                  xl_hbm, wg_ref, wu_ref, wd_ref, out_ref, xbuf, xb_scr, a_scr, sem, *, n_tt):
    i = pl.program_id(0)
    f = pl.program_id(1)
    tm = out_ref.shape[0]

    def move(tile, slot, wait):
        base = te_ref[tile] * n_tt
        r0 = r0_ref[tile]

        def one_run(j, carry):
            c0 = cs_ref[base + j]
            lo = jnp.maximum(c0, r0)
            hi = jnp.minimum(c0 + lp_ref[base + j], r0 + tm)
            n = jnp.maximum(hi - lo, 0)
            for cond, cp in _run_copies(n, xl_hbm, src_ref[base + j] + (lo - c0), xbuf.at[slot], lo - r0, sem.at[slot]):
                @pl.when(cond)
                def _():
                    cp.wait() if wait else cp.start()
            return carry

        lax.fori_loop(jlo_ref[tile], jhi_ref[tile], one_run, 0)

    @pl.when(f == 0)
    def _():
        @pl.when(i == 0)
        def _():
            xbuf[...] = jnp.zeros_like(xbuf)
            move(0, 0, False)

        @pl.when(i < nu_ref[0])
        def _():
            move(i, i % 2, True)

        @pl.when(i + 1 < nu_ref[0])
        def _():
            move(i + 1, (i + 1) % 2, False)

        row = lax.broadcasted_iota(jnp.int32, (tm, D_MODEL), 0)
        xb_scr[...] = jnp.where(row < valid_ref[i], xbuf[i % 2], 0.0).astype(BF16)
        out_ref[...] = jnp.zeros_like(out_ref)

    @pl.when(i < nu_ref[0])
    def _():
        xb = xb_scr[...]
        for c in range(a_scr.shape[1] // 256):
            cols = slice(c * 256, (c + 1) * 256)
            g = _dot(xb, wg_ref[:, cols])
            u = _dot(xb, wu_ref[:, cols])
            a_scr[:, cols] = (g * _sigmoid(g) * u).astype(BF16)
        out_ref[...] += _dot(a_scr[...], wd_ref[...])


def _moe_experts(tabs, xl, wg, wu, wd, n_rows, n_tt, tm, tf):
    F = wg.shape[2]
    nf = F // tf

    def f_eff(i, f, nu):
        return jnp.where(i < nu[0], f, nf - 1)

    return pl.pallas_call(
        functools.partial(_experts_body, n_tt=n_tt),
        grid_spec=pltpu.PrefetchScalarGridSpec(
            num_scalar_prefetch=len(tabs),
            grid=(n_rows // tm, nf),
            in_specs=[
                pl.BlockSpec(memory_space=pl.ANY),
                pl.BlockSpec((None, D_MODEL, tf), lambda i, f, te, nu, *_: (te[i], 0, f_eff(i, f, nu))),
                pl.BlockSpec((None, D_MODEL, tf), lambda i, f, te, nu, *_: (te[i], 0, f_eff(i, f, nu))),
                pl.BlockSpec((None, tf, D_MODEL), lambda i, f, te, nu, *_: (te[i], f_eff(i, f, nu), 0)),
            ],
            out_specs=pl.BlockSpec((tm, D_MODEL), lambda i, f, *_: (i, 0)),
            scratch_shapes=[pltpu.VMEM((2, tm, D_MODEL), F32), pltpu.VMEM((tm, D_MODEL), BF16),
                            pltpu.VMEM((tm, tf), BF16), pltpu.SemaphoreType.DMA((2,))],
        ),
        out_shape=jax.ShapeDtypeStruct((n_rows, D_MODEL), F32),
        compiler_params=_params(("arbitrary", "arbitrary")),
        name="moe_experts",
    )(*tabs, xl, wg, wu, wd)


def _combine_body(row_ref, lp_ref, off_ref, x_ref, pos_ref, gate_ref, ys_hbm, gain_ref, out_ref, yl, sem):
    j = pl.program_id(0)
    tm = x_ref.shape[0]
    lr = yl.shape[1]

    def move(tile, slot, wait):
        for e in range(N_EXPERTS):
            t = tile * N_EXPERTS + e
            for cond, cp in _run_copies(lp_ref[t], ys_hbm, row_ref[t], yl.at[slot], off_ref[t], sem.at[slot]):
                @pl.when(cond)
                def _():
                    cp.wait() if wait else cp.start()

    @pl.when(j == 0)
    def _():
        yl[...] = jnp.zeros_like(yl)
        move(0, 0, False)

    move(j, j % 2, True)

    @pl.when(j + 1 < pl.num_programs(0))
    def _():
        move(j + 1, (j + 1) % 2, False)

    last = j * N_EXPERTS + N_EXPERTS - 1
    used = off_ref[last] + lp_ref[last]
    row = lax.broadcasted_iota(jnp.int32, (lr, D_MODEL), 0)
    y_sorted = jnp.where(row < used, yl[j % 2], 0.0).astype(BF16)
    pos = pos_ref[...]
    gates = gate_ref[...]
    slot = lax.broadcasted_iota(jnp.int32, (tm, lr), 1)
    weights = jnp.where(slot == pos[:, 0:1], gates[:, 0:1], jnp.where(slot == pos[:, 1:2], gates[:, 1:2], 0.0))
    y = x_ref[...] + _dot(weights.astype(BF16), y_sorted)
    out_ref[...] = _rmsnorm(y, gain_ref[...])


def _moe_combine(tabs, x3, pos, gates, ys, gain, tm):
    T = x3.shape[0]
    return pl.pallas_call(
        _combine_body,
        grid_spec=pltpu.PrefetchScalarGridSpec(
            num_scalar_prefetch=len(tabs),
            grid=(T // tm,),
            in_specs=[
                pl.BlockSpec((tm, D_MODEL), lambda i, *_: (i, 0)),
                pl.BlockSpec((tm, LANES), lambda i, *_: (i, 0)),
                pl.BlockSpec((tm, LANES), lambda i, *_: (i, 0)),
                pl.BlockSpec(memory_space=pl.ANY),
                pl.BlockSpec((1, D_MODEL), lambda i, *_: (0, 0)),
            ],
            out_specs=pl.BlockSpec((tm, D_MODEL), lambda i, *_: (i, 0)),
            scratch_shapes=[pltpu.VMEM((2, _local_rows(tm), D_MODEL), F32), pltpu.SemaphoreType.DMA((2,))],
        ),
        out_shape=jax.ShapeDtypeStruct((T, D_MODEL), F32),
        compiler_params=_params(("arbitrary",)),
        name="moe_combine",
    )(*tabs, x3, pos, gates, ys, gain)


def _routing_tables(seg, n_tt, tm, tm_e):
    seg = seg.reshape(n_tt, SUBLANES, LANES)
    lp = seg[:, 1, :N_EXPERTS]
    off = seg[:, 2, :N_EXPERTS]
    cs = jnp.cumsum(lp, axis=0) - lp
    total = jnp.sum(lp, axis=0)
    padded = ((total + tm_e - 1) // tm_e) * tm_e
    ends = jnp.cumsum(padded)
    starts = ends - padded
    n_rows = -(-(TOP_K * n_tt * tm + N_EXPERTS * (SEG_ALIGN - 1) * n_tt) // tm_e) * tm_e + N_EXPERTS * tm_e
    tile_start = jnp.arange(n_rows // tm_e, dtype=jnp.int32) * tm_e
    te = jnp.minimum(jnp.sum((tile_start[:, None] >= ends[None, :]).astype(jnp.int32), axis=1), N_EXPERTS - 1)
    n_used = (ends[-1] // tm_e).astype(jnp.int32).reshape(1)
    r0 = tile_start - starts[te]
    cs_t = cs[:, te]
    run_end_t = cs_t + lp[:, te]
    jlo = jnp.sum((run_end_t <= r0[None, :]).astype(jnp.int32), axis=0)
    jhi = jnp.sum((cs_t < (r0 + tm_e)[None, :]).astype(jnp.int32), axis=0)
    valid = jnp.clip(total[te] - r0, 0, tm_e)
    src = jnp.arange(n_tt, dtype=jnp.int32)[:, None] * _local_rows(tm) + off
    i32 = lambda a: a.astype(jnp.int32)
    expert_tabs = (i32(te), n_used, i32(r0), i32(jlo), i32(jhi), i32(valid),
                   i32(cs.T.reshape(-1)), i32(lp.T.reshape(-1)), i32(src.T.reshape(-1)))
    combine_tabs = (i32((starts[None, :] + cs).reshape(-1)), i32(lp.reshape(-1)), i32(off.reshape(-1)))
    return expert_tabs, combine_tabs, n_rows


def _rope_tables(positions):
    B, S = positions.shape
    inv_freq = jnp.power(jnp.float32(ROPE_THETA), -jnp.arange(ROPE_HALF, dtype=F32) / ROPE_HALF)
    ang = positions.astype(F32)[..., None] * inv_freq
    cos, sin = jnp.cos(ang), jnp.sin(ang)
    rest = HEAD_DIM - ROPE_DIM
    cos_h = jnp.concatenate([cos, cos, jnp.ones((B, S, rest), F32)], axis=-1)
    sin_h = jnp.concatenate([-sin, sin, jnp.zeros((B, S, rest), F32)], axis=-1)
    reps = LANES // HEAD_DIM
    return (jnp.tile(cos_h, (1, 1, reps)).reshape(B * S, LANES),
            jnp.tile(sin_h, (1, 1, reps)).reshape(B * S, LANES))


def kernel(x, positions, norm_mix, norm_ffn, w_in_ab, fgate_bias, w_out_ab, w_in_c, lower_bounds, gnorm_c, w_out_c,
           w_gate_ffn, w_up_ffn, w_down_ffn, router, w_gate_moe, w_up_moe, w_down_moe, norm_final):
    B, S, D = x.shape
    T = B * S
    assert D == D_MODEL and S % PROJ_ROWS == 0
    tm_proj, tm, tm_e = PROJ_ROWS, ROUTE_ROWS, EXPERT_ROWS
    x2d = x.reshape(T, D)

    w_ab = jnp.pad(w_in_ab[0], ((0, 0), (0, QKV_WIDTH + LANES - w_in_ab.shape[2]))).astype(BF16)
    cos, sin = _rope_tables(positions)
    bias = jnp.pad(fgate_bias[0], (0, LANES - N_HEADS_B)).reshape(1, LANES)
    qkv, cum = _inproj_ab(x2d, norm_mix[0:1], w_ab, cos, sin, bias, tm_proj, S)
    qkv = qkv.reshape(B, S, QKV_WIDTH)
    out_a = _dilated_attention(qkv, B, S).reshape(T, WIDTH_A)
    out_b = _fox_attention(qkv, cum.reshape(B, S, LANES), B, S).reshape(T, WIDTH_B)
    w_o = w_out_ab[0].astype(BF16)
    x2 = _outproj_ffn(x2d, out_a, out_b, w_o[:WIDTH_A], w_o[WIDTH_A:], norm_ffn[0:1],
                      w_gate_ffn[0].astype(BF16), w_up_ffn[0].astype(BF16), w_down_ffn[0].astype(BF16), tm_proj)

    lb_all = jnp.cumsum(jax.nn.softmax(lower_bounds.astype(F32), axis=0), axis=0)
    lb = (lb_all - lb_all[0:1])[1].reshape(1, D)
    wq, wf, wi, wg = jnp.split(w_in_c[0], 4, axis=-1)
    w_c = jnp.concatenate([wq, wi, wg, wf], axis=-1).astype(BF16)
    qig, flog_c = _inproj_c(x2, norm_mix[1:2], w_c, tm_proj)
    o_c = _hgrn(qig.reshape(B, S, 3 * D), flog_c.reshape(B, S, D), lb, gnorm_c[0:1], B, S).reshape(T, D)

    r_pad = jnp.pad(router[0], ((0, 0), (0, LANES - N_EXPERTS))).astype(BF16)
    x3, xl, pos, gates, seg = _outproj_router(x2, o_c, w_out_c[0].astype(BF16), norm_ffn[1:2], r_pad, tm)
    expert_tabs, combine_tabs, n_rows = _routing_tables(seg, T // tm, tm, tm_e)
    ys = _moe_experts(expert_tabs, xl, w_gate_moe[0].astype(BF16), w_up_moe[0].astype(BF16),
                      w_down_moe[0].astype(BF16), n_rows, T // tm, tm_e, w_gate_moe.shape[3] // 2)
    out = _moe_combine(combine_tabs, x3, pos, gates, ys, norm_final.reshape(1, D), tm)
    return out.reshape(B, S, D)
```

```python
import functools

import jax
import jax.numpy as jnp
import numpy as np
from jax import lax
from jax.experimental import pallas as pl
from jax.experimental.pallas import tpu as pltpu

F32 = jnp.float32
BF16 = jnp.bfloat16

D_MODEL = 1024
HEAD_DIM = 64
N_HEADS_A = 8
N_HEADS_B = 8
WIDTH_A = N_HEADS_A * HEAD_DIM
WIDTH_B = N_HEADS_B * HEAD_DIM
QKV_WIDTH = 3 * (WIDTH_A + WIDTH_B)
ROPE_THETA = 500000.0
ROPE_DIM = HEAD_DIM // 4
ROPE_HALF = ROPE_DIM // 2
ATT_BLOCK = 128
DILATIONS = (1, 4, 16)
N_HEADS_C = 8
HGRN_DK = 128
HGRN_CHUNK = 64
HGRN_LEAF = 16
N_EXPERTS = 8
TOP_K = 2
EPS = 1e-6
LOG2E = 1.4426950408889634

LANES = 128
SUBLANES = 8
MXU = 256
VMEM_LIMIT = 56 * 1024 * 1024
PROJ_ROWS = 1024
ROUTE_ROWS = 512
EXPERT_ROWS = 1024

NT_DIMS = (((1,), (1,)), ((), ()))


def _params(semantics, **kw):
    return pltpu.CompilerParams(dimension_semantics=semantics, vmem_limit_bytes=VMEM_LIMIT, **kw)


def _rmsnorm(x, gain):
    return x * lax.rsqrt(jnp.mean(x * x, axis=-1, keepdims=True) + EPS) * gain


def _sigmoid(x):
    return 1.0 / (1.0 + jnp.exp(-x))


def _split3(x):
    hi = x.astype(BF16)
    r1 = x - hi.astype(F32)
    mid = r1.astype(BF16)
    lo = (r1 - mid.astype(F32)).astype(BF16)
    return hi, mid, lo


def _dot(a, b):
    return jnp.dot(a, b, preferred_element_type=F32)


def _dot_nt(a, b):
    return lax.dot_general(a, b, NT_DIMS, preferred_element_type=F32)


def _cumsum_groups(x):
    n, w = x.shape
    rows = lax.broadcasted_iota(jnp.int32, (SUBLANES, w), 0)
    out, carry = [], None
    for g in range(n // SUBLANES):
        xg = x[SUBLANES * g:SUBLANES * (g + 1), :]
        for s in (1, 2, 4):
            xg = xg + jnp.where(rows >= s, pltpu.roll(xg, s, 0), 0.0)
        if carry is not None:
            xg = xg + carry
        carry = xg[SUBLANES - 1:SUBLANES, :]
        out.append(xg)
    return jnp.concatenate(out, axis=0)


def _inproj_ab_body(x_ref, gain_ref, w_ref, cos_ref, sin_ref, bias_ref, qkv_ref, cum_ref, carry_scr, *, tiles_per_seq):
    h = _rmsnorm(x_ref[...], gain_ref[...]).astype(BF16)

    @pl.when(pl.program_id(0) % tiles_per_seq == 0)
    def _():
        carry_scr[...] = jnp.zeros_like(carry_scr)

    x = _dot(h, w_ref[:, QKV_WIDTH:QKV_WIDTH + LANES]) + bias_ref[...]
    logf = -(jnp.maximum(-x, 0.0) + jnp.log1p(jnp.exp(-jnp.abs(x))))
    cum = _cumsum_groups(logf * LOG2E) + carry_scr[0:1, :]
    cum_ref[...] = cum
    carry_scr[0:1, :] = cum[cum.shape[0] - 1:, :]

    cos = cos_ref[...]
    sin = sin_ref[...]
    lane = lax.broadcasted_iota(jnp.int32, cos.shape, 1)
    low = (lane & (HEAD_DIM - 1)) < ROPE_HALF
    scale = HEAD_DIM ** -0.5 * LOG2E
    for c in range(QKV_WIDTH // MXU):
        y = _dot(h, w_ref[:, c * MXU:(c + 1) * MXU])
        seg = c // 2
        for s in range(2):
            yy = y[:, s * LANES:(s + 1) * LANES]
            if seg in (0, 1):
                partner = jnp.where(low, pltpu.roll(yy, LANES - ROPE_HALF, 1), pltpu.roll(yy, ROPE_HALF, 1))
                yy = yy * cos + partner * sin
            if seg in (0, 3):
                yy = yy * scale
            qkv_ref[:, c * MXU + s * LANES:c * MXU + (s + 1) * LANES] = yy.astype(BF16)


def _inproj_ab(x2d, gain, w, cos, sin, bias, tm, seq_len):
    T = x2d.shape[0]
    wn = w.shape[1]
    return pl.pallas_call(
        functools.partial(_inproj_ab_body, tiles_per_seq=seq_len // tm),
        grid=(T // tm,),
        in_specs=[
            pl.BlockSpec((tm, D_MODEL), lambda i: (i, 0)),
            pl.BlockSpec((1, D_MODEL), lambda i: (0, 0)),
            pl.BlockSpec((D_MODEL, wn), lambda i: (0, 0)),
            pl.BlockSpec((tm, LANES), lambda i: (i, 0)),
            pl.BlockSpec((tm, LANES), lambda i: (i, 0)),
            pl.BlockSpec((1, LANES), lambda i: (0, 0)),
        ],
        out_specs=[
            pl.BlockSpec((tm, QKV_WIDTH), lambda i: (i, 0)),
            pl.BlockSpec((tm, LANES), lambda i: (i, 0)),
        ],
        out_shape=[
            jax.ShapeDtypeStruct((T, QKV_WIDTH), BF16),
            jax.ShapeDtypeStruct((T, LANES), F32),
        ],
        scratch_shapes=[pltpu.VMEM((SUBLANES, LANES), F32)],
        compiler_params=_params(("arbitrary",)),
        name="inproj_ab",
    )(x2d, gain, w, cos, sin, bias)


def _dilated_body(q_ref, k_ref, v_ref, o_ref, qf, kf, vf, ob, lb):
    S = q_ref.shape[0]
    nb = ATT_BLOCK
    qf[...] = q_ref[...].astype(F32)
    kf[...] = k_ref[...].astype(F32)
    vf[...] = v_ref[...].astype(F32)
    head0 = lax.broadcasted_iota(jnp.int32, (nb, LANES), 1) < HEAD_DIM
    qi2 = lax.broadcasted_iota(jnp.int32, (2 * nb, 2 * nb), 0) & (nb - 1)
    kj2 = lax.broadcasted_iota(jnp.int32, (2 * nb, 2 * nb), 1)
    valid2 = (kj2 >= qi2) & (kj2 <= qi2 + nb)
    qi1 = lax.broadcasted_iota(jnp.int32, (2 * nb, nb), 0) & (nb - 1)
    kj1 = lax.broadcasted_iota(jnp.int32, (2 * nb, nb), 1)
    valid1 = kj1 <= qi1

    def rows(start, size, r):
        return pl.ds(start, size) if r == 1 else pl.ds(start, size, stride=r)

    def block(br, r, q0, k0, nk):
        qs = qf[rows(q0, nb, r), :]
        kb = kf[rows(k0, nk, r), :].astype(BF16)
        vb = vf[rows(k0, nk, r), :].astype(BF16)
        q2 = jnp.concatenate([jnp.where(head0, qs, 0.0), jnp.where(head0, 0.0, qs)], axis=0).astype(BF16)
        s = jnp.where(valid2 if nk == 2 * nb else valid1, _dot_nt(q2, kb), -jnp.inf)
        m = jnp.max(s, axis=-1, keepdims=True)
        e = jnp.exp2(s - m)
        l = jnp.sum(e, axis=-1, keepdims=True)
        o = _dot(e.astype(BF16), vb) / l
        lse = jnp.broadcast_to(m + jnp.log2(l), (2 * nb, LANES))
        ob[br, rows(q0, nb, r), :] = jnp.where(head0, o[:nb], o[nb:])
        lb[br, rows(q0, nb, r), :] = jnp.where(head0, lse[:nb], lse[nb:])

    for br, r in enumerate(DILATIONS):
        n_blocks = S // (r * nb)
        if n_blocks == 1:
            def first_only(c, carry, br=br, r=r):
                block(br, r, c, c, nb)
                return carry
            lax.fori_loop(0, r, first_only, 0, unroll=True)
        else:
            for c in range(r):
                block(br, r, c, c, nb)

                def later(n, carry, br=br, r=r, c=c):
                    block(br, r, n * (nb * r) + c, (n - 1) * (nb * r) + c, 2 * nb)
                    return carry
                lax.fori_loop(1, n_blocks, later, 0, unroll=True)

    rows_per = 256
    for ch in range(S // rows_per):
        sl = pl.ds(ch * rows_per, rows_per)
        l0, l1, l2 = lb[0, sl, :], lb[1, sl, :], lb[2, sl, :]
        m = jnp.maximum(jnp.maximum(l0, l1), l2)
        w0, w1, w2 = jnp.exp2(l0 - m), jnp.exp2(l1 - m), jnp.exp2(l2 - m)
        o = (w0 * ob[0, sl, :] + w1 * ob[1, sl, :] + w2 * ob[2, sl, :]) / (w0 + w1 + w2)
        o_ref[sl, :] = o.astype(BF16)


def _dilated_attention(qkv, B, S):
    n_pairs = WIDTH_A // LANES
    blk = lambda off: pl.BlockSpec((None, S, LANES), lambda b, p: (b, 0, off + p))
    return pl.pallas_call(
        _dilated_body,
        grid=(B, n_pairs),
        in_specs=[blk(0), blk(n_pairs), blk(2 * n_pairs)],
        out_specs=pl.BlockSpec((None, S, LANES), lambda b, p: (b, 0, p)),
        out_shape=jax.ShapeDtypeStruct((B, S, WIDTH_A), BF16),
        scratch_shapes=[
            pltpu.VMEM((S, LANES), F32), pltpu.VMEM((S, LANES), F32), pltpu.VMEM((S, LANES), F32),
            pltpu.VMEM((len(DILATIONS), S, LANES), F32), pltpu.VMEM((len(DILATIONS), S, LANES), F32),
        ],
        compiler_params=_params(("parallel", "parallel")),
        name="dilated_attention",
    )(qkv, qkv, qkv)


def _fox_body(q_ref, k_ref, v_ref, c_ref, sel_ref, o_ref, qa_scr, ka_scr, *, tk):
    S = q_ref.shape[0]
    rows_per = 256
    lane = lax.broadcasted_iota(jnp.int32, (rows_per, LANES), 1)

    def build(i, carry):
        sl = pl.ds(i * rows_per, rows_per)
        q = q_ref[sl, :].astype(F32)
        k = k_ref[sl, :].astype(F32)
        extra = _dot(jnp.concatenate(_split3(c_ref[sl, :]), axis=1), sel_ref[...])
        for hh in range(2):
            own = (lane < HEAD_DIM) if hh == 0 else (lane >= HEAD_DIM)
            a0 = HEAD_DIM if hh == 0 else 0
            first = (lane >= a0) & (lane < a0 + 3)
            second = (lane >= a0 + 3) & (lane < a0 + 6)
            ex = extra[:, hh * LANES:(hh + 1) * LANES]
            qaug = jnp.where(own, q, jnp.where(first, ex, jnp.where(second, 1.0, 0.0)))
            kaug = jnp.where(own, k, jnp.where(second, ex, jnp.where(first, 1.0, 0.0)))
            qa_scr[hh, sl, :] = qaug.astype(BF16)
            ka_scr[hh, sl, :] = kaug.astype(BF16)
        return carry

    lax.fori_loop(0, S // rows_per, build, 0, unroll=True)

    n_blocks = S // tk
    row_t = lax.broadcasted_iota(jnp.int32, (tk, tk), 0)
    col_t = lax.broadcasted_iota(jnp.int32, (tk, tk), 1)
    causal = col_t <= row_t
    head0 = lax.broadcasted_iota(jnp.int32, (tk, LANES), 1) < HEAD_DIM
    state = [[None] * n_blocks for _ in range(2)]
    for j in range(n_blocks):
        ksl = pl.ds(j * tk, tk)
        vblk = v_ref[ksl, :]
        for hh in range(2):
            s_all = _dot_nt(qa_scr[hh, pl.ds(j * tk, S - j * tk), :], ka_scr[hh, ksl, :])
            es, scales = [], []
            for rb in range(j, n_blocks):
                s = s_all[(rb - j) * tk:(rb - j + 1) * tk, :]
                if rb == j:
                    s = jnp.where(causal, s, -jnp.inf)
                if j == 0:
                    m_new = jnp.max(s, axis=-1, keepdims=True)
                    alpha = None
                else:
                    m_old = state[hh][rb][0]
                    m_new = jnp.maximum(m_old, jnp.max(s, axis=-1, keepdims=True))
                    alpha = jnp.exp2(m_old - m_new)
                e = jnp.exp2(s - m_new)
                es.append(e.astype(BF16))
                scales.append((m_new, alpha, jnp.sum(e, axis=-1, keepdims=True)))
            pv_all = _dot(jnp.concatenate(es, axis=0) if len(es) > 1 else es[0], vblk)
            for rb in range(j, n_blocks):
                m_new, alpha, rowsum = scales[rb - j]
                pv = pv_all[(rb - j) * tk:(rb - j + 1) * tk, :]
                if alpha is None:
                    state[hh][rb] = (m_new, rowsum, pv)
                else:
                    _, l_old, acc_old = state[hh][rb]
                    state[hh][rb] = (m_new, alpha * l_old + rowsum, alpha * acc_old + pv)
        outs = [state[hh][j][2] / state[hh][j][1] for hh in range(2)]
        o_ref[pl.ds(j * tk, tk), :] = jnp.where(head0, outs[0], outs[1]).astype(BF16)


def _fox_selection(n_pairs):
    sel = np.zeros((n_pairs, 3 * LANES, 2 * LANES), np.float32)
    for p in range(n_pairs):
        for hh in range(2):
            a0 = hh * LANES + (HEAD_DIM if hh == 0 else 0)
            for piece in range(3):
                sel[p, piece * LANES + 2 * p + hh, a0 + piece] = 1.0
                sel[p, piece * LANES + 2 * p + hh, a0 + 3 + piece] = -1.0
    return jnp.asarray(sel, BF16)


def _fox_attention(qkv, cum, B, S, tk=256):
    n_pairs = WIDTH_B // LANES
    base = 3 * (WIDTH_A // LANES)
    blk = lambda off: pl.BlockSpec((None, S, LANES), lambda b, p: (b, 0, off + p))
    return pl.pallas_call(
        functools.partial(_fox_body, tk=tk),
        grid=(B, n_pairs),
        in_specs=[
            blk(base), blk(base + n_pairs), blk(base + 2 * n_pairs),
            pl.BlockSpec((None, S, LANES), lambda b, p: (b, 0, 0)),
            pl.BlockSpec((None, 3 * LANES, 2 * LANES), lambda b, p: (p, 0, 0)),
        ],
        out_specs=pl.BlockSpec((None, S, LANES), lambda b, p: (b, 0, p)),
        out_shape=jax.ShapeDtypeStruct((B, S, WIDTH_B), BF16),
        scratch_shapes=[pltpu.VMEM((2, S, LANES), BF16), pltpu.VMEM((2, S, LANES), BF16)],
        compiler_params=_params(("parallel", "parallel")),
        name="fox_attention",
    )(qkv, qkv, qkv, cum, _fox_selection(n_pairs))


def _outproj_ffn_body(x_ref, oa_ref, ob_ref, woa_ref, wob_ref, gain_ref, wg_ref, wu_ref, wd_ref, out_ref, a_scr):
    x1 = x_ref[...] + _dot(oa_ref[...], woa_ref[...]) + _dot(ob_ref[...], wob_ref[...])
    h = _rmsnorm(x1, gain_ref[...]).astype(BF16)
    for c in range(a_scr.shape[1] // MXU):
        cols = slice(c * MXU, (c + 1) * MXU)
        g = _dot(h, wg_ref[:, cols])
        u = _dot(h, wu_ref[:, cols])
        a_scr[:, cols] = (g * _sigmoid(g) * u).astype(BF16)
    out_ref[...] = x1 + _dot(a_scr[...], wd_ref[...])


def _outproj_ffn(x2d, oa, ob, woa, wob, gain, wg, wu, wd, tm):
    T = x2d.shape[0]
    F = wg.shape[1]
    once = dict(pipeline_mode=pl.Buffered(1))
    return pl.pallas_call(
        _outproj_ffn_body,
        grid=(T // tm,),
        in_specs=[
            pl.BlockSpec((tm, D_MODEL), lambda i: (i, 0)),
            pl.BlockSpec((tm, WIDTH_A), lambda i: (i, 0)),
            pl.BlockSpec((tm, WIDTH_B), lambda i: (i, 0)),
            pl.BlockSpec((WIDTH_A, D_MODEL), lambda i: (0, 0), **once),
            pl.BlockSpec((WIDTH_B, D_MODEL), lambda i: (0, 0), **once),
            pl.BlockSpec((1, D_MODEL), lambda i: (0, 0)),
            pl.BlockSpec((D_MODEL, F), lambda i: (0, 0), **once),
            pl.BlockSpec((D_MODEL, F), lambda i: (0, 0), **once),
            pl.BlockSpec((F, D_MODEL), lambda i: (0, 0), **once),
        ],
        out_specs=pl.BlockSpec((tm, D_MODEL), lambda i: (i, 0)),
        out_shape=jax.ShapeDtypeStruct((T, D_MODEL), F32),
        scratch_shapes=[pltpu.VMEM((tm, F), BF16)],
        compiler_params=_params(("parallel",)),
        name="outproj_ffn",
    )(x2d, oa, ob, woa, wob, gain, wg, wu, wd)


def _inproj_c_body(x_ref, gain_ref, w_ref, qig_ref, f_ref):
    h = _rmsnorm(x_ref[...], gain_ref[...]).astype(BF16)
    n_qig = qig_ref.shape[1]
    for c in range(n_qig // MXU):
        qig_ref[:, c * MXU:(c + 1) * MXU] = _dot(h, w_ref[:, c * MXU:(c + 1) * MXU]).astype(BF16)
    for c in range(f_ref.shape[1] // MXU):
        f_ref[:, c * MXU:(c + 1) * MXU] = _dot(h, w_ref[:, n_qig + c * MXU:n_qig + (c + 1) * MXU])


def _inproj_c(x2d, gain, w, tm):
    T = x2d.shape[0]
    return pl.pallas_call(
        _inproj_c_body,
        grid=(T // tm,),
        in_specs=[
            pl.BlockSpec((tm, D_MODEL), lambda i: (i, 0)),
            pl.BlockSpec((1, D_MODEL), lambda i: (0, 0)),
            pl.BlockSpec((D_MODEL, 4 * D_MODEL), lambda i: (0, 0)),
        ],
        out_specs=[
            pl.BlockSpec((tm, 3 * D_MODEL), lambda i: (i, 0)),
            pl.BlockSpec((tm, D_MODEL), lambda i: (i, 0)),
        ],
        out_shape=[
            jax.ShapeDtypeStruct((T, 3 * D_MODEL), BF16),
            jax.ShapeDtypeStruct((T, D_MODEL), F32),
        ],
        compiler_params=_params(("parallel",)),
        name="inproj_c",
    )(x2d, gain, w)


def _hgrn_body(q_ref, i_ref, g_ref, f_ref, lb_ref, gn_ref, o_ref, state_scr, o_scr, ops0, ops1, dec0, dec1, *, heads):
    S = q_ref.shape[0]
    C = HGRN_CHUNK
    half = C // 2
    quarter = C // 4
    assert quarter == HGRN_LEAF
    n_chunks = S // C
    row = lax.broadcasted_iota(jnp.int32, (C, C), 0)
    col = lax.broadcasted_iota(jnp.int32, (C, C), 1)
    mask_cross = (row >= half) & (col < half)
    mask_same = ((row // half) == (col // half)) & (col <= row)
    r = lax.broadcasted_iota(jnp.int32, (C, heads * HGRN_DK), 0)
    scale = HGRN_DK ** -0.5
    gn = jnp.concatenate([gn_ref[...]] * heads, axis=1)
    state_scr[...] = jnp.zeros_like(state_scr)

    def prepare(ci, ops, dec):
        sl = pl.ds(ci * C, C)
        lbv = lb_ref[...]
        f = lbv + (1.0 - lbv) * _sigmoid(f_ref[sl, :])
        k = 1.0 - f
        q = q_ref[sl, :].astype(F32) * scale
        b = _cumsum_groups(jnp.log2(f))
        b_last = b[C - 1:C, :]
        e_cross = b - b[half - 1:half, :]
        e_same = b - jnp.where(r < half, b[quarter - 1:quarter, :], b[half + quarter - 1:half + quarter, :])
        ops[0] = (q * jnp.exp2(jnp.minimum(e_cross, 0.0))).astype(BF16)
        ops[1] = (k * jnp.exp2(jnp.minimum(-e_cross, 0.0))).astype(BF16)
        ops[2] = (q * jnp.exp2(e_same)).astype(BF16)
        ops[3] = (k * jnp.exp2(-e_same)).astype(BF16)
        ops[4] = (q * jnp.exp2(b)).astype(BF16)
        ops[5] = (k * jnp.exp2(b_last - b)).astype(BF16)
        dec[0:1, :] = jnp.exp2(b_last)

    def contract(ci, ops, dec):
        sl = pl.ds(ci * C, C)
        for hd in range(heads):
            cols = slice(hd * HGRN_DK, (hd + 1) * HGRN_DK)
            v = i_ref[sl, cols]
            scores = (jnp.where(mask_cross, _dot_nt(ops[0, :, cols], ops[1, :, cols]), 0.0)
                      + jnp.where(mask_same, _dot_nt(ops[2, :, cols], ops[3, :, cols]), 0.0))
            state_t = state_scr[hd]
            o_scr[sl, cols] = _dot(scores.astype(BF16), v) + _dot_nt(ops[4, :, cols], state_t.astype(BF16))
            v_t = v.astype(F32).T.astype(BF16)
            state_scr[hd] = state_t * dec[0:1, cols] + _dot(v_t, ops[5, :, cols])

    def finish(ci):
        sl = pl.ds(ci * C, C)
        o = o_scr[sl, :]
        ys = []
        for hd in range(heads):
            oh = o[:, hd * HGRN_DK:(hd + 1) * HGRN_DK]
            ys.append(oh * lax.rsqrt(jnp.mean(oh * oh, axis=-1, keepdims=True) + EPS))
        gate = g_ref[sl, :].astype(F32)
        o_ref[sl, :] = (jnp.concatenate(ys, axis=1) * gn * (gate * _sigmoid(gate))).astype(BF16)

    prepare(0, ops0, dec0)

    def pair(j, carry):
        prepare(2 * j + 1, ops1, dec1)
        contract(2 * j, ops0, dec0)
        prepare(2 * j + 2, ops0, dec0)
        contract(2 * j + 1, ops1, dec1)
        finish(jnp.maximum(2 * j - 1, 0))
        finish(2 * j)
        return carry

    lax.fori_loop(0, n_chunks // 2 - 1, pair, 0, unroll=5)
    prepare(n_chunks - 1, ops1, dec1)
    contract(n_chunks - 2, ops0, dec0)
    contract(n_chunks - 1, ops1, dec1)
    for ci in range(n_chunks - 3, n_chunks):
        finish(ci)


def _hgrn(qig, flog, lb, gn, B, S, heads=4):
    ng = N_HEADS_C // heads
    w = heads * HGRN_DK
    blk = lambda off: pl.BlockSpec((None, S, w), lambda b, h: (b, 0, off + h))
    return pl.pallas_call(
        functools.partial(_hgrn_body, heads=heads),
        grid=(B, ng),
        in_specs=[
            blk(0), blk(ng), blk(2 * ng),
            pl.BlockSpec((None, S, w), lambda b, h: (b, 0, h)),
            pl.BlockSpec((1, w), lambda b, h: (0, h)),
            pl.BlockSpec((1, HGRN_DK), lambda b, h: (0, 0)),
        ],
        out_specs=pl.BlockSpec((None, S, w), lambda b, h: (b, 0, h)),
        out_shape=jax.ShapeDtypeStruct((B, S, D_MODEL), BF16),
        scratch_shapes=[
            pltpu.VMEM((heads, HGRN_DK, HGRN_DK), F32), pltpu.VMEM((S, w), F32),
            pltpu.VMEM((6, HGRN_CHUNK, w), BF16), pltpu.VMEM((6, HGRN_CHUNK, w), BF16),
            pltpu.VMEM((SUBLANES, w), F32), pltpu.VMEM((SUBLANES, w), F32),
        ],
        compiler_params=_params(("parallel", "parallel")),
        name="hgrn2",
    )(qig, qig, qig, flog, lb, gn)


SEG_ALIGN = SUBLANES
SEG_BITS = tuple(range(9, 2, -1))


def _local_rows(tm):
    return TOP_K * tm + N_EXPERTS * SEG_ALIGN


def _outproj_router_body(x_ref, o_ref, w_ref, gain_ref, r_ref, x3_ref, xl_ref, pos_ref, gate_ref, seg_ref, *, tm):
    for t in range(x_ref.shape[0] // tm):
        _route_tile(x_ref, o_ref, w_ref, gain_ref, r_ref, x3_ref, xl_ref, pos_ref, gate_ref, seg_ref, t, tm)


def _route_tile(x_ref, o_ref, w_ref, gain_ref, r_ref, x3_ref, xl_ref, pos_ref, gate_ref, seg_ref, t, tm):
    rows = pl.ds(t * tm, tm)
    lr = _local_rows(tm)
    x3 = x_ref[rows, :] + _dot(o_ref[rows, :], w_ref[...])
    x3_ref[rows, :] = x3
    h = _rmsnorm(x3, gain_ref[...])
    h_bf = h.astype(BF16)
    logits = _dot(h_bf, r_ref[...])
    lane = lax.broadcasted_iota(jnp.int32, logits.shape, 1)
    lane_f = lane.astype(F32)
    lg = jnp.where(lane < N_EXPERTS, logits, -jnp.inf)
    m1 = jnp.max(lg, axis=-1, keepdims=True)
    i1 = jnp.min(jnp.where(lg == m1, lane_f, float(LANES)), axis=-1, keepdims=True)
    lg2 = jnp.where(lane_f == i1, -jnp.inf, lg)
    m2 = jnp.max(lg2, axis=-1, keepdims=True)
    i2 = jnp.min(jnp.where(lg2 == m2, lane_f, float(LANES)), axis=-1, keepdims=True)
    e2 = jnp.exp(m2 - m1)
    den = 1.0 + e2
    gate_ref[rows, :] = jnp.where(lane == 0, 1.0 / den, jnp.where(lane == 1, e2 / den, 0.0))

    oh1 = (lane_f == i1).astype(F32)
    oh2 = (lane_f == i2).astype(F32)
    c1 = _cumsum_groups(oh1)
    c2 = _cumsum_groups(oh2)
    n1 = c1[tm - 1:tm, :]
    count = n1 + c2[tm - 1:tm, :]
    padded = jnp.floor((count + (SEG_ALIGN - 1.0)) * (1.0 / SEG_ALIGN)) * SEG_ALIGN
    run = jnp.broadcast_to(padded, (SUBLANES, LANES))
    lane8 = lax.broadcasted_iota(jnp.int32, (SUBLANES, LANES), 1)
    for s in (1, 2, 4):
        run = run + jnp.where(lane8 >= s, pltpu.roll(run, s, 1), 0.0)
    start = run[0:1, :] - padded
    pos1 = jnp.sum(oh1 * (start + c1 - 1.0), axis=-1, keepdims=True)
    pos2 = jnp.sum(oh2 * (start + n1 + c2 - 1.0), axis=-1, keepdims=True)
    pos_ref[rows, :] = jnp.where(lane == 0, pos1, jnp.where(lane == 1, pos2, 0.0)).astype(jnp.int32)
    pos1_row = jnp.broadcast_to(pos1, (tm, LANES)).T[0:1, :]
    pos2_row = jnp.broadcast_to(pos2, (tm, LANES)).T[0:1, :]
    slot = lax.broadcasted_iota(jnp.int32, (lr, tm), 0).astype(F32)
    perm = ((slot == pos1_row) | (slot == pos2_row)).astype(BF16)
    xl_ref[pl.ds(t * lr, lr), :] = _dot(perm, h_bf)
    r8 = lax.broadcasted_iota(jnp.int32, (SUBLANES, LANES), 0)
    seg_ref[pl.ds(t * SUBLANES, SUBLANES), :] = jnp.where(r8 == 0, count, jnp.where(r8 == 1, padded, jnp.where(r8 == 2, start, 0.0))).astype(jnp.int32)


def _outproj_router(x2d, o, w, gain, r, tm, group=2):
    T = x2d.shape[0]
    n_tt = T // tm
    lr = _local_rows(tm)
    tg = group * tm
    return pl.pallas_call(
        functools.partial(_outproj_router_body, tm=tm),
        grid=(n_tt // group,),
        in_specs=[
            pl.BlockSpec((tg, D_MODEL), lambda i: (i, 0)),
            pl.BlockSpec((tg, D_MODEL), lambda i: (i, 0)),
            pl.BlockSpec((D_MODEL, D_MODEL), lambda i: (0, 0)),
            pl.BlockSpec((1, D_MODEL), lambda i: (0, 0)),
            pl.BlockSpec((D_MODEL, LANES), lambda i: (0, 0)),
        ],
        out_specs=[
            pl.BlockSpec((tg, D_MODEL), lambda i: (i, 0)),
            pl.BlockSpec((group * lr, D_MODEL), lambda i: (i, 0)),
            pl.BlockSpec((tg, LANES), lambda i: (i, 0)),
            pl.BlockSpec((tg, LANES), lambda i: (i, 0)),
            pl.BlockSpec((group * SUBLANES, LANES), lambda i: (i, 0)),
        ],
        out_shape=[
            jax.ShapeDtypeStruct((T, D_MODEL), F32),
            jax.ShapeDtypeStruct((n_tt * lr, D_MODEL), F32),
            jax.ShapeDtypeStruct((T, LANES), jnp.int32),
            jax.ShapeDtypeStruct((T, LANES), F32),
            jax.ShapeDtypeStruct((n_tt * SUBLANES, LANES), jnp.int32),
        ],
        compiler_params=_params(("parallel",)),
        name="outproj_router",
    )(x2d, o, w, gain, r)


def _run_copies(n, src, s0, dst, d0, sem):
    out = []
    for b in SEG_BITS:
        offs = (n >> (b + 1)) << (b + 1)
        cp = pltpu.make_async_copy(src.at[pl.ds(pl.multiple_of(s0 + offs, SEG_ALIGN), 1 << b), :],
                                   dst.at[pl.ds(pl.multiple_of(d0 + offs, SEG_ALIGN), 1 << b), :], sem)
        out.append((((n >> b) & 1) == 1, cp))
    return out


def _experts_body(te_ref, nu_ref, r0_ref, jlo_ref, jhi_ref, valid_ref, cs_ref, lp_ref, src_ref,
                  xl_hbm, wg_ref, wu_ref, wd_ref, out_ref, xbuf, xb_scr, a_scr, sem, *, n_tt):
    i = pl.program_id(0)
    f = pl.program_id(1)
    tm = out_ref.shape[0]

    def move(tile, slot, wait):
        base = te_ref[tile] * n_tt
        r0 = r0_ref[tile]

        def one_run(j, carry):
            c0 = cs_ref[base + j]
            lo = jnp.maximum(c0, r0)
            hi = jnp.minimum(c0 + lp_ref[base + j], r0 + tm)
            n = jnp.maximum(hi - lo, 0)
            for cond, cp in _run_copies(n, xl_hbm, src_ref[base + j] + (lo - c0), xbuf.at[slot], lo - r0, sem.at[slot]):
                @pl.when(cond)
                def _():
                    cp.wait() if wait else cp.start()
            return carry

        lax.fori_loop(jlo_ref[tile], jhi_ref[tile], one_run, 0)

    @pl.when(f == 0)
    def _():
        @pl.when(i == 0)
        def _():
            xbuf[...] = jnp.zeros_like(xbuf)
            move(0, 0, False)

        @pl.when(i < nu_ref[0])
        def _():
            move(i, i % 2, True)

        @pl.when(i + 1 < nu_ref[0])
        def _():
            move(i + 1, (i + 1) % 2, False)

        row = lax.broadcasted_iota(jnp.int32, (tm, D_MODEL), 0)
        xb_scr[...] = jnp.where(row < valid_ref[i], xbuf[i % 2], 0.0).astype(BF16)

        @pl.when(i >= nu_ref[0])
        def _():
            out_ref[...] = jnp.zeros_like(out_ref)

    def swiglu(first):
        xb = xb_scr[...]
        for c in range(a_scr.shape[1] // MXU):
            cols = slice(c * MXU, (c + 1) * MXU)
            g = _dot(xb, wg_ref[:, cols])
            u = _dot(xb, wu_ref[:, cols])
            a_scr[:, cols] = (g * _sigmoid(g) * u).astype(BF16)
        d = _dot(a_scr[...], wd_ref[...])
        out_ref[...] = d if first else out_ref[...] + d

    @pl.when((i < nu_ref[0]) & (f == 0))
    def _():
        swiglu(True)

    @pl.when((i < nu_ref[0]) & (f != 0))
    def _():
        swiglu(False)


def _moe_experts(tabs, xl, wg, wu, wd, n_rows, n_tt, tm, tf):
    F = wg.shape[2]
    nf = F // tf

    def f_eff(i, f, nu):
        return jnp.where(i < nu[0], f, nf - 1)

    return pl.pallas_call(
        functools.partial(_experts_body, n_tt=n_tt),
        grid_spec=pltpu.PrefetchScalarGridSpec(
            num_scalar_prefetch=len(tabs),
            grid=(n_rows // tm, nf),
            in_specs=[
                pl.BlockSpec(memory_space=pl.ANY),
                pl.BlockSpec((None, D_MODEL, tf), lambda i, f, te, nu, *_: (te[i], 0, f_eff(i, f, nu))),
                pl.BlockSpec((None, D_MODEL, tf), lambda i, f, te, nu, *_: (te[i], 0, f_eff(i, f, nu))),
                pl.BlockSpec((None, tf, D_MODEL), lambda i, f, te, nu, *_: (te[i], f_eff(i, f, nu), 0)),
            ],
            out_specs=pl.BlockSpec((tm, D_MODEL), lambda i, f, *_: (i, 0)),
            scratch_shapes=[pltpu.VMEM((2, tm, D_MODEL), F32), pltpu.VMEM((tm, D_MODEL), BF16),
                            pltpu.VMEM((tm, tf), BF16), pltpu.SemaphoreType.DMA((2,))],
        ),
        out_shape=jax.ShapeDtypeStruct((n_rows, D_MODEL), F32),
        compiler_params=_params(("arbitrary", "arbitrary")),
        name="moe_experts",
    )(*tabs, xl, wg, wu, wd)


def _combine_body(row_ref, lp_ref, off_ref, x_ref, pos_ref, gate_ref, ys_hbm, gain_ref, out_ref, yl, sem):
    j = pl.program_id(0)
    tm = x_ref.shape[0]
    lr = yl.shape[1]

    def move(tile, slot, wait):
        for e in range(N_EXPERTS):
            t = tile * N_EXPERTS + e
            for cond, cp in _run_copies(lp_ref[t], ys_hbm, row_ref[t], yl.at[slot], off_ref[t], sem.at[slot]):
                @pl.when(cond)
                def _():
                    cp.wait() if wait else cp.start()

    @pl.when(j == 0)
    def _():
        yl[...] = jnp.zeros_like(yl)
        move(0, 0, False)

    move(j, j % 2, True)

    @pl.when(j + 1 < pl.num_programs(0))
    def _():
        move(j + 1, (j + 1) % 2, False)

    last = j * N_EXPERTS + N_EXPERTS - 1
    used = off_ref[last] + lp_ref[last]
    row = lax.broadcasted_iota(jnp.int32, (lr, D_MODEL), 0)
    y_sorted = jnp.where(row < used, yl[j % 2], 0.0).astype(BF16)
    pos = pos_ref[...]
    gates = gate_ref[...]
    slot = lax.broadcasted_iota(jnp.int32, (tm, lr), 1)
    weights = jnp.where(slot == pos[:, 0:1], gates[:, 0:1], jnp.where(slot == pos[:, 1:2], gates[:, 1:2], 0.0))
    y = x_ref[...] + _dot(weights.astype(BF16), y_sorted)
    out_ref[...] = _rmsnorm(y, gain_ref[...])


def _moe_combine(tabs, x3, pos, gates, ys, gain, tm):
    T = x3.shape[0]
    return pl.pallas_call(
        _combine_body,
        grid_spec=pltpu.PrefetchScalarGridSpec(
            num_scalar_prefetch=len(tabs),
            grid=(T // tm,),
            in_specs=[
                pl.BlockSpec((tm, D_MODEL), lambda i, *_: (i, 0)),
                pl.BlockSpec((tm, LANES), lambda i, *_: (i, 0)),
                pl.BlockSpec((tm, LANES), lambda i, *_: (i, 0)),
                pl.BlockSpec(memory_space=pl.ANY),
                pl.BlockSpec((1, D_MODEL), lambda i, *_: (0, 0)),
            ],
            out_specs=pl.BlockSpec((tm, D_MODEL), lambda i, *_: (i, 0)),
            scratch_shapes=[pltpu.VMEM((2, _local_rows(tm), D_MODEL), F32), pltpu.SemaphoreType.DMA((2,))],
        ),
        out_shape=jax.ShapeDtypeStruct((T, D_MODEL), F32),
        compiler_params=_params(("arbitrary",)),
        name="moe_combine",
    )(*tabs, x3, pos, gates, ys, gain)


def _routing_tables(seg, n_tt, tm, tm_e):
    seg = seg.reshape(n_tt, SUBLANES, LANES)
    lp = seg[:, 1, :N_EXPERTS]
    off = seg[:, 2, :N_EXPERTS]
    cs = jnp.cumsum(lp, axis=0) - lp
    total = jnp.sum(lp, axis=0)
    padded = ((total + tm_e - 1) // tm_e) * tm_e
    ends = jnp.cumsum(padded)
    starts = ends - padded
    n_rows = -(-(TOP_K * n_tt * tm + N_EXPERTS * (SEG_ALIGN - 1) * n_tt) // tm_e) * tm_e + N_EXPERTS * tm_e
    tile_start = jnp.arange(n_rows // tm_e, dtype=jnp.int32) * tm_e
    te = jnp.minimum(jnp.sum((tile_start[:, None] >= ends[None, :]).astype(jnp.int32), axis=1), N_EXPERTS - 1)
    n_used = (ends[-1] // tm_e).astype(jnp.int32).reshape(1)
    r0 = tile_start - starts[te]
    cs_t = cs[:, te]
    run_end_t = cs_t + lp[:, te]
    jlo = jnp.sum((run_end_t <= r0[None, :]).astype(jnp.int32), axis=0)
    jhi = jnp.sum((cs_t < (r0 + tm_e)[None, :]).astype(jnp.int32), axis=0)
    valid = jnp.clip(total[te] - r0, 0, tm_e)
    src = jnp.arange(n_tt, dtype=jnp.int32)[:, None] * _local_rows(tm) + off
    i32 = lambda a: a.astype(jnp.int32)
    expert_tabs = (i32(te), n_used, i32(r0), i32(jlo), i32(jhi), i32(valid),
                   i32(cs.T.reshape(-1)), i32(lp.T.reshape(-1)), i32(src.T.reshape(-1)))
    combine_tabs = (i32((starts[None, :] + cs).reshape(-1)), i32(lp.reshape(-1)), i32(off.reshape(-1)))
    return expert_tabs, combine_tabs, n_rows


def _rope_tables(positions):
    B, S = positions.shape
    inv_freq = jnp.power(jnp.float32(ROPE_THETA), -jnp.arange(ROPE_HALF, dtype=F32) / ROPE_HALF)
    ang = positions.astype(F32)[..., None] * inv_freq
    cos, sin = jnp.cos(ang), jnp.sin(ang)
    rest = HEAD_DIM - ROPE_DIM
    cos_h = jnp.concatenate([cos, cos, jnp.ones((B, S, rest), F32)], axis=-1)
    sin_h = jnp.concatenate([-sin, sin, jnp.zeros((B, S, rest), F32)], axis=-1)
    reps = LANES // HEAD_DIM
    return (jnp.tile(cos_h, (1, 1, reps)).reshape(B * S, LANES),
            jnp.tile(sin_h, (1, 1, reps)).reshape(B * S, LANES))


def kernel(x, positions, norm_mix, norm_ffn, w_in_ab, fgate_bias, w_out_ab, w_in_c, lower_bounds, gnorm_c, w_out_c,
           w_gate_ffn, w_up_ffn, w_down_ffn, router, w_gate_moe, w_up_moe, w_down_moe, norm_final):
    B, S, D = x.shape
    T = B * S
    assert D == D_MODEL and S % PROJ_ROWS == 0
    tm_proj, tm, tm_e = PROJ_ROWS, ROUTE_ROWS, EXPERT_ROWS
    x2d = x.reshape(T, D)

    w_ab = jnp.pad(w_in_ab[0], ((0, 0), (0, QKV_WIDTH + LANES - w_in_ab.shape[2]))).astype(BF16)
    cos, sin = _rope_tables(positions)
    bias = jnp.pad(fgate_bias[0], (0, LANES - N_HEADS_B)).reshape(1, LANES)
    qkv, cum = _inproj_ab(x2d, norm_mix[0:1], w_ab, cos, sin, bias, tm_proj, S)
    qkv = qkv.reshape(B, S, QKV_WIDTH)
    out_a = _dilated_attention(qkv, B, S).reshape(T, WIDTH_A)
    out_b = _fox_attention(qkv, cum.reshape(B, S, LANES), B, S).reshape(T, WIDTH_B)
    w_o = w_out_ab[0].astype(BF16)
    x2 = _outproj_ffn(x2d, out_a, out_b, w_o[:WIDTH_A], w_o[WIDTH_A:], norm_ffn[0:1],
                      w_gate_ffn[0].astype(BF16), w_up_ffn[0].astype(BF16), w_down_ffn[0].astype(BF16), tm_proj)

    lb_all = jnp.cumsum(jax.nn.softmax(lower_bounds.astype(F32), axis=0), axis=0)
    lb = (lb_all - lb_all[0:1])[1].reshape(1, D)
    wq, wf, wi, wg = jnp.split(w_in_c[0], 4, axis=-1)
    w_c = jnp.concatenate([wq, wi, wg, wf], axis=-1).astype(BF16)
    qig, flog_c = _inproj_c(x2, norm_mix[1:2], w_c, tm_proj)
    o_c = _hgrn(qig.reshape(B, S, 3 * D), flog_c.reshape(B, S, D), lb, gnorm_c[0:1], B, S).reshape(T, D)

    r_pad = jnp.pad(router[0], ((0, 0), (0, LANES - N_EXPERTS))).astype(BF16)
    x3, xl, pos, gates, seg = _outproj_router(x2, o_c, w_out_c[0].astype(BF16), norm_ffn[1:2], r_pad, tm)
    expert_tabs, combine_tabs, n_rows = _routing_tables(seg, T // tm, tm, tm_e)
    ys = _moe_experts(expert_tabs, xl, w_gate_moe[0].astype(BF16), w_up_moe[0].astype(BF16),
                      w_down_moe[0].astype(BF16), n_rows, T // tm, tm_e, w_gate_moe.shape[3] // 2)
    out = _moe_combine(combine_tabs, x3, pos, gates, ys, norm_final.reshape(1, D), tm)
    return out.reshape(B, S, D)
```

```python
import functools

import jax
import jax.numpy as jnp
import numpy as np
from jax import lax
from jax.experimental import pallas as pl
from jax.experimental.pallas import tpu as pltpu

F32 = jnp.float32
BF16 = jnp.bfloat16

D_MODEL = 1024
HEAD_DIM = 64
N_HEADS_A = 8
N_HEADS_B = 8
WIDTH_A = N_HEADS_A * HEAD_DIM
WIDTH_B = N_HEADS_B * HEAD_DIM
QKV_WIDTH = 3 * (WIDTH_A + WIDTH_B)
ROPE_THETA = 500000.0
ROPE_DIM = HEAD_DIM // 4
ROPE_HALF = ROPE_DIM // 2
ATT_BLOCK = 128
DILATIONS = (1, 4, 16)
N_HEADS_C = 8
HGRN_DK = 128
HGRN_CHUNK = 64
HGRN_LEAF = 16
N_EXPERTS = 8
TOP_K = 2
EPS = 1e-6
LOG2E = 1.4426950408889634

LANES = 128
SUBLANES = 8
MXU = 256
VMEM_LIMIT = 56 * 1024 * 1024
PROJ_ROWS = 1024
ROUTE_ROWS = 512
EXPERT_ROWS = 1024

NT_DIMS = (((1,), (1,)), ((), ()))


def _params(semantics, **kw):
    return pltpu.CompilerParams(dimension_semantics=semantics, vmem_limit_bytes=VMEM_LIMIT, **kw)


def _rmsnorm(x, gain):
    return x * lax.rsqrt(jnp.mean(x * x, axis=-1, keepdims=True) + EPS) * gain


def _sigmoid(x):
    return 1.0 / (1.0 + jnp.exp(-x))


def _split3(x):
    hi = x.astype(BF16)
    r1 = x - hi.astype(F32)
    mid = r1.astype(BF16)
    lo = (r1 - mid.astype(F32)).astype(BF16)
    return hi, mid, lo


def _dot(a, b):
    return jnp.dot(a, b, preferred_element_type=F32)


def _dot_nt(a, b):
    return lax.dot_general(a, b, NT_DIMS, preferred_element_type=F32)


def _cumsum_groups(x):
    n, w = x.shape
    rows = lax.broadcasted_iota(jnp.int32, (SUBLANES, w), 0)
    out, carry = [], None
    for g in range(n // SUBLANES):
        xg = x[SUBLANES * g:SUBLANES * (g + 1), :]
        for s in (1, 2, 4):
            xg = xg + jnp.where(rows >= s, pltpu.roll(xg, s, 0), 0.0)
        if carry is not None:
            xg = xg + carry
        carry = xg[SUBLANES - 1:SUBLANES, :]
        out.append(xg)
    return jnp.concatenate(out, axis=0)


def _inproj_ab_body(x_ref, gain_ref, w_ref, pos_ref, freq_ref, sign_ref, bias_ref, qkv_ref, cum_ref, carry_scr, *, tiles_per_seq):
    h = _rmsnorm(x_ref[...], gain_ref[...]).astype(BF16)

    @pl.when(pl.program_id(0) % tiles_per_seq == 0)
    def _():
        carry_scr[...] = jnp.zeros_like(carry_scr)

    x = _dot(h, w_ref[:, QKV_WIDTH:QKV_WIDTH + LANES]) + bias_ref[...]
    logf = -(jnp.maximum(-x, 0.0) + jnp.log1p(jnp.exp(-jnp.abs(x))))
    cum = _cumsum_groups(logf * LOG2E) + carry_scr[0:1, :]
    cum_ref[...] = cum
    carry_scr[0:1, :] = cum[cum.shape[0] - 1:, :]

    ang = pos_ref[...].astype(F32) * freq_ref[...]
    cos = jnp.cos(ang)
    sin = jnp.sin(ang) * sign_ref[...]
    lane = lax.broadcasted_iota(jnp.int32, cos.shape, 1)
    low = (lane & (HEAD_DIM - 1)) < ROPE_HALF
    scale = HEAD_DIM ** -0.5 * LOG2E
    for c in range(QKV_WIDTH // MXU):
        y = _dot(h, w_ref[:, c * MXU:(c + 1) * MXU])
        seg = c // 2
        for s in range(2):
            yy = y[:, s * LANES:(s + 1) * LANES]
            if seg in (0, 1):
                partner = jnp.where(low, pltpu.roll(yy, LANES - ROPE_HALF, 1), pltpu.roll(yy, ROPE_HALF, 1))
                yy = yy * cos + partner * sin
            if seg in (0, 3):
                yy = yy * scale
            qkv_ref[:, c * MXU + s * LANES:c * MXU + (s + 1) * LANES] = yy.astype(BF16)


def _inproj_ab(x2d, gain, w, pos, freq, sign, bias, tm, seq_len):
    T = x2d.shape[0]
    wn = w.shape[1]
    return pl.pallas_call(
        functools.partial(_inproj_ab_body, tiles_per_seq=seq_len // tm),
        grid=(T // tm,),
        in_specs=[
            pl.BlockSpec((tm, D_MODEL), lambda i: (i, 0)),
            pl.BlockSpec((1, D_MODEL), lambda i: (0, 0)),
            pl.BlockSpec((D_MODEL, wn), lambda i: (0, 0)),
            pl.BlockSpec((tm, 1), lambda i: (i, 0)),
            pl.BlockSpec((1, LANES), lambda i: (0, 0)),
            pl.BlockSpec((1, LANES), lambda i: (0, 0)),
            pl.BlockSpec((1, LANES), lambda i: (0, 0)),
        ],
        out_specs=[
            pl.BlockSpec((tm, QKV_WIDTH), lambda i: (i, 0)),
            pl.BlockSpec((tm, LANES), lambda i: (i, 0)),
        ],
        out_shape=[
            jax.ShapeDtypeStruct((T, QKV_WIDTH), BF16),
            jax.ShapeDtypeStruct((T, LANES), F32),
        ],
        scratch_shapes=[pltpu.VMEM((SUBLANES, LANES), F32)],
        compiler_params=_params(("arbitrary",)),
        name="inproj_ab",
    )(x2d, gain, w, pos, freq, sign, bias)


def _dilated_body(q_ref, k_ref, v_ref, o_ref, qf, kf, vf, ob, lb):
    S = q_ref.shape[0]
    nb = ATT_BLOCK
    qf[...] = q_ref[...].astype(F32)
    kf[...] = k_ref[...].astype(F32)
    vf[...] = v_ref[...].astype(F32)
    head0 = lax.broadcasted_iota(jnp.int32, (nb, LANES), 1) < HEAD_DIM
    qi2 = lax.broadcasted_iota(jnp.int32, (2 * nb, 2 * nb), 0) & (nb - 1)
    kj2 = lax.broadcasted_iota(jnp.int32, (2 * nb, 2 * nb), 1)
    valid2 = (kj2 >= qi2) & (kj2 <= qi2 + nb)
    qi1 = lax.broadcasted_iota(jnp.int32, (2 * nb, nb), 0) & (nb - 1)
    kj1 = lax.broadcasted_iota(jnp.int32, (2 * nb, nb), 1)
    valid1 = kj1 <= qi1

    def rows(start, size, r):
        return pl.ds(start, size) if r == 1 else pl.ds(start, size, stride=r)

    def block(br, r, q0, k0, nk):
        qs = qf[rows(q0, nb, r), :]
        kb = kf[rows(k0, nk, r), :].astype(BF16)
        vb = vf[rows(k0, nk, r), :].astype(BF16)
        q2 = jnp.concatenate([jnp.where(head0, qs, 0.0), jnp.where(head0, 0.0, qs)], axis=0).astype(BF16)
        s = jnp.where(valid2 if nk == 2 * nb else valid1, _dot_nt(q2, kb), -jnp.inf)
        m = jnp.max(s, axis=-1, keepdims=True)
        e = jnp.exp2(s - m)
        l = jnp.sum(e, axis=-1, keepdims=True)
        o = _dot(e.astype(BF16), vb) / l
        lse = jnp.broadcast_to(m + jnp.log2(l), (2 * nb, LANES))
        ob[br, rows(q0, nb, r), :] = jnp.where(head0, o[:nb], o[nb:])
        lb[br, rows(q0, nb, r), :] = jnp.where(head0, lse[:nb], lse[nb:])

    for br, r in enumerate(DILATIONS):
        n_blocks = S // (r * nb)
        if n_blocks == 1:
            def first_only(c, carry, br=br, r=r):
                block(br, r, c, c, nb)
                return carry
            lax.fori_loop(0, r, first_only, 0, unroll=True)
        else:
            for c in range(r):
                block(br, r, c, c, nb)

                def later(n, carry, br=br, r=r, c=c):
                    block(br, r, n * (nb * r) + c, (n - 1) * (nb * r) + c, 2 * nb)
                    return carry
                lax.fori_loop(1, n_blocks, later, 0, unroll=True)

    rows_per = 256
    for ch in range(S // rows_per):
        sl = pl.ds(ch * rows_per, rows_per)
        l0, l1, l2 = lb[0, sl, :], lb[1, sl, :], lb[2, sl, :]
        m = jnp.maximum(jnp.maximum(l0, l1), l2)
        w0, w1, w2 = jnp.exp2(l0 - m), jnp.exp2(l1 - m), jnp.exp2(l2 - m)
        o = (w0 * ob[0, sl, :] + w1 * ob[1, sl, :] + w2 * ob[2, sl, :]) / (w0 + w1 + w2)
        o_ref[sl, :] = o.astype(BF16)


def _dilated_attention(qkv, B, S):
    n_pairs = WIDTH_A // LANES
    blk = lambda off: pl.BlockSpec((None, S, LANES), lambda b, p: (b, 0, off + p))
    return pl.pallas_call(
        _dilated_body,
        grid=(B, n_pairs),
        in_specs=[blk(0), blk(n_pairs), blk(2 * n_pairs)],
        out_specs=pl.BlockSpec((None, S, LANES), lambda b, p: (b, 0, p)),
        out_shape=jax.ShapeDtypeStruct((B, S, WIDTH_A), BF16),
        scratch_shapes=[
            pltpu.VMEM((S, LANES), F32), pltpu.VMEM((S, LANES), F32), pltpu.VMEM((S, LANES), F32),
            pltpu.VMEM((len(DILATIONS), S, LANES), F32), pltpu.VMEM((len(DILATIONS), S, LANES), F32),
        ],
        compiler_params=_params(("parallel", "parallel")),
        name="dilated_attention",
    )(qkv, qkv, qkv)


def _fox_body(q_ref, k_ref, v_ref, c_ref, sel_ref, o_ref, qa_scr, ka_scr, *, tk):
    S = q_ref.shape[0]
    rows_per = 256
    lane = lax.broadcasted_iota(jnp.int32, (rows_per, LANES), 1)

    def build(i, carry):
        sl = pl.ds(i * rows_per, rows_per)
        q = q_ref[sl, :].astype(F32)
        k = k_ref[sl, :].astype(F32)
        extra = _dot(jnp.concatenate(_split3(c_ref[sl, :]), axis=1), sel_ref[...])
        for hh in range(2):
            own = (lane < HEAD_DIM) if hh == 0 else (lane >= HEAD_DIM)
            a0 = HEAD_DIM if hh == 0 else 0
            first = (lane >= a0) & (lane < a0 + 3)
            second = (lane >= a0 + 3) & (lane < a0 + 6)
            ex = extra[:, hh * LANES:(hh + 1) * LANES]
            qaug = jnp.where(own, q, jnp.where(first, ex, jnp.where(second, 1.0, 0.0)))
            kaug = jnp.where(own, k, jnp.where(second, ex, jnp.where(first, 1.0, 0.0)))
            qa_scr[hh, sl, :] = qaug.astype(BF16)
            ka_scr[hh, sl, :] = kaug.astype(BF16)
        return carry

    lax.fori_loop(0, S // rows_per, build, 0, unroll=True)

    n_blocks = S // tk
    row_t = lax.broadcasted_iota(jnp.int32, (tk, tk), 0)
    col_t = lax.broadcasted_iota(jnp.int32, (tk, tk), 1)
    causal = col_t <= row_t
    head0 = lax.broadcasted_iota(jnp.int32, (tk, LANES), 1) < HEAD_DIM
    state = [[None] * n_blocks for _ in range(2)]
    for j in range(n_blocks):
        ksl = pl.ds(j * tk, tk)
        vblk = v_ref[ksl, :]
        for hh in range(2):
            s_all = _dot_nt(qa_scr[hh, pl.ds(j * tk, S - j * tk), :], ka_scr[hh, ksl, :])
            es, scales = [], []
            for rb in range(j, n_blocks):
                s = s_all[(rb - j) * tk:(rb - j + 1) * tk, :]
                if rb == j:
                    s = jnp.where(causal, s, -jnp.inf)
                if j == 0:
                    m_new = jnp.max(s, axis=-1, keepdims=True)
                    alpha = None
                else:
                    m_old = state[hh][rb][0]
                    m_new = jnp.maximum(m_old, jnp.max(s, axis=-1, keepdims=True))
                    alpha = jnp.exp2(m_old - m_new)
                e = jnp.exp2(s - m_new)
                es.append(e.astype(BF16))
                scales.append((m_new, alpha, jnp.sum(e, axis=-1, keepdims=True)))
            pv_all = _dot(jnp.concatenate(es, axis=0) if len(es) > 1 else es[0], vblk)
            for rb in range(j, n_blocks):
                m_new, alpha, rowsum = scales[rb - j]
                pv = pv_all[(rb - j) * tk:(rb - j + 1) * tk, :]
                if alpha is None:
                    state[hh][rb] = (m_new, rowsum, pv)
                else:
                    _, l_old, acc_old = state[hh][rb]
                    state[hh][rb] = (m_new, alpha * l_old + rowsum, alpha * acc_old + pv)
        outs = [state[hh][j][2] / state[hh][j][1] for hh in range(2)]
        o_ref[pl.ds(j * tk, tk), :] = jnp.where(head0, outs[0], outs[1]).astype(BF16)


def _fox_selection(n_pairs):
    sel = np.zeros((n_pairs, 3 * LANES, 2 * LANES), np.float32)
    for p in range(n_pairs):
        for hh in range(2):
            a0 = hh * LANES + (HEAD_DIM if hh == 0 else 0)
            for piece in range(3):
                sel[p, piece * LANES + 2 * p + hh, a0 + piece] = 1.0
                sel[p, piece * LANES + 2 * p + hh, a0 + 3 + piece] = -1.0
    return jnp.asarray(sel, BF16)


def _fox_attention(qkv, cum, B, S, tk=256):
    n_pairs = WIDTH_B // LANES
    base = 3 * (WIDTH_A // LANES)
    blk = lambda off: pl.BlockSpec((None, S, LANES), lambda b, p: (b, 0, off + p))
    return pl.pallas_call(
        functools.partial(_fox_body, tk=tk),
        grid=(B, n_pairs),
        in_specs=[
            blk(base), blk(base + n_pairs), blk(base + 2 * n_pairs),
            pl.BlockSpec((None, S, LANES), lambda b, p: (b, 0, 0)),
            pl.BlockSpec((None, 3 * LANES, 2 * LANES), lambda b, p: (p, 0, 0)),
        ],
        out_specs=pl.BlockSpec((None, S, LANES), lambda b, p: (b, 0, p)),
        out_shape=jax.ShapeDtypeStruct((B, S, WIDTH_B), BF16),
        scratch_shapes=[pltpu.VMEM((2, S, LANES), BF16), pltpu.VMEM((2, S, LANES), BF16)],
        compiler_params=_params(("parallel", "parallel")),
        name="fox_attention",
    )(qkv, qkv, qkv, cum, _fox_selection(n_pairs))


def _outproj_ffn_body(x_ref, oa_ref, ob_ref, woa_ref, wob_ref, gain_ref, wg_ref, wu_ref, wd_ref, out_ref, a_scr):
    x1 = x_ref[...] + _dot(oa_ref[...], woa_ref[...]) + _dot(ob_ref[...], wob_ref[...])
    h = _rmsnorm(x1, gain_ref[...]).astype(BF16)
    for c in range(a_scr.shape[1] // MXU):
        cols = slice(c * MXU, (c + 1) * MXU)
        g = _dot(h, wg_ref[:, cols])
        u = _dot(h, wu_ref[:, cols])
        a_scr[:, cols] = (g * _sigmoid(g) * u).astype(BF16)
    out_ref[...] = x1 + _dot(a_scr[...], wd_ref[...])


def _outproj_ffn(x2d, oa, ob, woa, wob, gain, wg, wu, wd, tm):
    T = x2d.shape[0]
    F = wg.shape[1]
    once = dict(pipeline_mode=pl.Buffered(1))
    return pl.pallas_call(
        _outproj_ffn_body,
        grid=(T // tm,),
        in_specs=[
            pl.BlockSpec((tm, D_MODEL), lambda i: (i, 0)),
            pl.BlockSpec((tm, WIDTH_A), lambda i: (i, 0)),
            pl.BlockSpec((tm, WIDTH_B), lambda i: (i, 0)),
            pl.BlockSpec((WIDTH_A, D_MODEL), lambda i: (0, 0), **once),
            pl.BlockSpec((WIDTH_B, D_MODEL), lambda i: (0, 0), **once),
            pl.BlockSpec((1, D_MODEL), lambda i: (0, 0)),
            pl.BlockSpec((D_MODEL, F), lambda i: (0, 0), **once),
            pl.BlockSpec((D_MODEL, F), lambda i: (0, 0), **once),
            pl.BlockSpec((F, D_MODEL), lambda i: (0, 0), **once),
        ],
        out_specs=pl.BlockSpec((tm, D_MODEL), lambda i: (i, 0)),
        out_shape=jax.ShapeDtypeStruct((T, D_MODEL), F32),
        scratch_shapes=[pltpu.VMEM((tm, F), BF16)],
        compiler_params=_params(("parallel",)),
        name="outproj_ffn",
    )(x2d, oa, ob, woa, wob, gain, wg, wu, wd)


def _inproj_c_body(x_ref, gain_ref, w_ref, qig_ref, f_ref):
    h = _rmsnorm(x_ref[...], gain_ref[...]).astype(BF16)
    n_qig = qig_ref.shape[1]
    for c in range(n_qig // MXU):
        qig_ref[:, c * MXU:(c + 1) * MXU] = _dot(h, w_ref[:, c * MXU:(c + 1) * MXU]).astype(BF16)
    for c in range(f_ref.shape[1] // MXU):
        f_ref[:, c * MXU:(c + 1) * MXU] = _dot(h, w_ref[:, n_qig + c * MXU:n_qig + (c + 1) * MXU])


def _inproj_c(x2d, gain, w, tm):
    T = x2d.shape[0]
    return pl.pallas_call(
        _inproj_c_body,
        grid=(T // tm,),
        in_specs=[
            pl.BlockSpec((tm, D_MODEL), lambda i: (i, 0)),
            pl.BlockSpec((1, D_MODEL), lambda i: (0, 0)),
            pl.BlockSpec((D_MODEL, 4 * D_MODEL), lambda i: (0, 0)),
        ],
        out_specs=[
            pl.BlockSpec((tm, 3 * D_MODEL), lambda i: (i, 0)),
            pl.BlockSpec((tm, D_MODEL), lambda i: (i, 0)),
        ],
        out_shape=[
            jax.ShapeDtypeStruct((T, 3 * D_MODEL), BF16),
            jax.ShapeDtypeStruct((T, D_MODEL), F32),
        ],
        compiler_params=_params(("parallel",)),
        name="inproj_c",
    )(x2d, gain, w)


def _hgrn_body(q_ref, i_ref, g_ref, f_ref, lb_ref, gn_ref, o_ref, state_scr, o_scr, ops0, ops1, dec0, dec1, *, heads):
    S = q_ref.shape[0]
    C = HGRN_CHUNK
    half = C // 2
    quarter = C // 4
    assert quarter == HGRN_LEAF
    n_chunks = S // C
    row = lax.broadcasted_iota(jnp.int32, (C, C), 0)
    col = lax.broadcasted_iota(jnp.int32, (C, C), 1)
    mask_cross = (row >= half) & (col < half)
    mask_same = ((row // half) == (col // half)) & (col <= row)
    r = lax.broadcasted_iota(jnp.int32, (C, heads * HGRN_DK), 0)
    scale = HGRN_DK ** -0.5
    gn = jnp.concatenate([gn_ref[...]] * heads, axis=1)
    state_scr[...] = jnp.zeros_like(state_scr)

    def prepare(ci, ops, dec):
        sl = pl.ds(ci * C, C)
        lbv = lb_ref[...]
        f = lbv + (1.0 - lbv) * _sigmoid(f_ref[sl, :])
        k = 1.0 - f
        q = q_ref[sl, :].astype(F32) * scale
        b = _cumsum_groups(jnp.log2(f))
        b_last = b[C - 1:C, :]
        e_cross = b - b[half - 1:half, :]
        e_same = b - jnp.where(r < half, b[quarter - 1:quarter, :], b[half + quarter - 1:half + quarter, :])
        ops[0] = (q * jnp.exp2(jnp.minimum(e_cross, 0.0))).astype(BF16)
        ops[1] = (k * jnp.exp2(jnp.minimum(-e_cross, 0.0))).astype(BF16)
        ops[2] = (q * jnp.exp2(e_same)).astype(BF16)
        ops[3] = (k * jnp.exp2(-e_same)).astype(BF16)
        ops[4] = (q * jnp.exp2(b)).astype(BF16)
        ops[5] = (k * jnp.exp2(b_last - b)).astype(BF16)
        dec[0:1, :] = jnp.exp2(b_last)

    def contract(ci, ops, dec):
        sl = pl.ds(ci * C, C)
        for hd in range(heads):
            cols = slice(hd * HGRN_DK, (hd + 1) * HGRN_DK)
            v = i_ref[sl, cols]
            scores = (jnp.where(mask_cross, _dot_nt(ops[0, :, cols], ops[1, :, cols]), 0.0)
                      + jnp.where(mask_same, _dot_nt(ops[2, :, cols], ops[3, :, cols]), 0.0))
            state_t = state_scr[hd]
            o_scr[sl, cols] = _dot(scores.astype(BF16), v) + _dot_nt(ops[4, :, cols], state_t.astype(BF16))
            v_t = v.astype(F32).T.astype(BF16)
            state_scr[hd] = state_t * dec[0:1, cols] + _dot(v_t, ops[5, :, cols])

    def finish(ci):
        sl = pl.ds(ci * C, C)
        o = o_scr[sl, :]
        ys = []
        for hd in range(heads):
            oh = o[:, hd * HGRN_DK:(hd + 1) * HGRN_DK]
            ys.append(oh * lax.rsqrt(jnp.mean(oh * oh, axis=-1, keepdims=True) + EPS))
        gate = g_ref[sl, :].astype(F32)
        o_ref[sl, :] = (jnp.concatenate(ys, axis=1) * gn * (gate * _sigmoid(gate))).astype(BF16)

    prepare(0, ops0, dec0)

    def pair(j, carry):
        prepare(2 * j + 1, ops1, dec1)
        contract(2 * j, ops0, dec0)
        prepare(2 * j + 2, ops0, dec0)
        contract(2 * j + 1, ops1, dec1)
        finish(jnp.maximum(2 * j - 1, 0))
        finish(2 * j)
        return carry

    lax.fori_loop(0, n_chunks // 2 - 1, pair, 0, unroll=5)
    prepare(n_chunks - 1, ops1, dec1)
    contract(n_chunks - 2, ops0, dec0)
    contract(n_chunks - 1, ops1, dec1)
    for ci in range(n_chunks - 3, n_chunks):
        finish(ci)


def _hgrn(qig, flog, lb, gn, B, S, heads=4):
    ng = N_HEADS_C // heads
    w = heads * HGRN_DK
    blk = lambda off: pl.BlockSpec((None, S, w), lambda b, h: (b, 0, off + h))
    return pl.pallas_call(
        functools.partial(_hgrn_body, heads=heads),
        grid=(B, ng),
        in_specs=[
            blk(0), blk(ng), blk(2 * ng),
            pl.BlockSpec((None, S, w), lambda b, h: (b, 0, h)),
            pl.BlockSpec((1, w), lambda b, h: (0, h)),
            pl.BlockSpec((1, HGRN_DK), lambda b, h: (0, 0)),
        ],
        out_specs=pl.BlockSpec((None, S, w), lambda b, h: (b, 0, h)),
        out_shape=jax.ShapeDtypeStruct((B, S, D_MODEL), BF16),
        scratch_shapes=[
            pltpu.VMEM((heads, HGRN_DK, HGRN_DK), F32), pltpu.VMEM((S, w), F32),
            pltpu.VMEM((6, HGRN_CHUNK, w), BF16), pltpu.VMEM((6, HGRN_CHUNK, w), BF16),
            pltpu.VMEM((SUBLANES, w), F32), pltpu.VMEM((SUBLANES, w), F32),
        ],
        compiler_params=_params(("parallel", "parallel")),
        name="hgrn2",
    )(qig, qig, qig, flog, lb, gn)


SEG_ALIGN = SUBLANES
SEG_BITS = tuple(range(9, 2, -1))


def _local_rows(tm):
    return TOP_K * tm + N_EXPERTS * SEG_ALIGN


def _outproj_router_body(x_ref, o_ref, w_ref, gain_ref, r_ref, x3_ref, xl_ref, pos_ref, gate_ref, seg_ref, *, tm):
    for t in range(x_ref.shape[0] // tm):
        _route_tile(x_ref, o_ref, w_ref, gain_ref, r_ref, x3_ref, xl_ref, pos_ref, gate_ref, seg_ref, t, tm)


def _route_tile(x_ref, o_ref, w_ref, gain_ref, r_ref, x3_ref, xl_ref, pos_ref, gate_ref, seg_ref, t, tm):
    rows = pl.ds(t * tm, tm)
    lr = _local_rows(tm)
    x3 = x_ref[rows, :] + _dot(o_ref[rows, :], w_ref[...])
    x3_ref[rows, :] = x3
    h = _rmsnorm(x3, gain_ref[...])
    h_bf = h.astype(BF16)
    logits = _dot(h_bf, r_ref[...])
    lane = lax.broadcasted_iota(jnp.int32, logits.shape, 1)
    lane_f = lane.astype(F32)
    lg = jnp.where(lane < N_EXPERTS, logits, -jnp.inf)
    m1 = jnp.max(lg, axis=-1, keepdims=True)
    i1 = jnp.min(jnp.where(lg == m1, lane_f, float(LANES)), axis=-1, keepdims=True)
    lg2 = jnp.where(lane_f == i1, -jnp.inf, lg)
    m2 = jnp.max(lg2, axis=-1, keepdims=True)
    i2 = jnp.min(jnp.where(lg2 == m2, lane_f, float(LANES)), axis=-1, keepdims=True)
    e2 = jnp.exp(m2 - m1)
    den = 1.0 + e2
    gate_ref[rows, :] = jnp.where(lane == 0, 1.0 / den, jnp.where(lane == 1, e2 / den, 0.0))

    oh1 = (lane_f == i1).astype(F32)
    oh2 = (lane_f == i2).astype(F32)
    c1 = _cumsum_groups(oh1)
    c2 = _cumsum_groups(oh2)
    n1 = c1[tm - 1:tm, :]
    count = n1 + c2[tm - 1:tm, :]
    padded = jnp.floor((count + (SEG_ALIGN - 1.0)) * (1.0 / SEG_ALIGN)) * SEG_ALIGN
    run = jnp.broadcast_to(padded, (SUBLANES, LANES))
    lane8 = lax.broadcasted_iota(jnp.int32, (SUBLANES, LANES), 1)
    for s in (1, 2, 4):
        run = run + jnp.where(lane8 >= s, pltpu.roll(run, s, 1), 0.0)
    start = run[0:1, :] - padded
    pos1 = jnp.sum(oh1 * (start + c1 - 1.0), axis=-1, keepdims=True)
    pos2 = jnp.sum(oh2 * (start + n1 + c2 - 1.0), axis=-1, keepdims=True)
    pos_ref[rows, :] = jnp.where(lane == 0, pos1, jnp.where(lane == 1, pos2, 0.0)).astype(jnp.int32)
    pos1_row = jnp.broadcast_to(pos1, (tm, LANES)).T[0:1, :]
    pos2_row = jnp.broadcast_to(pos2, (tm, LANES)).T[0:1, :]
    slot = lax.broadcasted_iota(jnp.int32, (lr, tm), 0).astype(F32)
    perm = ((slot == pos1_row) | (slot == pos2_row)).astype(BF16)
    xl_ref[pl.ds(t * lr, lr), :] = _dot(perm, h_bf)
    r8 = lax.broadcasted_iota(jnp.int32, (SUBLANES, LANES), 0)
    seg_ref[pl.ds(t * SUBLANES, SUBLANES), :] = jnp.where(r8 == 0, count, jnp.where(r8 == 1, padded, jnp.where(r8 == 2, start, 0.0))).astype(jnp.int32)


def _outproj_router(x2d, o, w, gain, r, tm, group=2):
    T = x2d.shape[0]
    n_tt = T // tm
    lr = _local_rows(tm)
    tg = group * tm
    return pl.pallas_call(
        functools.partial(_outproj_router_body, tm=tm),
        grid=(n_tt // group,),
        in_specs=[
            pl.BlockSpec((tg, D_MODEL), lambda i: (i, 0)),
            pl.BlockSpec((tg, D_MODEL), lambda i: (i, 0)),
            pl.BlockSpec((D_MODEL, D_MODEL), lambda i: (0, 0)),
            pl.BlockSpec((1, D_MODEL), lambda i: (0, 0)),
            pl.BlockSpec((D_MODEL, LANES), lambda i: (0, 0)),
        ],
        out_specs=[
            pl.BlockSpec((tg, D_MODEL), lambda i: (i, 0)),
            pl.BlockSpec((group * lr, D_MODEL), lambda i: (i, 0)),
            pl.BlockSpec((tg, LANES), lambda i: (i, 0)),
            pl.BlockSpec((tg, LANES), lambda i: (i, 0)),
            pl.BlockSpec((group * SUBLANES, LANES), lambda i: (i, 0)),
        ],
        out_shape=[
            jax.ShapeDtypeStruct((T, D_MODEL), F32),
            jax.ShapeDtypeStruct((n_tt * lr, D_MODEL), F32),
            jax.ShapeDtypeStruct((T, LANES), jnp.int32),
            jax.ShapeDtypeStruct((T, LANES), F32),
            jax.ShapeDtypeStruct((n_tt * SUBLANES, LANES), jnp.int32),
        ],
        compiler_params=_params(("parallel",)),
        name="outproj_router",
    )(x2d, o, w, gain, r)


def _run_copies(n, src, s0, dst, d0, sem):
    out = []
    for b in SEG_BITS:
        offs = (n >> (b + 1)) << (b + 1)
        cp = pltpu.make_async_copy(src.at[pl.ds(pl.multiple_of(s0 + offs, SEG_ALIGN), 1 << b), :],
                                   dst.at[pl.ds(pl.multiple_of(d0 + offs, SEG_ALIGN), 1 << b), :], sem)
        out.append((((n >> b) & 1) == 1, cp))
    return out


def _experts_body(te_ref, nu_ref, r0_ref, jlo_ref, jhi_ref, valid_ref, cs_ref, lp_ref, src_ref,
                  xl_hbm, wg_ref, wu_ref, wd_ref, out_ref, xbuf, xb_scr, a_scr, sem, *, n_tt):
    i = pl.program_id(0)
    f = pl.program_id(1)
    tm = out_ref.shape[0]

    def move(tile, slot, wait):
        base = te_ref[tile] * n_tt
        r0 = r0_ref[tile]

        def one_run(j, carry):
            c0 = cs_ref[base + j]
            lo = jnp.maximum(c0, r0)
            hi = jnp.minimum(c0 + lp_ref[base + j], r0 + tm)
            n = jnp.maximum(hi - lo, 0)
            for cond, cp in _run_copies(n, xl_hbm, src_ref[base + j] + (lo - c0), xbuf.at[slot], lo - r0, sem.at[slot]):
                @pl.when(cond)
                def _():
                    cp.wait() if wait else cp.start()
            return carry

        lax.fori_loop(jlo_ref[tile], jhi_ref[tile], one_run, 0)

    @pl.when(f == 0)
    def _():
        @pl.when(i == 0)
        def _():
            xbuf[...] = jnp.zeros_like(xbuf)
            move(0, 0, False)

        @pl.when(i < nu_ref[0])
        def _():
            move(i, i % 2, True)

        @pl.when(i + 1 < nu_ref[0])
        def _():
            move(i + 1, (i + 1) % 2, False)

        row = lax.broadcasted_iota(jnp.int32, (tm, D_MODEL), 0)
        xb_scr[...] = jnp.where(row < valid_ref[i], xbuf[i % 2], 0.0).astype(BF16)

        @pl.when(i >= nu_ref[0])
        def _():
            out_ref[...] = jnp.zeros_like(out_ref)

    def swiglu(first):
        xb = xb_scr[...]
        for c in range(a_scr.shape[1] // MXU):
            cols = slice(c * MXU, (c + 1) * MXU)
            g = _dot(xb, wg_ref[:, cols])
            u = _dot(xb, wu_ref[:, cols])
            a_scr[:, cols] = (g * _sigmoid(g) * u).astype(BF16)
        d = _dot(a_scr[...], wd_ref[...])
        out_ref[...] = d if first else out_ref[...] + d

    @pl.when((i < nu_ref[0]) & (f == 0))
    def _():
        swiglu(True)

    @pl.when((i < nu_ref[0]) & (f != 0))
    def _():
        swiglu(False)


def _moe_experts(tabs, xl, wg, wu, wd, n_rows, n_tt, tm, tf):
    F = wg.shape[2]
    nf = F // tf

    def f_eff(i, f, nu):
        return jnp.where(i < nu[0], f, nf - 1)

    return pl.pallas_call(
        functools.partial(_experts_body, n_tt=n_tt),
        grid_spec=pltpu.PrefetchScalarGridSpec(
            num_scalar_prefetch=len(tabs),
            grid=(n_rows // tm, nf),
            in_specs=[
                pl.BlockSpec(memory_space=pl.ANY),
                pl.BlockSpec((None, D_MODEL, tf), lambda i, f, te, nu, *_: (te[i], 0, f_eff(i, f, nu))),
                pl.BlockSpec((None, D_MODEL, tf), lambda i, f, te, nu, *_: (te[i], 0, f_eff(i, f, nu))),
                pl.BlockSpec((None, tf, D_MODEL), lambda i, f, te, nu, *_: (te[i], f_eff(i, f, nu), 0)),
            ],
            out_specs=pl.BlockSpec((tm, D_MODEL), lambda i, f, *_: (i, 0)),
            scratch_shapes=[pltpu.VMEM((2, tm, D_MODEL), F32), pltpu.VMEM((tm, D_MODEL), BF16),
                            pltpu.VMEM((tm, tf), BF16), pltpu.SemaphoreType.DMA((2,))],
        ),
        out_shape=jax.ShapeDtypeStruct((n_rows, D_MODEL), F32),
        compiler_params=_params(("arbitrary", "arbitrary")),
        name="moe_experts",
    )(*tabs, xl, wg, wu, wd)


def _combine_body(row_ref, lp_ref, off_ref, x_ref, pos_ref, gate_ref, ys_hbm, gain_ref, out_ref, yl, sem):
    j = pl.program_id(0)
    tm = x_ref.shape[0]
    lr = yl.shape[1]

    def move(tile, slot, wait):
        for e in range(N_EXPERTS):
            t = tile * N_EXPERTS + e
            for cond, cp in _run_copies(lp_ref[t], ys_hbm, row_ref[t], yl.at[slot], off_ref[t], sem.at[slot]):
                @pl.when(cond)
                def _():
                    cp.wait() if wait else cp.start()

    @pl.when(j == 0)
    def _():
        yl[...] = jnp.zeros_like(yl)
        move(0, 0, False)

    move(j, j % 2, True)

    @pl.when(j + 1 < pl.num_programs(0))
    def _():
        move(j + 1, (j + 1) % 2, False)

    last = j * N_EXPERTS + N_EXPERTS - 1
    used = off_ref[last] + lp_ref[last]
    row = lax.broadcasted_iota(jnp.int32, (lr, D_MODEL), 0)
    y_sorted = jnp.where(row < used, yl[j % 2], 0.0).astype(BF16)
    pos = pos_ref[...]
    gates = gate_ref[...]
    slot = lax.broadcasted_iota(jnp.int32, (tm, lr), 1)
    weights = jnp.where(slot == pos[:, 0:1], gates[:, 0:1], jnp.where(slot == pos[:, 1:2], gates[:, 1:2], 0.0))
    y = x_ref[...] + _dot(weights.astype(BF16), y_sorted)
    out_ref[...] = _rmsnorm(y, gain_ref[...])


def _moe_combine(tabs, x3, pos, gates, ys, gain, tm):
    T = x3.shape[0]
    return pl.pallas_call(
        _combine_body,
        grid_spec=pltpu.PrefetchScalarGridSpec(
            num_scalar_prefetch=len(tabs),
            grid=(T // tm,),
            in_specs=[
                pl.BlockSpec((tm, D_MODEL), lambda i, *_: (i, 0)),
                pl.BlockSpec((tm, LANES), lambda i, *_: (i, 0)),
                pl.BlockSpec((tm, LANES), lambda i, *_: (i, 0)),
                pl.BlockSpec(memory_space=pl.ANY),
                pl.BlockSpec((1, D_MODEL), lambda i, *_: (0, 0)),
            ],
            out_specs=pl.BlockSpec((tm, D_MODEL), lambda i, *_: (i, 0)),
            scratch_shapes=[pltpu.VMEM((2, _local_rows(tm), D_MODEL), F32), pltpu.SemaphoreType.DMA((2,))],
        ),
        out_shape=jax.ShapeDtypeStruct((T, D_MODEL), F32),
        compiler_params=_params(("arbitrary",)),
        name="moe_combine",
    )(*tabs, x3, pos, gates, ys, gain)


def _routing_tables(seg, n_tt, tm, tm_e):
    seg = seg.reshape(n_tt, SUBLANES, LANES)
    lp = seg[:, 1, :N_EXPERTS]
    off = seg[:, 2, :N_EXPERTS]
    cs = jnp.cumsum(lp, axis=0) - lp
    total = jnp.sum(lp, axis=0)
    padded = ((total + tm_e - 1) // tm_e) * tm_e
    ends = jnp.cumsum(padded)
    starts = ends - padded
    n_rows = -(-(TOP_K * n_tt * tm + N_EXPERTS * (SEG_ALIGN - 1) * n_tt) // tm_e) * tm_e + N_EXPERTS * tm_e
    tile_start = jnp.arange(n_rows // tm_e, dtype=jnp.int32) * tm_e
    te = jnp.minimum(jnp.sum((tile_start[:, None] >= ends[None, :]).astype(jnp.int32), axis=1), N_EXPERTS - 1)
    n_used = (ends[-1] // tm_e).astype(jnp.int32).reshape(1)
    r0 = tile_start - starts[te]
    cs_t = cs[:, te]
    run_end_t = cs_t + lp[:, te]
    jlo = jnp.sum((run_end_t <= r0[None, :]).astype(jnp.int32), axis=0)
    jhi = jnp.sum((cs_t < (r0 + tm_e)[None, :]).astype(jnp.int32), axis=0)
    valid = jnp.clip(total[te] - r0, 0, tm_e)
    src = jnp.arange(n_tt, dtype=jnp.int32)[:, None] * _local_rows(tm) + off
    i32 = lambda a: a.astype(jnp.int32)
    expert_tabs = (i32(te), n_used, i32(r0), i32(jlo), i32(jhi), i32(valid),
                   i32(cs.T.reshape(-1)), i32(lp.T.reshape(-1)), i32(src.T.reshape(-1)))
    combine_tabs = (i32((starts[None, :] + cs).reshape(-1)), i32(lp.reshape(-1)), i32(off.reshape(-1)))
    return expert_tabs, combine_tabs, n_rows


def _rope_lanes():
    inv_freq = jnp.power(jnp.float32(ROPE_THETA), -jnp.arange(ROPE_HALF, dtype=F32) / ROPE_HALF)
    rest = jnp.zeros((HEAD_DIM - ROPE_DIM,), F32)
    freq = jnp.concatenate([inv_freq, inv_freq, rest])
    sign = jnp.concatenate([-jnp.ones((ROPE_HALF,), F32), jnp.ones((ROPE_HALF,), F32), rest])
    reps = LANES // HEAD_DIM
    return jnp.tile(freq, reps).reshape(1, LANES), jnp.tile(sign, reps).reshape(1, LANES)


def kernel(x, positions, norm_mix, norm_ffn, w_in_ab, fgate_bias, w_out_ab, w_in_c, lower_bounds, gnorm_c, w_out_c,
           w_gate_ffn, w_up_ffn, w_down_ffn, router, w_gate_moe, w_up_moe, w_down_moe, norm_final):
    B, S, D = x.shape
    T = B * S
    assert D == D_MODEL and S % PROJ_ROWS == 0
    tm_proj, tm, tm_e = PROJ_ROWS, ROUTE_ROWS, EXPERT_ROWS
    x2d = x.reshape(T, D)

    w_ab = jnp.pad(w_in_ab[0], ((0, 0), (0, QKV_WIDTH + LANES - w_in_ab.shape[2]))).astype(BF16)
    freq, sign = _rope_lanes()
    bias = jnp.pad(fgate_bias[0], (0, LANES - N_HEADS_B)).reshape(1, LANES)
    qkv, cum = _inproj_ab(x2d, norm_mix[0:1], w_ab, positions.reshape(T, 1), freq, sign, bias, tm_proj, S)
    qkv = qkv.reshape(B, S, QKV_WIDTH)
    out_a = _dilated_attention(qkv, B, S).reshape(T, WIDTH_A)
    out_b = _fox_attention(qkv, cum.reshape(B, S, LANES), B, S).reshape(T, WIDTH_B)
    w_o = w_out_ab[0].astype(BF16)
    x2 = _outproj_ffn(x2d, out_a, out_b, w_o[:WIDTH_A], w_o[WIDTH_A:], norm_ffn[0:1],
                      w_gate_ffn[0].astype(BF16), w_up_ffn[0].astype(BF16), w_down_ffn[0].astype(BF16), tm_proj)

    lb_all = jnp.cumsum(jax.nn.softmax(lower_bounds.astype(F32), axis=0), axis=0)
    lb = (lb_all - lb_all[0:1])[1].reshape(1, D)
    wq, wf, wi, wg = jnp.split(w_in_c[0], 4, axis=-1)
    w_c = jnp.concatenate([wq, wi, wg, wf], axis=-1).astype(BF16)
    qig, flog_c = _inproj_c(x2, norm_mix[1:2], w_c, tm_proj)
    o_c = _hgrn(qig.reshape(B, S, 3 * D), flog_c.reshape(B, S, D), lb, gnorm_c[0:1], B, S).reshape(T, D)

    r_pad = jnp.pad(router[0], ((0, 0), (0, LANES - N_EXPERTS))).astype(BF16)
    x3, xl, pos, gates, seg = _outproj_router(x2, o_c, w_out_c[0].astype(BF16), norm_ffn[1:2], r_pad, tm)
    expert_tabs, combine_tabs, n_rows = _routing_tables(seg, T // tm, tm, tm_e)
    ys = _moe_experts(expert_tabs, xl, w_gate_moe[0].astype(BF16), w_up_moe[0].astype(BF16),
                      w_down_moe[0].astype(BF16), n_rows, T // tm, tm_e, w_gate_moe.shape[3] // 2)
    out = _moe_combine(combine_tabs, x3, pos, gates, ys, norm_final.reshape(1, D), tm)
    return out.reshape(B, S, D)
```

```python
import functools

import jax
import jax.numpy as jnp
import numpy as np
from jax import lax
from jax.experimental import pallas as pl
from jax.experimental.pallas import tpu as pltpu

F32 = jnp.float32
BF16 = jnp.bfloat16

D_MODEL = 1024
HEAD_DIM = 64
N_HEADS_A = 8
N_HEADS_B = 8
WIDTH_A = N_HEADS_A * HEAD_DIM
WIDTH_B = N_HEADS_B * HEAD_DIM
QKV_WIDTH = 3 * (WIDTH_A + WIDTH_B)
ROPE_THETA = 500000.0
ROPE_DIM = HEAD_DIM // 4
ROPE_HALF = ROPE_DIM // 2
ATT_BLOCK = 128
DILATIONS = (1, 4, 16)
N_HEADS_C = 8
HGRN_DK = 128
HGRN_CHUNK = 64
HGRN_LEAF = 16
N_EXPERTS = 8
TOP_K = 2
EPS = 1e-6
LOG2E = 1.4426950408889634

LANES = 128
SUBLANES = 8
MXU = 256
VMEM_LIMIT = 56 * 1024 * 1024
PROJ_ROWS = 1024
ROUTE_ROWS = 512
EXPERT_ROWS = 1024

NT_DIMS = (((1,), (1,)), ((), ()))


def _params(semantics, **kw):
    return pltpu.CompilerParams(dimension_semantics=semantics, vmem_limit_bytes=VMEM_LIMIT, **kw)


def _rmsnorm(x, gain):
    return x * lax.rsqrt(jnp.mean(x * x, axis=-1, keepdims=True) + EPS) * gain


def _sigmoid(x):
    return 1.0 / (1.0 + jnp.exp(-x))


def _split3(x):
    hi = x.astype(BF16)
    r1 = x - hi.astype(F32)
    mid = r1.astype(BF16)
    lo = (r1 - mid.astype(F32)).astype(BF16)
    return hi, mid, lo


def _dot(a, b):
    return jnp.dot(a, b, preferred_element_type=F32)


def _dot_nt(a, b):
    return lax.dot_general(a, b, NT_DIMS, preferred_element_type=F32)


def _cumsum_groups(x):
    n, w = x.shape
    rows = lax.broadcasted_iota(jnp.int32, (SUBLANES, w), 0)
    out, carry = [], None
    for g in range(n // SUBLANES):
        xg = x[SUBLANES * g:SUBLANES * (g + 1), :]
        for s in (1, 2, 4):
            xg = xg + jnp.where(rows >= s, pltpu.roll(xg, s, 0), 0.0)
        if carry is not None:
            xg = xg + carry
        carry = xg[SUBLANES - 1:SUBLANES, :]
        out.append(xg)
    return jnp.concatenate(out, axis=0)


def _inproj_ab_body(x_ref, gain_ref, w_ref, pos_ref, freq_ref, sign_ref, bias_ref, qkv_ref, cum_ref, carry_scr, *, tiles_per_seq):
    h = _rmsnorm(x_ref[...], gain_ref[...]).astype(BF16)

    @pl.when(pl.program_id(0) % tiles_per_seq == 0)
    def _():
        carry_scr[...] = jnp.zeros_like(carry_scr)

    x = _dot(h, w_ref[:, QKV_WIDTH:QKV_WIDTH + LANES]) + bias_ref[...]
    logf = -(jnp.maximum(-x, 0.0) + jnp.log1p(jnp.exp(-jnp.abs(x))))
    cum = _cumsum_groups(logf * LOG2E) + carry_scr[0:1, :]
    cum_ref[...] = cum
    carry_scr[0:1, :] = cum[cum.shape[0] - 1:, :]

    half = x_ref.shape[0] // 2
    lane_h = lax.broadcasted_iota(jnp.int32, (half, LANES), 1)
    left = lane_h < HEAD_DIM
    pos = pos_ref[...].astype(F32)
    ang = jnp.where(left, pos[:half], pos[half:]) * freq_ref[...]
    cos2 = jnp.cos(ang)
    sin2 = jnp.sin(ang) * sign_ref[...]
    cos2_sw = pltpu.roll(cos2, HEAD_DIM, 1)
    sin2_sw = pltpu.roll(sin2, HEAD_DIM, 1)
    cos = jnp.concatenate([jnp.where(left, cos2, cos2_sw), jnp.where(left, cos2_sw, cos2)], axis=0)
    sin = jnp.concatenate([jnp.where(left, sin2, sin2_sw), jnp.where(left, sin2_sw, sin2)], axis=0)
    lane = lax.broadcasted_iota(jnp.int32, cos.shape, 1)
    low = (lane & (HEAD_DIM - 1)) < ROPE_HALF
    scale = HEAD_DIM ** -0.5 * LOG2E
    for c in range(QKV_WIDTH // MXU):
        y = _dot(h, w_ref[:, c * MXU:(c + 1) * MXU])
        seg = c // 2
        for s in range(2):
            yy = y[:, s * LANES:(s + 1) * LANES]
            if seg in (0, 1):
                partner = jnp.where(low, pltpu.roll(yy, LANES - ROPE_HALF, 1), pltpu.roll(yy, ROPE_HALF, 1))
                yy = yy * cos + partner * sin
            if seg in (0, 3):
                yy = yy * scale
            qkv_ref[:, c * MXU + s * LANES:c * MXU + (s + 1) * LANES] = yy.astype(BF16)


def _inproj_ab(x2d, gain, w, pos, freq, sign, bias, tm, seq_len):
    T = x2d.shape[0]
    wn = w.shape[1]
    return pl.pallas_call(
        functools.partial(_inproj_ab_body, tiles_per_seq=seq_len // tm),
        grid=(T // tm,),
        in_specs=[
            pl.BlockSpec((tm, D_MODEL), lambda i: (i, 0)),
            pl.BlockSpec((1, D_MODEL), lambda i: (0, 0)),
            pl.BlockSpec((D_MODEL, wn), lambda i: (0, 0)),
            pl.BlockSpec((tm, 1), lambda i: (i, 0)),
            pl.BlockSpec((1, LANES), lambda i: (0, 0)),
            pl.BlockSpec((1, LANES), lambda i: (0, 0)),
            pl.BlockSpec((1, LANES), lambda i: (0, 0)),
        ],
        out_specs=[
            pl.BlockSpec((tm, QKV_WIDTH), lambda i: (i, 0)),
            pl.BlockSpec((tm, LANES), lambda i: (i, 0)),
        ],
        out_shape=[
            jax.ShapeDtypeStruct((T, QKV_WIDTH), BF16),
            jax.ShapeDtypeStruct((T, LANES), F32),
        ],
        scratch_shapes=[pltpu.VMEM((SUBLANES, LANES), F32)],
        compiler_params=_params(("arbitrary",)),
        name="inproj_ab",
    )(x2d, gain, w, pos, freq, sign, bias)


def _dilated_body(q_ref, k_ref, v_ref, o_ref, qf, kf, vf, ob, lb):
    S = q_ref.shape[0]
    nb = ATT_BLOCK
    qf[...] = q_ref[...].astype(F32)
    kf[...] = k_ref[...].astype(F32)
    vf[...] = v_ref[...].astype(F32)
    head0 = lax.broadcasted_iota(jnp.int32, (nb, LANES), 1) < HEAD_DIM
    qi2 = lax.broadcasted_iota(jnp.int32, (2 * nb, 2 * nb), 0) & (nb - 1)
    kj2 = lax.broadcasted_iota(jnp.int32, (2 * nb, 2 * nb), 1)
    valid2 = (kj2 >= qi2) & (kj2 <= qi2 + nb)
    qi1 = lax.broadcasted_iota(jnp.int32, (2 * nb, nb), 0) & (nb - 1)
    kj1 = lax.broadcasted_iota(jnp.int32, (2 * nb, nb), 1)
    valid1 = kj1 <= qi1

    def rows(start, size, r):
        return pl.ds(start, size) if r == 1 else pl.ds(start, size, stride=r)

    def block(br, r, q0, k0, nk):
        qs = qf[rows(q0, nb, r), :]
        kb = kf[rows(k0, nk, r), :].astype(BF16)
        vb = vf[rows(k0, nk, r), :].astype(BF16)
        q2 = jnp.concatenate([jnp.where(head0, qs, 0.0), jnp.where(head0, 0.0, qs)], axis=0).astype(BF16)
        s = jnp.where(valid2 if nk == 2 * nb else valid1, _dot_nt(q2, kb), -jnp.inf)
        m = jnp.max(s, axis=-1, keepdims=True)
        e = jnp.exp2(s - m)
        l = jnp.sum(e, axis=-1, keepdims=True)
        o = _dot(e.astype(BF16), vb) / l
        lse = jnp.broadcast_to(m + jnp.log2(l), (2 * nb, LANES))
        ob[br, rows(q0, nb, r), :] = jnp.where(head0, o[:nb], o[nb:])
        lb[br, rows(q0, nb, r), :] = jnp.where(head0, lse[:nb], lse[nb:])

    for br, r in enumerate(DILATIONS):
        n_blocks = S // (r * nb)
        if n_blocks == 1:
            def first_only(c, carry, br=br, r=r):
                block(br, r, c, c, nb)
                return carry
            lax.fori_loop(0, r, first_only, 0, unroll=True)
        else:
            for c in range(r):
                block(br, r, c, c, nb)

                def later(n, carry, br=br, r=r, c=c):
                    block(br, r, n * (nb * r) + c, (n - 1) * (nb * r) + c, 2 * nb)
                    return carry
                lax.fori_loop(1, n_blocks, later, 0, unroll=True)

    rows_per = 256
    for ch in range(S // rows_per):
        sl = pl.ds(ch * rows_per, rows_per)
        l0, l1, l2 = lb[0, sl, :], lb[1, sl, :], lb[2, sl, :]
        m = jnp.maximum(jnp.maximum(l0, l1), l2)
        w0, w1, w2 = jnp.exp2(l0 - m), jnp.exp2(l1 - m), jnp.exp2(l2 - m)
        o = (w0 * ob[0, sl, :] + w1 * ob[1, sl, :] + w2 * ob[2, sl, :]) / (w0 + w1 + w2)
        o_ref[sl, :] = o.astype(BF16)


def _dilated_attention(qkv, B, S):
    n_pairs = WIDTH_A // LANES
    blk = lambda off: pl.BlockSpec((None, S, LANES), lambda b, p: (b, 0, off + p))
    return pl.pallas_call(
        _dilated_body,
        grid=(B, n_pairs),
        in_specs=[blk(0), blk(n_pairs), blk(2 * n_pairs)],
        out_specs=pl.BlockSpec((None, S, LANES), lambda b, p: (b, 0, p)),
        out_shape=jax.ShapeDtypeStruct((B, S, WIDTH_A), BF16),
        scratch_shapes=[
            pltpu.VMEM((S, LANES), F32), pltpu.VMEM((S, LANES), F32), pltpu.VMEM((S, LANES), F32),
            pltpu.VMEM((len(DILATIONS), S, LANES), F32), pltpu.VMEM((len(DILATIONS), S, LANES), F32),
        ],
        compiler_params=_params(("parallel", "parallel")),
        name="dilated_attention",
    )(qkv, qkv, qkv)


def _fox_body(q_ref, k_ref, v_ref, c_ref, sel_ref, o_ref, qa_scr, ka_scr, *, tk):
    S = q_ref.shape[0]
    rows_per = 256
    lane = lax.broadcasted_iota(jnp.int32, (rows_per, LANES), 1)

    def build(i, carry):
        sl = pl.ds(i * rows_per, rows_per)
        q = q_ref[sl, :].astype(F32)
        k = k_ref[sl, :].astype(F32)
        extra = _dot(jnp.concatenate(_split3(c_ref[sl, :]), axis=1), sel_ref[...])
        for hh in range(2):
            own = (lane < HEAD_DIM) if hh == 0 else (lane >= HEAD_DIM)
            a0 = HEAD_DIM if hh == 0 else 0
            first = (lane >= a0) & (lane < a0 + 3)
            second = (lane >= a0 + 3) & (lane < a0 + 6)
            ex = extra[:, hh * LANES:(hh + 1) * LANES]
            qaug = jnp.where(own, q, jnp.where(first, ex, jnp.where(second, 1.0, 0.0)))
            kaug = jnp.where(own, k, jnp.where(second, ex, jnp.where(first, 1.0, 0.0)))
            qa_scr[hh, sl, :] = qaug.astype(BF16)
            ka_scr[hh, sl, :] = kaug.astype(BF16)
        return carry

    lax.fori_loop(0, S // rows_per, build, 0, unroll=True)

    n_blocks = S // tk
    row_t = lax.broadcasted_iota(jnp.int32, (tk, tk), 0)
    col_t = lax.broadcasted_iota(jnp.int32, (tk, tk), 1)
    causal = col_t <= row_t
    head0 = lax.broadcasted_iota(jnp.int32, (tk, LANES), 1) < HEAD_DIM
    state = [[None] * n_blocks for _ in range(2)]
    for j in range(n_blocks):
        ksl = pl.ds(j * tk, tk)
        vblk = v_ref[ksl, :]
        for hh in range(2):
            s_all = _dot_nt(qa_scr[hh, pl.ds(j * tk, S - j * tk), :], ka_scr[hh, ksl, :])
            es, scales = [], []
            for rb in range(j, n_blocks):
                s = s_all[(rb - j) * tk:(rb - j + 1) * tk, :]
                if rb == j:
                    s = jnp.where(causal, s, -jnp.inf)
                if j == 0:
                    m_new = jnp.max(s, axis=-1, keepdims=True)
                    alpha = None
                else:
                    m_old = state[hh][rb][0]
                    m_new = jnp.maximum(m_old, jnp.max(s, axis=-1, keepdims=True))
                    alpha = jnp.exp2(m_old - m_new)
                e = jnp.exp2(s - m_new)
                es.append(e.astype(BF16))
                scales.append((m_new, alpha, jnp.sum(e, axis=-1, keepdims=True)))
            pv_all = _dot(jnp.concatenate(es, axis=0) if len(es) > 1 else es[0], vblk)
            for rb in range(j, n_blocks):
                m_new, alpha, rowsum = scales[rb - j]
                pv = pv_all[(rb - j) * tk:(rb - j + 1) * tk, :]
                if alpha is None:
                    state[hh][rb] = (m_new, rowsum, pv)
                else:
                    _, l_old, acc_old = state[hh][rb]
                    state[hh][rb] = (m_new, alpha * l_old + rowsum, alpha * acc_old + pv)
        outs = [state[hh][j][2] / state[hh][j][1] for hh in range(2)]
        o_ref[pl.ds(j * tk, tk), :] = jnp.where(head0, outs[0], outs[1]).astype(BF16)


def _fox_selection(n_pairs):
    sel = np.zeros((n_pairs, 3 * LANES, 2 * LANES), np.float32)
    for p in range(n_pairs):
        for hh in range(2):
            a0 = hh * LANES + (HEAD_DIM if hh == 0 else 0)
            for piece in range(3):
                sel[p, piece * LANES + 2 * p + hh, a0 + piece] = 1.0
                sel[p, piece * LANES + 2 * p + hh, a0 + 3 + piece] = -1.0
    return jnp.asarray(sel, BF16)


def _fox_attention(qkv, cum, B, S, tk=256):
    n_pairs = WIDTH_B // LANES
    base = 3 * (WIDTH_A // LANES)
    blk = lambda off: pl.BlockSpec((None, S, LANES), lambda b, p: (b, 0, off + p))
    return pl.pallas_call(
        functools.partial(_fox_body, tk=tk),
        grid=(B, n_pairs),
        in_specs=[
            blk(base), blk(base + n_pairs), blk(base + 2 * n_pairs),
            pl.BlockSpec((None, S, LANES), lambda b, p: (b, 0, 0)),
            pl.BlockSpec((None, 3 * LANES, 2 * LANES), lambda b, p: (p, 0, 0)),
        ],
        out_specs=pl.BlockSpec((None, S, LANES), lambda b, p: (b, 0, p)),
        out_shape=jax.ShapeDtypeStruct((B, S, WIDTH_B), BF16),
        scratch_shapes=[pltpu.VMEM((2, S, LANES), BF16), pltpu.VMEM((2, S, LANES), BF16)],
        compiler_params=_params(("parallel", "parallel")),
        name="fox_attention",
    )(qkv, qkv, qkv, cum, _fox_selection(n_pairs))


def _outproj_ffn_body(x_ref, oa_ref, ob_ref, woa_ref, wob_ref, gain_ref, wg_ref, wu_ref, wd_ref, out_ref, a_scr):
    x1 = x_ref[...] + _dot(oa_ref[...], woa_ref[...]) + _dot(ob_ref[...], wob_ref[...])
    h = _rmsnorm(x1, gain_ref[...]).astype(BF16)
    for c in range(a_scr.shape[1] // MXU):
        cols = slice(c * MXU, (c + 1) * MXU)
        g = _dot(h, wg_ref[:, cols])
        u = _dot(h, wu_ref[:, cols])
        a_scr[:, cols] = (g * _sigmoid(g) * u).astype(BF16)
    out_ref[...] = x1 + _dot(a_scr[...], wd_ref[...])


def _outproj_ffn(x2d, oa, ob, woa, wob, gain, wg, wu, wd, tm):
    T = x2d.shape[0]
    F = wg.shape[1]
    once = dict(pipeline_mode=pl.Buffered(1))
    return pl.pallas_call(
        _outproj_ffn_body,
        grid=(T // tm,),
        in_specs=[
            pl.BlockSpec((tm, D_MODEL), lambda i: (i, 0)),
            pl.BlockSpec((tm, WIDTH_A), lambda i: (i, 0)),
            pl.BlockSpec((tm, WIDTH_B), lambda i: (i, 0)),
            pl.BlockSpec((WIDTH_A, D_MODEL), lambda i: (0, 0), **once),
            pl.BlockSpec((WIDTH_B, D_MODEL), lambda i: (0, 0), **once),
            pl.BlockSpec((1, D_MODEL), lambda i: (0, 0)),
            pl.BlockSpec((D_MODEL, F), lambda i: (0, 0), **once),
            pl.BlockSpec((D_MODEL, F), lambda i: (0, 0), **once),
            pl.BlockSpec((F, D_MODEL), lambda i: (0, 0), **once),
        ],
        out_specs=pl.BlockSpec((tm, D_MODEL), lambda i: (i, 0)),
        out_shape=jax.ShapeDtypeStruct((T, D_MODEL), F32),
        scratch_shapes=[pltpu.VMEM((tm, F), BF16)],
        compiler_params=_params(("parallel",)),
        name="outproj_ffn",
    )(x2d, oa, ob, woa, wob, gain, wg, wu, wd)


def _inproj_c_body(x_ref, gain_ref, w_ref, qig_ref, f_ref):
    h = _rmsnorm(x_ref[...], gain_ref[...]).astype(BF16)
    n_qig = qig_ref.shape[1]
    for c in range(n_qig // MXU):
        qig_ref[:, c * MXU:(c + 1) * MXU] = _dot(h, w_ref[:, c * MXU:(c + 1) * MXU]).astype(BF16)
    for c in range(f_ref.shape[1] // MXU):
        f_ref[:, c * MXU:(c + 1) * MXU] = _dot(h, w_ref[:, n_qig + c * MXU:n_qig + (c + 1) * MXU])


def _inproj_c(x2d, gain, w, tm):
    T = x2d.shape[0]
    return pl.pallas_call(
        _inproj_c_body,
        grid=(T // tm,),
        in_specs=[
            pl.BlockSpec((tm, D_MODEL), lambda i: (i, 0)),
            pl.BlockSpec((1, D_MODEL), lambda i: (0, 0)),
            pl.BlockSpec((D_MODEL, 4 * D_MODEL), lambda i: (0, 0)),
        ],
        out_specs=[
            pl.BlockSpec((tm, 3 * D_MODEL), lambda i: (i, 0)),
            pl.BlockSpec((tm, D_MODEL), lambda i: (i, 0)),
        ],
        out_shape=[
            jax.ShapeDtypeStruct((T, 3 * D_MODEL), BF16),
            jax.ShapeDtypeStruct((T, D_MODEL), F32),
        ],
        compiler_params=_params(("parallel",)),
        name="inproj_c",
    )(x2d, gain, w)


def _hgrn_body(q_ref, i_ref, g_ref, f_ref, lb_ref, gn_ref, o_ref, state_scr, o_scr, ops0, ops1, dec0, dec1, *, heads):
    S = q_ref.shape[0]
    C = HGRN_CHUNK
    half = C // 2
    quarter = C // 4
    assert quarter == HGRN_LEAF
    n_chunks = S // C
    row = lax.broadcasted_iota(jnp.int32, (C, C), 0)
    col = lax.broadcasted_iota(jnp.int32, (C, C), 1)
    mask_cross = (row >= half) & (col < half)
    mask_same = ((row // half) == (col // half)) & (col <= row)
    r = lax.broadcasted_iota(jnp.int32, (C, heads * HGRN_DK), 0)
    scale = HGRN_DK ** -0.5
    gn = jnp.concatenate([gn_ref[...]] * heads, axis=1)
    state_scr[...] = jnp.zeros_like(state_scr)

    def prepare(ci, ops, dec):
        sl = pl.ds(ci * C, C)
        lbv = lb_ref[...]
        f = lbv + (1.0 - lbv) * _sigmoid(f_ref[sl, :])
        k = 1.0 - f
        q = q_ref[sl, :].astype(F32) * scale
        b = _cumsum_groups(jnp.log2(f))
        b_last = b[C - 1:C, :]
        e_cross = b - b[half - 1:half, :]
        e_same = b - jnp.where(r < half, b[quarter - 1:quarter, :], b[half + quarter - 1:half + quarter, :])
        ops[0] = (q * jnp.exp2(jnp.minimum(e_cross, 0.0))).astype(BF16)
        ops[1] = (k * jnp.exp2(jnp.minimum(-e_cross, 0.0))).astype(BF16)
        ops[2] = (q * jnp.exp2(e_same)).astype(BF16)
        ops[3] = (k * jnp.exp2(-e_same)).astype(BF16)
        ops[4] = (q * jnp.exp2(b)).astype(BF16)
        ops[5] = (k * jnp.exp2(b_last - b)).astype(BF16)
        dec[0:1, :] = jnp.exp2(b_last)

    def contract(ci, ops, dec):
        sl = pl.ds(ci * C, C)
        for hd in range(heads):
            cols = slice(hd * HGRN_DK, (hd + 1) * HGRN_DK)
            v = i_ref[sl, cols]
            scores = (jnp.where(mask_cross, _dot_nt(ops[0, :, cols], ops[1, :, cols]), 0.0)
                      + jnp.where(mask_same, _dot_nt(ops[2, :, cols], ops[3, :, cols]), 0.0))
            state_t = state_scr[hd]
            o_scr[sl, cols] = _dot(scores.astype(BF16), v) + _dot_nt(ops[4, :, cols], state_t.astype(BF16))
            v_t = v.astype(F32).T.astype(BF16)
            state_scr[hd] = state_t * dec[0:1, cols] + _dot(v_t, ops[5, :, cols])

    def finish(ci):
        sl = pl.ds(ci * C, C)
        o = o_scr[sl, :]
        ys = []
        for hd in range(heads):
            oh = o[:, hd * HGRN_DK:(hd + 1) * HGRN_DK]
            ys.append(oh * lax.rsqrt(jnp.mean(oh * oh, axis=-1, keepdims=True) + EPS))
        gate = g_ref[sl, :].astype(F32)
        o_ref[sl, :] = (jnp.concatenate(ys, axis=1) * gn * (gate * _sigmoid(gate))).astype(BF16)

    prepare(0, ops0, dec0)

    def pair(j, carry):
        prepare(2 * j + 1, ops1, dec1)
        contract(2 * j, ops0, dec0)
        prepare(2 * j + 2, ops0, dec0)
        contract(2 * j + 1, ops1, dec1)
        finish(jnp.maximum(2 * j - 1, 0))
        finish(2 * j)
        return carry

    lax.fori_loop(0, n_chunks // 2 - 1, pair, 0, unroll=5)
    prepare(n_chunks - 1, ops1, dec1)
    contract(n_chunks - 2, ops0, dec0)
    contract(n_chunks - 1, ops1, dec1)
    for ci in range(n_chunks - 3, n_chunks):
        finish(ci)


def _hgrn(qig, flog, lb, gn, B, S, heads=4):
    ng = N_HEADS_C // heads
    w = heads * HGRN_DK
    blk = lambda off: pl.BlockSpec((None, S, w), lambda b, h: (b, 0, off + h))
    return pl.pallas_call(
        functools.partial(_hgrn_body, heads=heads),
        grid=(B, ng),
        in_specs=[
            blk(0), blk(ng), blk(2 * ng),
            pl.BlockSpec((None, S, w), lambda b, h: (b, 0, h)),
            pl.BlockSpec((1, w), lambda b, h: (0, h)),
            pl.BlockSpec((1, HGRN_DK), lambda b, h: (0, 0)),
        ],
        out_specs=pl.BlockSpec((None, S, w), lambda b, h: (b, 0, h)),
        out_shape=jax.ShapeDtypeStruct((B, S, D_MODEL), BF16),
        scratch_shapes=[
            pltpu.VMEM((heads, HGRN_DK, HGRN_DK), F32), pltpu.VMEM((S, w), F32),
            pltpu.VMEM((6, HGRN_CHUNK, w), BF16), pltpu.VMEM((6, HGRN_CHUNK, w), BF16),
            pltpu.VMEM((SUBLANES, w), F32), pltpu.VMEM((SUBLANES, w), F32),
        ],
        compiler_params=_params(("parallel", "parallel")),
        name="hgrn2",
    )(qig, qig, qig, flog, lb, gn)


SEG_ALIGN = SUBLANES
SEG_BITS = tuple(range(9, 2, -1))


def _local_rows(tm):
    return TOP_K * tm + N_EXPERTS * SEG_ALIGN


def _outproj_router_body(x_ref, o_ref, w_ref, gain_ref, r_ref, x3_ref, xl_ref, pos_ref, gate_ref, seg_ref, *, tm):
    for t in range(x_ref.shape[0] // tm):
        _route_tile(x_ref, o_ref, w_ref, gain_ref, r_ref, x3_ref, xl_ref, pos_ref, gate_ref, seg_ref, t, tm)


def _route_tile(x_ref, o_ref, w_ref, gain_ref, r_ref, x3_ref, xl_ref, pos_ref, gate_ref, seg_ref, t, tm):
    rows = pl.ds(t * tm, tm)
    lr = _local_rows(tm)
    x3 = x_ref[rows, :] + _dot(o_ref[rows, :], w_ref[...])
    x3_ref[rows, :] = x3
    h = _rmsnorm(x3, gain_ref[...])
    h_bf = h.astype(BF16)
    logits = _dot(h_bf, r_ref[...])
    lane = lax.broadcasted_iota(jnp.int32, logits.shape, 1)
    lane_f = lane.astype(F32)
    lg = jnp.where(lane < N_EXPERTS, logits, -jnp.inf)
    m1 = jnp.max(lg, axis=-1, keepdims=True)
    i1 = jnp.min(jnp.where(lg == m1, lane_f, float(LANES)), axis=-1, keepdims=True)
    lg2 = jnp.where(lane_f == i1, -jnp.inf, lg)
    m2 = jnp.max(lg2, axis=-1, keepdims=True)
    i2 = jnp.min(jnp.where(lg2 == m2, lane_f, float(LANES)), axis=-1, keepdims=True)
    e2 = jnp.exp(m2 - m1)
    den = 1.0 + e2
    gate_ref[rows, :] = jnp.where(lane == 0, 1.0 / den, jnp.where(lane == 1, e2 / den, 0.0))

    oh1 = (lane_f == i1).astype(F32)
    oh2 = (lane_f == i2).astype(F32)
    c1 = _cumsum_groups(oh1)
    c2 = _cumsum_groups(oh2)
    n1 = c1[tm - 1:tm, :]
    count = n1 + c2[tm - 1:tm, :]
    padded = jnp.floor((count + (SEG_ALIGN - 1.0)) * (1.0 / SEG_ALIGN)) * SEG_ALIGN
    run = jnp.broadcast_to(padded, (SUBLANES, LANES))
    lane8 = lax.broadcasted_iota(jnp.int32, (SUBLANES, LANES), 1)
    for s in (1, 2, 4):
        run = run + jnp.where(lane8 >= s, pltpu.roll(run, s, 1), 0.0)
    start = run[0:1, :] - padded
    pos1 = jnp.sum(oh1 * (start + c1 - 1.0), axis=-1, keepdims=True)
    pos2 = jnp.sum(oh2 * (start + n1 + c2 - 1.0), axis=-1, keepdims=True)
    pos_ref[rows, :] = jnp.where(lane == 0, pos1, jnp.where(lane == 1, pos2, 0.0)).astype(jnp.int32)
    pos1_row = jnp.broadcast_to(pos1, (tm, LANES)).T[0:1, :]
    pos2_row = jnp.broadcast_to(pos2, (tm, LANES)).T[0:1, :]
    slot = lax.broadcasted_iota(jnp.int32, (lr, tm), 0).astype(F32)
    perm = ((slot == pos1_row) | (slot == pos2_row)).astype(BF16)
    xl_ref[pl.ds(t * lr, lr), :] = _dot(perm, h_bf)
    r8 = lax.broadcasted_iota(jnp.int32, (SUBLANES, LANES), 0)
    seg_ref[pl.ds(t * SUBLANES, SUBLANES), :] = jnp.where(r8 == 0, count, jnp.where(r8 == 1, padded, jnp.where(r8 == 2, start, 0.0))).astype(jnp.int32)


def _outproj_router(x2d, o, w, gain, r, tm, group=2):
    T = x2d.shape[0]
    n_tt = T // tm
    lr = _local_rows(tm)
    tg = group * tm
    return pl.pallas_call(
        functools.partial(_outproj_router_body, tm=tm),
        grid=(n_tt // group,),
        in_specs=[
            pl.BlockSpec((tg, D_MODEL), lambda i: (i, 0)),
            pl.BlockSpec((tg, D_MODEL), lambda i: (i, 0)),
            pl.BlockSpec((D_MODEL, D_MODEL), lambda i: (0, 0)),
            pl.BlockSpec((1, D_MODEL), lambda i: (0, 0)),
            pl.BlockSpec((D_MODEL, LANES), lambda i: (0, 0)),
        ],
        out_specs=[
            pl.BlockSpec((tg, D_MODEL), lambda i: (i, 0)),
            pl.BlockSpec((group * lr, D_MODEL), lambda i: (i, 0)),
            pl.BlockSpec((tg, LANES), lambda i: (i, 0)),
            pl.BlockSpec((tg, LANES), lambda i: (i, 0)),
            pl.BlockSpec((group * SUBLANES, LANES), lambda i: (i, 0)),
        ],
        out_shape=[
            jax.ShapeDtypeStruct((T, D_MODEL), F32),
            jax.ShapeDtypeStruct((n_tt * lr, D_MODEL), F32),
            jax.ShapeDtypeStruct((T, LANES), jnp.int32),
            jax.ShapeDtypeStruct((T, LANES), F32),
            jax.ShapeDtypeStruct((n_tt * SUBLANES, LANES), jnp.int32),
        ],
        compiler_params=_params(("parallel",)),
        name="outproj_router",
    )(x2d, o, w, gain, r)


def _run_copies(n, src, s0, dst, d0, sem):
    out = []
    for b in SEG_BITS:
        offs = (n >> (b + 1)) << (b + 1)
        cp = pltpu.make_async_copy(src.at[pl.ds(pl.multiple_of(s0 + offs, SEG_ALIGN), 1 << b), :],
                                   dst.at[pl.ds(pl.multiple_of(d0 + offs, SEG_ALIGN), 1 << b), :], sem)
        out.append((((n >> b) & 1) == 1, cp))
    return out


def _experts_body(te_ref, nu_ref, r0_ref, jlo_ref, jhi_ref, valid_ref, cs_ref, lp_ref, src_ref,
                  xl_hbm, wg_ref, wu_ref, wd_ref, out_ref, xbuf, xb_scr, a_scr, sem, *, n_tt):
    i = pl.program_id(0)
    f = pl.program_id(1)
    tm = out_ref.shape[0]

    def move(tile, slot, wait):
        base = te_ref[tile] * n_tt
        r0 = r0_ref[tile]

        def one_run(j, carry):
            c0 = cs_ref[base + j]
            lo = jnp.maximum(c0, r0)
            hi = jnp.minimum(c0 + lp_ref[base + j], r0 + tm)
            n = jnp.maximum(hi - lo, 0)
            for cond, cp in _run_copies(n, xl_hbm, src_ref[base + j] + (lo - c0), xbuf.at[slot], lo - r0, sem.at[slot]):
                @pl.when(cond)
                def _():
                    cp.wait() if wait else cp.start()
            return carry

        lax.fori_loop(jlo_ref[tile], jhi_ref[tile], one_run, 0)

    @pl.when(f == 0)
    def _():
        @pl.when(i == 0)
        def _():
            xbuf[...] = jnp.zeros_like(xbuf)
            move(0, 0, False)

        @pl.when(i < nu_ref[0])
        def _():
            move(i, i % 2, True)

        @pl.when(i + 1 < nu_ref[0])
        def _():
            move(i + 1, (i + 1) % 2, False)

        row = lax.broadcasted_iota(jnp.int32, (tm, D_MODEL), 0)
        xb_scr[...] = jnp.where(row < valid_ref[i], xbuf[i % 2], 0.0).astype(BF16)

        @pl.when(i >= nu_ref[0])
        def _():
            out_ref[...] = jnp.zeros_like(out_ref)

    def swiglu(first):
        xb = xb_scr[...]
        for c in range(a_scr.shape[1] // MXU):
            cols = slice(c * MXU, (c + 1) * MXU)
            g = _dot(xb, wg_ref[:, cols])
            u = _dot(xb, wu_ref[:, cols])
            a_scr[:, cols] = (g * _sigmoid(g) * u).astype(BF16)
        d = _dot(a_scr[...], wd_ref[...])
        out_ref[...] = d if first else out_ref[...] + d

    @pl.when((i < nu_ref[0]) & (f == 0))
    def _():
        swiglu(True)

    @pl.when((i < nu_ref[0]) & (f != 0))
    def _():
        swiglu(False)


def _moe_experts(tabs, xl, wg, wu, wd, n_rows, n_tt, tm, tf):
    F = wg.shape[2]
    nf = F // tf

    def f_eff(i, f, nu):
        return jnp.where(i < nu[0], f, nf - 1)

    return pl.pallas_call(
        functools.partial(_experts_body, n_tt=n_tt),
        grid_spec=pltpu.PrefetchScalarGridSpec(
            num_scalar_prefetch=len(tabs),
            grid=(n_rows // tm, nf),
            in_specs=[
                pl.BlockSpec(memory_space=pl.ANY),
                pl.BlockSpec((None, D_MODEL, tf), lambda i, f, te, nu, *_: (te[i], 0, f_eff(i, f, nu))),
                pl.BlockSpec((None, D_MODEL, tf), lambda i, f, te, nu, *_: (te[i], 0, f_eff(i, f, nu))),
                pl.BlockSpec((None, tf, D_MODEL), lambda i, f, te, nu, *_: (te[i], f_eff(i, f, nu), 0)),
            ],
            out_specs=pl.BlockSpec((tm, D_MODEL), lambda i, f, *_: (i, 0)),
            scratch_shapes=[pltpu.VMEM((2, tm, D_MODEL), F32), pltpu.VMEM((tm, D_MODEL), BF16),
                            pltpu.VMEM((tm, tf), BF16), pltpu.SemaphoreType.DMA((2,))],
        ),
        out_shape=jax.ShapeDtypeStruct((n_rows, D_MODEL), F32),
        compiler_params=_params(("arbitrary", "arbitrary")),
        name="moe_experts",
    )(*tabs, xl, wg, wu, wd)


def _combine_body(row_ref, lp_ref, off_ref, x_ref, pos_ref, gate_ref, ys_hbm, gain_ref, out_ref, yl, sem):
    j = pl.program_id(0)
    tm = x_ref.shape[0]
    lr = yl.shape[1]

    def move(tile, slot, wait):
        for e in range(N_EXPERTS):
            t = tile * N_EXPERTS + e
            for cond, cp in _run_copies(lp_ref[t], ys_hbm, row_ref[t], yl.at[slot], off_ref[t], sem.at[slot]):
                @pl.when(cond)
                def _():
                    cp.wait() if wait else cp.start()

    @pl.when(j == 0)
    def _():
        yl[...] = jnp.zeros_like(yl)
        move(0, 0, False)

    move(j, j % 2, True)

    @pl.when(j + 1 < pl.num_programs(0))
    def _():
        move(j + 1, (j + 1) % 2, False)

    last = j * N_EXPERTS + N_EXPERTS - 1
    used = off_ref[last] + lp_ref[last]
    row = lax.broadcasted_iota(jnp.int32, (lr, D_MODEL), 0)
    y_sorted = jnp.where(row < used, yl[j % 2], 0.0).astype(BF16)
    pos = pos_ref[...]
    gates = gate_ref[...]
    slot = lax.broadcasted_iota(jnp.int32, (tm, lr), 1)
    weights = jnp.where(slot == pos[:, 0:1], gates[:, 0:1], jnp.where(slot == pos[:, 1:2], gates[:, 1:2], 0.0))
    y = x_ref[...] + _dot(weights.astype(BF16), y_sorted)
    out_ref[...] = _rmsnorm(y, gain_ref[...])


def _moe_combine(tabs, x3, pos, gates, ys, gain, tm):
    T = x3.shape[0]
    return pl.pallas_call(
        _combine_body,
        grid_spec=pltpu.PrefetchScalarGridSpec(
            num_scalar_prefetch=len(tabs),
            grid=(T // tm,),
            in_specs=[
                pl.BlockSpec((tm, D_MODEL), lambda i, *_: (i, 0)),
                pl.BlockSpec((tm, LANES), lambda i, *_: (i, 0)),
                pl.BlockSpec((tm, LANES), lambda i, *_: (i, 0)),
                pl.BlockSpec(memory_space=pl.ANY),
                pl.BlockSpec((1, D_MODEL), lambda i, *_: (0, 0)),
            ],
            out_specs=pl.BlockSpec((tm, D_MODEL), lambda i, *_: (i, 0)),
            scratch_shapes=[pltpu.VMEM((2, _local_rows(tm), D_MODEL), F32), pltpu.SemaphoreType.DMA((2,))],
        ),
        out_shape=jax.ShapeDtypeStruct((T, D_MODEL), F32),
        compiler_params=_params(("arbitrary",)),
        name="moe_combine",
    )(*tabs, x3, pos, gates, ys, gain)


def _routing_tables(seg, n_tt, tm, tm_e):
    seg = seg.reshape(n_tt, SUBLANES, LANES)
    lp = seg[:, 1, :N_EXPERTS]
    off = seg[:, 2, :N_EXPERTS]
    cs = jnp.cumsum(lp, axis=0) - lp
    total = jnp.sum(lp, axis=0)
    padded = ((total + tm_e - 1) // tm_e) * tm_e
    ends = jnp.cumsum(padded)
    starts = ends - padded
    n_rows = -(-(TOP_K * n_tt * tm + N_EXPERTS * (SEG_ALIGN - 1) * n_tt) // tm_e) * tm_e + N_EXPERTS * tm_e
    tile_start = jnp.arange(n_rows // tm_e, dtype=jnp.int32) * tm_e
    te = jnp.minimum(jnp.sum((tile_start[:, None] >= ends[None, :]).astype(jnp.int32), axis=1), N_EXPERTS - 1)
    n_used = (ends[-1] // tm_e).astype(jnp.int32).reshape(1)
    r0 = tile_start - starts[te]
    cs_t = cs[:, te]
    run_end_t = cs_t + lp[:, te]
    jlo = jnp.sum((run_end_t <= r0[None, :]).astype(jnp.int32), axis=0)
    jhi = jnp.sum((cs_t < (r0 + tm_e)[None, :]).astype(jnp.int32), axis=0)
    valid = jnp.clip(total[te] - r0, 0, tm_e)
    src = jnp.arange(n_tt, dtype=jnp.int32)[:, None] * _local_rows(tm) + off
    i32 = lambda a: a.astype(jnp.int32)
    expert_tabs = (i32(te), n_used, i32(r0), i32(jlo), i32(jhi), i32(valid),
                   i32(cs.T.reshape(-1)), i32(lp.T.reshape(-1)), i32(src.T.reshape(-1)))
    combine_tabs = (i32((starts[None, :] + cs).reshape(-1)), i32(lp.reshape(-1)), i32(off.reshape(-1)))
    return expert_tabs, combine_tabs, n_rows


def _rope_lanes():
    inv_freq = jnp.power(jnp.float32(ROPE_THETA), -jnp.arange(ROPE_HALF, dtype=F32) / ROPE_HALF)
    rest = jnp.zeros((HEAD_DIM - ROPE_DIM,), F32)
    freq = jnp.concatenate([inv_freq, inv_freq, rest])
    sign = jnp.concatenate([-jnp.ones((ROPE_HALF,), F32), jnp.ones((ROPE_HALF,), F32), rest])
    reps = LANES // HEAD_DIM
    return jnp.tile(freq, reps).reshape(1, LANES), jnp.tile(sign, reps).reshape(1, LANES)


def kernel(x, positions, norm_mix, norm_ffn, w_in_ab, fgate_bias, w_out_ab, w_in_c, lower_bounds, gnorm_c, w_out_c,
           w_gate_ffn, w_up_ffn, w_down_ffn, router, w_gate_moe, w_up_moe, w_down_moe, norm_final):
    B, S, D = x.shape
    T = B * S
    assert D == D_MODEL and S % PROJ_ROWS == 0
    tm_proj, tm, tm_e = PROJ_ROWS, ROUTE_ROWS, EXPERT_ROWS
    x2d = x.reshape(T, D)

    w_ab = jnp.pad(w_in_ab[0], ((0, 0), (0, QKV_WIDTH + LANES - w_in_ab.shape[2]))).astype(BF16)
    freq, sign = _rope_lanes()
    bias = jnp.pad(fgate_bias[0], (0, LANES - N_HEADS_B)).reshape(1, LANES)
    qkv, cum = _inproj_ab(x2d, norm_mix[0:1], w_ab, positions.reshape(T, 1), freq, sign, bias, tm_proj, S)
    qkv = qkv.reshape(B, S, QKV_WIDTH)
    out_a = _dilated_attention(qkv, B, S).reshape(T, WIDTH_A)
    out_b = _fox_attention(qkv, cum.reshape(B, S, LANES), B, S).reshape(T, WIDTH_B)
    w_o = w_out_ab[0].astype(BF16)
    x2 = _outproj_ffn(x2d, out_a, out_b, w_o[:WIDTH_A], w_o[WIDTH_A:], norm_ffn[0:1],
                      w_gate_ffn[0].astype(BF16), w_up_ffn[0].astype(BF16), w_down_ffn[0].astype(BF16), tm_proj)

    lb_all = jnp.cumsum(jax.nn.softmax(lower_bounds.astype(F32), axis=0), axis=0)
    lb = (lb_all - lb_all[0:1])[1].reshape(1, D)
    wq, wf, wi, wg = jnp.split(w_in_c[0], 4, axis=-1)
    w_c = jnp.concatenate([wq, wi, wg, wf], axis=-1).astype(BF16)
    qig, flog_c = _inproj_c(x2, norm_mix[1:2], w_c, tm_proj)
    o_c = _hgrn(qig.reshape(B, S, 3 * D), flog_c.reshape(B, S, D), lb, gnorm_c[0:1], B, S).reshape(T, D)

    r_pad = jnp.pad(router[0], ((0, 0), (0, LANES - N_EXPERTS))).astype(BF16)
    x3, xl, pos, gates, seg = _outproj_router(x2, o_c, w_out_c[0].astype(BF16), norm_ffn[1:2], r_pad, tm)
    expert_tabs, combine_tabs, n_rows = _routing_tables(seg, T // tm, tm, tm_e)
    ys = _moe_experts(expert_tabs, xl, w_gate_moe[0].astype(BF16), w_up_moe[0].astype(BF16),
                      w_down_moe[0].astype(BF16), n_rows, T // tm, tm_e, w_gate_moe.shape[3] // 2)
    out = _moe_combine(combine_tabs, x3, pos, gates, ys, norm_final.reshape(1, D), tm)
    return out.reshape(B, S, D)
```
